```python
import jax, jax.numpy as jnp
from jax import lax
import numpy as np

D_MODEL = 1024
BATCH = 4
SEQ = 8192
DEPTH = 2

GRID_W = 64
CTX_LEN = 256
N_MIXERS = 2
D_FF = 4 * D_MODEL
CONV_WIDTH = 31
CONV_DIM = D_MODEL
MLSTM_INNER = 2 * D_MODEL
MLSTM_HEADS = 4
MLSTM_HEAD_DIM = MLSTM_INNER // MLSTM_HEADS
QKV_CONV_WIDTH = 5
CHUNK = 128
N_A = (DEPTH + 1) // 2
N_B = DEPTH // 2
ALPHA = (2 * DEPTH) ** 0.25
BETA = (8 * DEPTH) ** -0.25
LN_EPS = 1e-5

kernel_name = 'hybrid_conformer_mlstm_dit_block'


def layer_norm(x, g, b):
    xf = x.astype(jnp.float32)
    mu = jnp.mean(xf, axis=-1, keepdims=True)
    var = jnp.mean(jnp.square(xf - mu), axis=-1, keepdims=True)
    return ((xf - mu) * lax.rsqrt(var + LN_EPS) * g.astype(jnp.float32) + b.astype(jnp.float32)).astype(x.dtype)


def adaln(cvec, w, b):
    return jnp.split(jax.nn.silu(cvec) @ w + b, 6, axis=-1)


def modulate(x, shift, scale):
    return x * (1 + scale) + shift


def dwconv(x, w):
    return lax.conv_general_dilated(x, w[:, None, :], window_strides=(1,), padding='SAME',
                                    dimension_numbers=('NWC', 'WIO', 'NWC'),
                                    feature_group_count=x.shape[-1])


def sq_relu_mlp(u, w1, w2):
    return jnp.square(jax.nn.relu(u @ w1)) @ w2


def conformer_conv(u, rows, w_pw1, b_pw1, w_dw, b_dw, ln_g, ln_b, w_pw2, b_pw2):
    a = u @ w_pw1 + b_pw1
    a = a[..., :CONV_DIM] * jax.nn.sigmoid(a[..., CONV_DIM:])
    bsz, s, ch = a.shape
    if rows is None:
        a = dwconv(a, w_dw)
    else:
        a = dwconv(a.reshape(bsz * rows, GRID_W, ch), w_dw).reshape(bsz, s, ch)
    a = jax.nn.silu(layer_norm(a + b_dw, ln_g, ln_b))
    return a @ w_pw2 + b_pw2


def mlstm_chunk_scan(q, k, v, i_pre, f_pre, state):
    bsz, nh, s, dh = q.shape
    nc = s // CHUNK

    def to_chunks(t):
        return jnp.moveaxis(t.reshape(bsz, nh, nc, CHUNK, *t.shape[3:]), 2, 0)

    lf = jax.nn.log_sigmoid(f_pre)
    lower = jnp.tril(jnp.ones((CHUNK, CHUNK), dtype=bool))

    def step(carry, xs):
        C, n, m = carry
        qc, kc, vc, ic, lfc = xs
        bcum = jnp.cumsum(lfc, axis=-1)
        dlog = bcum[..., :, None] - bcum[..., None, :] + ic[..., None, :]
        dlog = jnp.where(lower, dlog, -jnp.inf)
        inter = bcum + m[..., None]
        m_row = jnp.maximum(inter, jnp.max(dlog, axis=-1))
        scores = jnp.einsum('bhtd,bhsd->bhts', qc, kc) * jnp.exp(dlog - m_row[..., None])
        w_inter = jnp.exp(inter - m_row)
        num = jnp.einsum('bhts,bhsd->bhtd', scores, vc) + w_inter[..., None] * jnp.einsum('bhtd,bhde->bhte', qc, C)
        den = jnp.sum(scores, axis=-1) + w_inter * jnp.einsum('bhtd,bhd->bht', qc, n)
        h = num / jnp.maximum(jnp.abs(den), jnp.exp(-m_row))[..., None]
        b_last = bcum[..., -1]
        dk = b_last[..., None] - bcum + ic
        m_new = jnp.maximum(b_last + m, jnp.max(dk, axis=-1))
        wk = jnp.exp(dk - m_new[..., None])[..., None] * kc
        decay = jnp.exp(b_last + m - m_new)
        C_new = decay[..., None, None] * C + jnp.einsum('bhsd,bhse->bhde', wk, vc)
        n_new = decay[..., None] * n + jnp.sum(wk, axis=2)
        return (C_new, n_new, m_new), h

    state, h = lax.scan(step, state, (to_chunks(q), to_chunks(k), to_chunks(v), to_chunks(i_pre), to_chunks(lf)))
    h = jnp.moveaxis(h, 0, 2).reshape(bsz, nh, s, dh)
    return h, state


def mlstm_features(u, w_up, w_conv, b_conv, w_q, w_k, w_v, w_gate, b_gate, w_o, b_o):
    up = u @ w_up
    xm, z = up[..., :MLSTM_INNER], up[..., MLSTM_INNER:]
    xc = jax.nn.silu(dwconv(xm, w_conv) + b_conv)
    q, k, v = xc @ w_q, xc @ w_k, xm @ w_v
    gates = jnp.concatenate([q, k, v], axis=-1) @ w_gate + b_gate
    o = jax.nn.sigmoid(xm @ w_o + b_o)
    return xc, z, q, k, v, gates, o


def to_heads(t):
    bsz, s, _ = t.shape
    return jnp.transpose(t.reshape(bsz, s, MLSTM_HEADS, MLSTM_HEAD_DIM), (0, 2, 1, 3)).astype(jnp.float32)


def bidir_scan(feats, states):
    _, _, q, k, v, gates, _ = feats
    qh = to_heads(q)
    kh = to_heads(k) * (MLSTM_HEAD_DIM ** -0.5)
    vh = to_heads(v)
    g = jnp.moveaxis(gates.astype(jnp.float32), -1, 1)
    i_f, f_f, i_b, f_b = jnp.split(g, 4, axis=1)
    h_f, st_f = mlstm_chunk_scan(qh, kh, vh, i_f, f_f, states[0])
    flip = lambda t: jnp.flip(t, axis=2)
    h_b, st_b = mlstm_chunk_scan(flip(qh), flip(kh), flip(vh), flip(i_b), flip(f_b), states[1])
    return h_f, flip(h_b), (st_f, st_b)


def mlstm_output(feats, h_f, h_b, gn_g, skip, w_down):
    xc, z, _, _, _, _, o = feats
    bsz, s, _ = o.shape
    from_heads = lambda t: jnp.transpose(t, (0, 2, 1, 3)).reshape(bsz, s, MLSTM_INNER)
    of = o.astype(jnp.float32)
    hsum = of[..., :MLSTM_INNER] * from_heads(h_f) + of[..., MLSTM_INNER:] * from_heads(h_b)
    hh = hsum.reshape(bsz, s, MLSTM_HEADS, MLSTM_HEAD_DIM)
    mu = jnp.mean(hh, axis=-1, keepdims=True)
    var = jnp.mean(jnp.square(hh - mu), axis=-1, keepdims=True)
    hn = ((hh - mu) * lax.rsqrt(var + LN_EPS)).reshape(bsz, s, MLSTM_INNER) * gn_g.astype(jnp.float32)
    y = (hn.astype(xc.dtype) + skip * xc) * jax.nn.silu(z)
    return y @ w_down


def mlstm_mixer(u, uc, need_ctx, w_up, w_conv, b_conv, w_q, w_k, w_v, w_gate, b_gate, w_o, b_o, gn_g, skip, w_down):
    p_in = (w_up, w_conv, b_conv, w_q, w_k, w_v, w_gate, b_gate, w_o, b_o)
    bsz = u.shape[0]
    zero_state = (jnp.zeros((bsz, MLSTM_HEADS, MLSTM_HEAD_DIM, MLSTM_HEAD_DIM), jnp.float32),
                  jnp.zeros((bsz, MLSTM_HEADS, MLSTM_HEAD_DIM), jnp.float32),
                  jnp.zeros((bsz, MLSTM_HEADS), jnp.float32))
    fc = mlstm_features(uc, *p_in)
    hf_c, hb_c, ctx_states = bidir_scan(fc, (zero_state, zero_state))
    fl = mlstm_features(u, *p_in)
    hf, hb, _ = bidir_scan(fl, ctx_states)
    y = mlstm_output(fl, hf, hb, gn_g, skip, w_down).astype(u.dtype)
    yc = mlstm_output(fc, hf_c, hb_c, gn_g, skip, w_down).astype(u.dtype) if need_ctx else None
    return y, yc


def setup_inputs(seed: int = 0) -> dict:
    key = jax.random.key(seed)
    ks = iter(jax.random.split(key, 48))

    def nrm(shape, scale):
        return jax.random.normal(next(ks), shape, jnp.float32) * scale

    D, E, H = D_MODEL, MLSTM_INNER, MLSTM_HEADS
    i_bias = nrm((N_B, 2, H), 0.1)
    f_bias = jnp.linspace(3.0, 6.0, H)[None, None, :] + nrm((N_B, 2, H), 0.1)
    b_gate = jnp.concatenate([i_bias[:, 0], f_bias[:, 0], i_bias[:, 1], f_bias[:, 1]], axis=-1)
    return {
        'x': nrm((BATCH, SEQ, D), 1.0),
        'c': nrm((BATCH, D), 1.0),
        'ctx': nrm((BATCH, CTX_LEN, D), 1.0),
        'c_ctx': nrm((D,), 1.0),
        'ada_w': nrm((DEPTH, D, 6 * D), 0.5 * D ** -0.5),
        'ada_b': nrm((DEPTH, 6 * D), 0.02),
        'ln1_g': 1.0 + nrm((DEPTH, D), 0.05),
        'ln1_b': nrm((DEPTH, D), 0.02),
        'ln2_g': 1.0 + nrm((DEPTH, D), 0.05),
        'ln2_b': nrm((DEPTH, D), 0.02),
        'mlp_w1': nrm((DEPTH, D, D_FF), D ** -0.5),
        'mlp_w2': nrm((DEPTH, D_FF, D), BETA * D_FF ** -0.5),
        'conv_w_pw1': nrm((N_A, D, 2 * CONV_DIM), D ** -0.5),
        'conv_b_pw1': nrm((N_A, 2 * CONV_DIM), 0.02),
        'conv_w_dw': nrm((N_A, CONV_WIDTH, CONV_DIM), CONV_WIDTH ** -0.5),
        'conv_b_dw': nrm((N_A, CONV_DIM), 0.02),
        'conv_ln_g': 1.0 + nrm((N_A, CONV_DIM), 0.05),
        'conv_ln_b': nrm((N_A, CONV_DIM), 0.02),
        'conv_w_pw2': nrm((N_A, CONV_DIM, D), BETA * CONV_DIM ** -0.5),
        'conv_b_pw2': nrm((N_A, D), 0.02),
        'ml_w_up': nrm((N_B, D, 2 * E), D ** -0.5),
        'ml_w_conv': nrm((N_B, QKV_CONV_WIDTH, E), QKV_CONV_WIDTH ** -0.5),
        'ml_b_conv': nrm((N_B, E), 0.02),
        'ml_w_q': nrm((N_B, E, E), E ** -0.5),
        'ml_w_k': nrm((N_B, E, E), E ** -0.5),
        'ml_w_v': nrm((N_B, E, E), E ** -0.5),
        'ml_w_gate': nrm((N_B, 3 * E, 4 * H), (3 * E) ** -0.5),
        'ml_b_gate': b_gate,
        'ml_w_o': nrm((N_B, E, 2 * E), E ** -0.5),
        'ml_b_o': nrm((N_B, 2 * E), 0.02),
        'ml_gn_g': 1.0 + nrm((N_B, E), 0.05),
        'ml_skip': 1.0 + nrm((N_B, E), 0.05),
        'ml_w_down': nrm((N_B, E, D), BETA * E ** -0.5),
    }


def reference(x, c, ctx, c_ctx, ada_w, ada_b, ln1_g, ln1_b, ln2_g, ln2_b, mlp_w1, mlp_w2,
              conv_w_pw1, conv_b_pw1, conv_w_dw, conv_b_dw, conv_ln_g, conv_ln_b, conv_w_pw2, conv_b_pw2,
              ml_w_up, ml_w_conv, ml_b_conv, ml_w_q, ml_w_k, ml_w_v, ml_w_gate, ml_b_gate, ml_w_o, ml_b_o,
              ml_gn_g, ml_skip, ml_w_down):
    rows = x.shape[1] // GRID_W
    h, hc = x, ctx
    for i in range(DEPTH):
        need_ctx = i < DEPTH - 1
        j = i // N_MIXERS
        sh1, sc1, g1, sh2, sc2, g2 = adaln(c[:, None, :], ada_w[i], ada_b[i])
        csh1, csc1, cg1, csh2, csc2, cg2 = adaln(c_ctx, ada_w[i], ada_b[i])
        u = modulate(h, sh1, sc1)
        uc = modulate(hc, csh1, csc1)
        if i % N_MIXERS == 0:
            cp = (conv_w_pw1[j], conv_b_pw1[j], conv_w_dw[j], conv_b_dw[j], conv_ln_g[j], conv_ln_b[j],
                  conv_w_pw2[j], conv_b_pw2[j])
            y = conformer_conv(u, rows, *cp)
            yc = conformer_conv(uc, None, *cp) if need_ctx else None
        else:
            y, yc = mlstm_mixer(u, uc, need_ctx, ml_w_up[j], ml_w_conv[j], ml_b_conv[j], ml_w_q[j], ml_w_k[j],
                                ml_w_v[j], ml_w_gate[j], ml_b_gate[j], ml_w_o[j], ml_b_o[j], ml_gn_g[j],
                                ml_skip[j], ml_w_down[j])
        h = layer_norm(ALPHA * h + g1 * y, ln1_g[i], ln1_b[i])
        h = layer_norm(ALPHA * h + g2 * sq_relu_mlp(modulate(h, sh2, sc2), mlp_w1[i], mlp_w2[i]), ln2_g[i], ln2_b[i])
        if need_ctx:
            hc = layer_norm(ALPHA * hc + cg1 * yc, ln1_g[i], ln1_b[i])
            hc = layer_norm(ALPHA * hc + cg2 * sq_relu_mlp(modulate(hc, csh2, csc2), mlp_w1[i], mlp_w2[i]),
                            ln2_g[i], ln2_b[i])
    return h
```

```python
import functools

import jax
import jax.numpy as jnp
from jax import lax
from jax.experimental import pallas as pl
from jax.experimental.pallas import tpu as pltpu

F32 = jnp.float32
BF16 = jnp.bfloat16

D_MODEL = 1024
GRID_W = 64
N_MIXERS = 2
CONV_WIDTH = 31
QKV_CONV_WIDTH = 5
MLSTM_HEADS = 4
CHUNK = 128
LN_EPS = 1e-5

V7X_LANES = 128
V7X_SUBLANES = 8
V7X_BF16_ROWS = 16
V7X_VMEM_LIMIT_BYTES = 56 * 1024 * 1024

TOKEN_TILE = 256
MLP_FF_CHUNK = 1024
CONV_ROW_BLOCK = 64
CONV_PAD_ROWS = 16
MOD_ROWS = 8
NEG_BIG = -1e30


def _resident(shape):
    zeros = (0,) * len(shape)
    return pl.BlockSpec(shape, lambda *_: zeros, pipeline_mode=pl.Buffered(1))


def _params(*semantics):
    return pltpu.CompilerParams(dimension_semantics=semantics,
                                vmem_limit_bytes=V7X_VMEM_LIMIT_BYTES)


def _layer_norm(r, g, b):
    mu = jnp.mean(r, axis=-1, keepdims=True)
    d = r - mu
    var = jnp.mean(d * d, axis=-1, keepdims=True)
    return d * lax.rsqrt(var + LN_EPS) * g + b


def _silu(x):
    return x * jax.nn.sigmoid(x)


def _dot(a, b):
    return jnp.dot(a, b, preferred_element_type=F32)


def _adaln_kernel(c_ref, w_ref, b_ref, o_ref):
    o_ref[0] = _dot(_silu(c_ref[...]), w_ref[0]) + b_ref[0]


def _adaln(cvec, ada_w, ada_b):
    depth, d, n = ada_w.shape
    tn = n // 4
    return pl.pallas_call(
        _adaln_kernel,
        grid=(depth, n // tn),
        in_specs=[pl.BlockSpec((MOD_ROWS, d), lambda i, j: (0, 0)),
                  pl.BlockSpec((1, d, tn), lambda i, j: (i, 0, j)),
                  pl.BlockSpec((1, 1, tn), lambda i, j: (i, 0, j))],
        out_specs=pl.BlockSpec((1, MOD_ROWS, tn), lambda i, j: (i, 0, j)),
        out_shape=jax.ShapeDtypeStruct((depth, MOD_ROWS, n), F32),
        compiler_params=_params("parallel", "parallel"),
        name="adaln",
    )(cvec, ada_w, ada_b.reshape(depth, 1, n))


def _conv_mixer_kernel(x_ref, ctx_ref, mod_ref, wpw1_ref, bpw1_ref, wdw_ref, bdw_ref, cg_ref, cb_ref,
                       wpw2_ref, bpw2_ref, lng_ref, lnb_ref, o_ref, pad_lat_ref, pad_ctx_ref, conv_ref,
                       *, alpha, n_ctx_tiles):
    t = o_ref.shape[1]
    c = wpw2_ref.shape[0]
    ncb = c // V7X_LANES
    left = (CONV_WIDTH - 1) // 2

    def mixer(h, pad_ref):
        nseg, padded, _ = pad_ref.shape[1:]
        seg = padded - 2 * CONV_PAD_ROWS
        sh, sc, g1 = mod_ref[0, 0, 0:1, :], mod_ref[0, 0, 1:2, :], mod_ref[0, 0, 2:3, :]
        u = (h * (1.0 + sc) + sh).astype(BF16)
        a = _dot(u, wpw1_ref[...]) + bpw1_ref[...]
        glu = a[:, :c] * jax.nn.sigmoid(a[:, c:])
        zero_rows = jnp.zeros((CONV_PAD_ROWS, V7X_LANES), F32)
        for cb in range(ncb):
            lanes = slice(cb * V7X_LANES, (cb + 1) * V7X_LANES)
            for s in range(nseg):
                pad_ref[cb, s, 0:CONV_PAD_ROWS, :] = zero_rows
                pad_ref[cb, s, CONV_PAD_ROWS + seg:, :] = zero_rows
                pad_ref[cb, s, CONV_PAD_ROWS:CONV_PAD_ROWS + seg, :] = glu[s * seg:(s + 1) * seg, lanes]

        def conv_block(cb, carry):
            w = wdw_ref[cb]
            for s in range(nseg):
                for r0 in range(0, seg, CONV_ROW_BLOCK):
                    acc = jnp.zeros((CONV_ROW_BLOCK, V7X_LANES), F32)
                    for k in range(CONV_WIDTH):
                        start = CONV_PAD_ROWS + r0 - left + k
                        acc = acc + pad_ref[cb, s, start:start + CONV_ROW_BLOCK, :] * w[k:k + 1, :]
                    conv_ref[cb, s * seg + r0:s * seg + r0 + CONV_ROW_BLOCK, :] = acc
            return carry

        lax.fori_loop(0, ncb, conv_block, 0)
        conv = jnp.concatenate([conv_ref[cb] for cb in range(ncb)], axis=1) + bdw_ref[...]
        act = _silu(_layer_norm(conv, cg_ref[...], cb_ref[...])).astype(BF16)
        y = _dot(act, wpw2_ref[...]) + bpw2_ref[...]
        o_ref[0] = _layer_norm(alpha * h + g1 * y, lng_ref[...], lnb_ref[...])

    j = pl.program_id(1)

    @pl.when(j < n_ctx_tiles)
    def _():
        mixer(ctx_ref[0], pad_ctx_ref)

    @pl.when(j >= n_ctx_tiles)
    def _():
        mixer(x_ref[0], pad_lat_ref)


def _conv_mixer_layer(x, ctx, mod, p, alpha):
    bsz, s, d = x.shape
    ctx_len = ctx.shape[1]
    t = TOKEN_TILE
    assert s % t == 0 and ctx_len % t == 0 and t % GRID_W == 0
    nct = ctx_len // t
    c = p["w_pw2"].shape[0]
    ncb = c // V7X_LANES
    taps = p["w_dw"].shape[0]
    taps_padded = -(-taps // V7X_SUBLANES) * V7X_SUBLANES
    w_dw = jnp.pad(p["w_dw"], ((0, taps_padded - taps), (0, 0)))
    w_dw = w_dw.reshape(taps_padded, ncb, V7X_LANES).transpose(1, 0, 2)
    row = lambda v: v.reshape(1, -1)
    kern = functools.partial(_conv_mixer_kernel, alpha=alpha, n_ctx_tiles=nct)
    return pl.pallas_call(
        kern,
        grid=(bsz, nct + s // t),
        in_specs=[pl.BlockSpec((1, t, d), lambda b, j: (b, jnp.maximum(j - nct, 0), 0)),
                  pl.BlockSpec((1, t, d), lambda b, j: (b, jnp.minimum(j, nct - 1), 0)),
                  pl.BlockSpec((1, 1, MOD_ROWS, d), lambda b, j: (jnp.where(j >= nct, 1, 0), b, 0, 0)),
                  _resident((d, 2 * c)), _resident((1, 2 * c)),
                  _resident((ncb, taps_padded, V7X_LANES)), _resident((1, c)),
                  _resident((1, c)), _resident((1, c)),
                  _resident((c, d)), _resident((1, d)), _resident((1, d)), _resident((1, d))],
        out_specs=pl.BlockSpec((1, t, d), lambda b, j: (b, j, 0)),
        out_shape=jax.ShapeDtypeStruct((bsz, ctx_len + s, d), F32),
        scratch_shapes=[pltpu.VMEM((ncb, t // GRID_W, GRID_W + 2 * CONV_PAD_ROWS, V7X_LANES), F32),
                        pltpu.VMEM((ncb, 1, t + 2 * CONV_PAD_ROWS, V7X_LANES), F32),
                        pltpu.VMEM((ncb, t, V7X_LANES), F32)],
        compiler_params=_params("parallel", "parallel"),
        name="conv_mixer_layer",
    )(x, ctx, mod, p["w_pw1"].astype(BF16), row(p["b_pw1"]), w_dw, row(p["b_dw"]),
      row(p["ln_g"]), row(p["ln_b"]), p["w_pw2"].astype(BF16), row(p["b_pw2"]),
      row(p["ln1_g"]), row(p["ln1_b"]))


def _mlp_kernel(h_ref, mod_ref, w1_ref, w2_ref, g_ref, b_ref, o_ref, *, alpha):
    h = h_ref[0]
    sh, sc, g2 = mod_ref[0, 0, 3:4, :], mod_ref[0, 0, 4:5, :], mod_ref[0, 0, 5:6, :]
    u = (h * (1.0 + sc) + sh).astype(BF16)
    ff = w1_ref.shape[1]
    acc = jnp.zeros(h.shape, F32)
    for f0 in range(0, ff, MLP_FF_CHUNK):
        hid = jnp.maximum(_dot(u, w1_ref[:, f0:f0 + MLP_FF_CHUNK]), 0.0)
        acc = acc + _dot((hid * hid).astype(BF16), w2_ref[f0:f0 + MLP_FF_CHUNK, :])
    o_ref[0] = _layer_norm(alpha * h + g2 * acc, g_ref[...], b_ref[...])


def _mlp_layer(h, mod, w1, w2, ln_g, ln_b, alpha, n_ctx_tiles):
    bsz, s, d = h.shape
    t = TOKEN_TILE
    assert s % t == 0 and w1.shape[1] % MLP_FF_CHUNK == 0
    ff = w1.shape[1]
    return pl.pallas_call(
        functools.partial(_mlp_kernel, alpha=alpha),
        grid=(bsz, s // t),
        in_specs=[pl.BlockSpec((1, t, d), lambda b, j: (b, j, 0)),
                  pl.BlockSpec((1, 1, MOD_ROWS, d), lambda b, j: (jnp.where(j >= n_ctx_tiles, 1, 0), b, 0, 0)),
                  _resident((d, ff)), _resident((ff, d)), _resident((1, d)), _resident((1, d))],
        out_specs=pl.BlockSpec((1, t, d), lambda b, j: (b, j, 0)),
        out_shape=jax.ShapeDtypeStruct((bsz, s, d), F32),
        compiler_params=_params("parallel", "parallel"),
        name="mlp_layer",
    )(h, mod, w1.astype(BF16), w2.astype(BF16), ln_g.reshape(1, d), ln_b.reshape(1, d))


def _mlstm_up_kernel(h_ref, prev_ref, next_ref, mod_ref, wup_ref, wconv_ref, bconv_ref,
                     xm_ref, xc_ref, z_ref, u_ref, xm_all_ref, *, n_ctx_tiles):
    t = h_ref.shape[1]
    e = xm_ref.shape[2]
    halo = prev_ref.shape[1]
    left = (QKV_CONV_WIDTH - 1) // 2
    j = pl.program_id(1)
    nj = pl.num_programs(1)
    sh, sc = mod_ref[0, 0, 0:1, :], mod_ref[0, 0, 1:2, :]
    modulate = lambda v: (v * (1.0 + sc) + sh).astype(BF16)
    u_ref[0:halo, :] = modulate(prev_ref[0])
    u_ref[halo:halo + t, :] = modulate(h_ref[0])
    u_ref[halo + t:, :] = modulate(next_ref[0])
    keep_prev = jnp.where((j == 0) | (j == n_ctx_tiles), 0.0, 1.0)
    keep_next = jnp.where((j == n_ctx_tiles - 1) | (j == nj - 1), 0.0, 1.0)
    xm_all = _dot(u_ref[...], wup_ref[:, :e])
    xm_all_ref[0:halo, :] = xm_all[0:halo] * keep_prev
    xm_all_ref[halo:halo + t, :] = xm_all[halo:halo + t]
    xm_all_ref[halo + t:, :] = xm_all[halo + t:] * keep_next
    xm_ref[0] = xm_all[halo:halo + t].astype(BF16)
    z_ref[0] = _dot(u_ref[halo:halo + t, :], wup_ref[:, e:]).astype(BF16)
    for r0 in range(0, t, CONV_ROW_BLOCK):
        for c0 in range(0, e, 4 * V7X_LANES):
            lanes = slice(c0, c0 + 4 * V7X_LANES)
            acc = jnp.zeros((CONV_ROW_BLOCK, 4 * V7X_LANES), F32)
            for k in range(QKV_CONV_WIDTH):
                start = halo + r0 - left + k
                acc = acc + xm_all_ref[start:start + CONV_ROW_BLOCK, lanes] * wconv_ref[k:k + 1, lanes]
            xc_ref[0, r0:r0 + CONV_ROW_BLOCK, lanes] = _silu(acc + bconv_ref[:, lanes]).astype(BF16)


def _mlstm_up(h, mod, w_up, w_conv, b_conv, n_ctx_tiles):
    bsz, s, d = h.shape
    e = w_up.shape[1] // 2
    t = TOKEN_TILE
    halo = V7X_BF16_ROWS
    assert s % t == 0 and t % halo == 0
    nhb = s // halo
    taps = w_conv.shape[0]
    w_conv = jnp.pad(w_conv, ((0, V7X_SUBLANES - taps), (0, 0)))
    tile = pl.BlockSpec((1, t, e), lambda b, j: (b, j, 0))
    out = jax.ShapeDtypeStruct((bsz, s, e), BF16)
    return pl.pallas_call(
        functools.partial(_mlstm_up_kernel, n_ctx_tiles=n_ctx_tiles),
        grid=(bsz, s // t),
        in_specs=[pl.BlockSpec((1, t, d), lambda b, j: (b, j, 0)),
                  pl.BlockSpec((1, halo, d), lambda b, j: (b, jnp.maximum(j * (t // halo) - 1, 0), 0)),
                  pl.BlockSpec((1, halo, d), lambda b, j: (b, jnp.minimum((j + 1) * (t // halo), nhb - 1), 0)),
                  pl.BlockSpec((1, 1, MOD_ROWS, d), lambda b, j: (jnp.where(j >= n_ctx_tiles, 1, 0), b, 0, 0)),
                  _resident((d, 2 * e)), _resident((V7X_SUBLANES, e)), _resident((1, e))],
        out_specs=[tile, tile, tile],
        out_shape=[out, out, out],
        scratch_shapes=[pltpu.VMEM((t + 2 * halo, d), BF16),
                        pltpu.VMEM((t + 2 * halo, e), F32)],
        compiler_params=_params("parallel", "parallel"),
        name="mlstm_up",
    )(h, h, h, mod, w_up.astype(BF16), w_conv, b_conv.reshape(1, e))


def _mlstm_qkv_kernel(xc_ref, xm_ref, wq_ref, wk_ref, wv_ref, wg_ref, bg_ref,
                      q_ref, k_ref, v_ref, g_ref, *, k_scale):
    e = wq_ref.shape[0]
    xc = xc_ref[0]
    q = _dot(xc, wq_ref[...])
    q_ref[0] = q.astype(BF16)
    gates = _dot(q.astype(BF16), wg_ref[0:e, :])
    k = _dot(xc, wk_ref[...])
    k_ref[0] = (k * k_scale).astype(BF16)
    gates = gates + _dot(k.astype(BF16), wg_ref[e:2 * e, :])
    v = _dot(xm_ref[0], wv_ref[...]).astype(BF16)
    v_ref[0] = v
    g_ref[0] = gates + _dot(v, wg_ref[2 * e:, :]) + bg_ref[...]


def _mlstm_qkv(xc, xm, w_q, w_k, w_v, w_gate, b_gate):
    bsz, s, e = xc.shape
    t = TOKEN_TILE
    ng = w_gate.shape[1]
    k_scale = float(e // MLSTM_HEADS) ** -0.5
    tile = pl.BlockSpec((1, t, e), lambda b, j: (b, j, 0))
    out = jax.ShapeDtypeStruct((bsz, s, e), BF16)
    return pl.pallas_call(
        functools.partial(_mlstm_qkv_kernel, k_scale=k_scale),
        grid=(bsz, s // t),
        in_specs=[tile, tile, _resident((e, e)), _resident((e, e)), _resident((e, e)),
                  _resident((3 * e, ng)), _resident((1, ng))],
        out_specs=[tile, tile, tile, pl.BlockSpec((1, t, ng), lambda b, j: (b, j, 0))],
        out_shape=[out, out, out, jax.ShapeDtypeStruct((bsz, s, ng), F32)],
        compiler_params=_params("parallel", "parallel"),
        name="mlstm_qkv",
    )(xc, xm, w_q.astype(BF16), w_k.astype(BF16), w_v.astype(BF16), w_gate.astype(BF16),
      b_gate.reshape(1, ng))


def _log_sigmoid(x):
    return jnp.minimum(x, 0.0) - jnp.log1p(jnp.exp(-jnp.abs(x)))


def _mlstm_scan_kernel(q_ref, k_ref, v_ref, gcol_ref, grow_ref, o_ref, c_ref, n_ref, m_ref):
    length = q_ref.shape[1]
    nh = c_ref.shape[0]
    dh = c_ref.shape[1]
    backward = pl.program_id(1) == 1

    @pl.when(pl.program_id(2) == 0)
    def _():
        c_ref[...] = jnp.zeros(c_ref.shape, F32)
        n_ref[...] = jnp.zeros(n_ref.shape, F32)
        m_ref[...] = jnp.zeros(m_ref.shape, F32)

    rows = lax.broadcasted_iota(jnp.int32, (length, length), 0)
    cols = lax.broadcasted_iota(jnp.int32, (length, length), 1)
    visible = (rows - cols) * jnp.where(backward, -1, 1) >= 0
    tri = jnp.where(visible, 1.0, 0.0)
    gcol = gcol_ref[0, 0]
    grow = grow_ref[0, 0]
    lf_col = _log_sigmoid(gcol)
    lf_row = _log_sigmoid(grow)
    bcum_col = jnp.dot(tri, lf_col, preferred_element_type=F32, precision=lax.Precision.HIGHEST)
    bcum_row = lax.dot_general(lf_row, tri, (((1,), (1,)), ((), ())),
                               preferred_element_type=F32, precision=lax.Precision.HIGHEST)
    ones_rows = jnp.ones((V7X_SUBLANES, length), BF16)

    for h in range(nh):
        lanes = slice(h * dh, (h + 1) * dh)
        q = q_ref[0, :, lanes]
        k = k_ref[0, :, lanes]
        v = v_ref[0, :, lanes]
        i_col = gcol[:, h:h + 1]
        i_row = grow[h:h + 1, :]
        b_t = bcum_col[:, nh + h:nh + h + 1]
        b_s = bcum_row[nh + h:nh + h + 1, :]
        total = jnp.sum(lf_row[nh + h:nh + h + 1, :], axis=1, keepdims=True)
        m_old = m_ref[h, 0:1, 0:1]
        dlog = jnp.where(visible, b_t - b_s + i_row, NEG_BIG)
        inter = b_t + m_old
        m_row = jnp.maximum(inter, jnp.max(dlog, axis=1, keepdims=True))
        qk = lax.dot_general(q, k, (((1,), (1,)), ((), ())), preferred_element_type=F32)
        scores = qk * jnp.exp(dlog - m_row)
        w_inter = jnp.exp(inter - m_row)
        q_c = _dot(q, c_ref[h].astype(BF16))
        q_n = lax.dot_general(q, n_ref[h].astype(BF16), (((1,), (1,)), ((), ())),
                              preferred_element_type=F32)[:, 0:1]
        num = _dot(scores.astype(BF16), v) + w_inter * q_c
        den = jnp.sum(scores, axis=1, keepdims=True) + w_inter * q_n
        o_ref[0, 0, :, lanes] = num / jnp.maximum(jnp.abs(den), jnp.exp(-m_row))

        dk_col = total - b_t + i_col
        dk_row = total - b_s + i_row
        m_new = jnp.maximum(total + m_old, jnp.max(dk_row, axis=1, keepdims=True))
        wk = (jnp.exp(dk_col - m_new) * k.astype(F32)).astype(BF16)
        decay = jnp.exp(total + m_old - m_new)
        c_ref[h] = decay * c_ref[h] + lax.dot_general(wk, v, (((0,), (0,)), ((), ())),
                                                      preferred_element_type=F32)
        n_ref[h] = decay * n_ref[h] + _dot(ones_rows, wk)
        m_ref[h] = jnp.broadcast_to(m_new, m_ref.shape[1:])


def _mlstm_scan(q, k, v, gates, n_ctx_chunks):
    bsz, s, e = q.shape
    nh = MLSTM_HEADS
    dh = e // nh
    nch = s // CHUNK
    ncc = n_ctx_chunks
    g = gates.reshape(bsz, s, 2, 2 * nh).transpose(0, 2, 1, 3)
    g_t = g.transpose(0, 1, 3, 2)

    def chunk(d, step):
        bwd = jnp.where(step < ncc, ncc - 1 - step, nch - 1 - (step - ncc))
        return jnp.where(d == 0, step, bwd)

    def out_chunk(d, step):
        first_latent = jnp.where(d == 0, ncc, nch - 1)
        return jnp.where(step < ncc, first_latent, chunk(d, step)) - ncc

    tile = pl.BlockSpec((1, CHUNK, e), lambda b, d, st: (b, chunk(d, st), 0))
    return pl.pallas_call(
        _mlstm_scan_kernel,
        grid=(bsz, 2, nch),
        in_specs=[tile, tile, tile,
                  pl.BlockSpec((1, 1, CHUNK, 2 * nh), lambda b, d, st: (b, d, chunk(d, st), 0)),
                  pl.BlockSpec((1, 1, 2 * nh, CHUNK), lambda b, d, st: (b, d, 0, chunk(d, st)))],
        out_specs=pl.BlockSpec((1, 1, CHUNK, e), lambda b, d, st: (d, b, out_chunk(d, st), 0)),
        out_shape=jax.ShapeDtypeStruct((2, bsz, s - ncc * CHUNK, e), F32),
        scratch_shapes=[pltpu.VMEM((nh, dh, dh), F32),
                        pltpu.VMEM((nh, V7X_SUBLANES, dh), F32),
                        pltpu.VMEM((nh, V7X_SUBLANES, V7X_LANES), F32)],
        compiler_params=_params("parallel", "parallel", "arbitrary"),
        name="mlstm_scan",
    )(q, k, v, g, g_t)


def _mlstm_out_kernel(h_ref, xm_ref, xc_ref, z_ref, hs_ref, mod_ref, wo_ref, bo_ref, gn_ref, skip_ref,
                      wdown_ref, lng_ref, lnb_ref, o_ref, *, alpha):
    e = xm_ref.shape[2]
    dh = e // MLSTM_HEADS
    h = h_ref[0]
    xm = xm_ref[0]
    acc = jnp.zeros(h.shape, F32)
    for hd in range(MLSTM_HEADS):
        lanes = slice(hd * dh, (hd + 1) * dh)
        lanes_b = slice(e + hd * dh, e + (hd + 1) * dh)
        o_f = jax.nn.sigmoid(_dot(xm, wo_ref[:, lanes]) + bo_ref[:, lanes])
        o_b = jax.nn.sigmoid(_dot(xm, wo_ref[:, lanes_b]) + bo_ref[:, lanes_b])
        hsum = o_f * hs_ref[0, 0, :, lanes] + o_b * hs_ref[1, 0, :, lanes]
        mu = jnp.mean(hsum, axis=-1, keepdims=True)
        dlt = hsum - mu
        var = jnp.mean(dlt * dlt, axis=-1, keepdims=True)
        hn = dlt * lax.rsqrt(var + LN_EPS) * gn_ref[:, lanes]
        y = (hn + skip_ref[:, lanes] * xc_ref[0, :, lanes].astype(F32)) * _silu(z_ref[0, :, lanes].astype(F32))
        acc = acc + _dot(y.astype(BF16), wdown_ref[lanes, :])
    g1 = mod_ref[0, 0, 2:3, :]
    o_ref[0] = _layer_norm(alpha * h + g1 * acc, lng_ref[...], lnb_ref[...])


def _mlstm_out(hcat, xm, xc, z, hs, mod, w_o, b_o, gn_g, skip, w_down, ln_g, ln_b, alpha, n_ctx_tiles):
    bsz, scat, d = hcat.shape
    e = xm.shape[2]
    t = TOKEN_TILE
    nct = n_ctx_tiles
    s = scat - nct * t
    cat_tile = lambda width: pl.BlockSpec((1, t, width), lambda b, j: (b, j + nct, 0))
    row = lambda v: v.reshape(1, -1)
    return pl.pallas_call(
        functools.partial(_mlstm_out_kernel, alpha=alpha),
        grid=(bsz, s // t),
        in_specs=[cat_tile(d), cat_tile(e), cat_tile(e), cat_tile(e),
                  pl.BlockSpec((2, 1, t, e), lambda b, j: (0, b, j, 0)),
                  pl.BlockSpec((1, 1, MOD_ROWS, d), lambda b, j: (1, b, 0, 0)),
                  _resident((e, 2 * e)), _resident((1, 2 * e)), _resident((1, e)), _resident((1, e)),
                  _resident((e, d)), _resident((1, d)), _resident((1, d))],
        out_specs=pl.BlockSpec((1, t, d), lambda b, j: (b, j, 0)),
        out_shape=jax.ShapeDtypeStruct((bsz, s, d), F32),
        compiler_params=_params("parallel", "parallel"),
        name="mlstm_out",
    )(hcat, xm, xc, z, hs, mod, w_o.astype(BF16), row(b_o), row(gn_g), row(skip),
      w_down.astype(BF16), row(ln_g), row(ln_b))


def _mod_table(ada_out_i, bsz, d):
    m = ada_out_i.reshape(MOD_ROWS, 6, d)
    lat = m[:bsz]
    ctx = jnp.broadcast_to(m[bsz][None], (bsz, 6, d))
    tab = jnp.stack([ctx, lat], axis=0)
    return jnp.pad(tab, ((0, 0), (0, 0), (0, MOD_ROWS - 6), (0, 0)))


def kernel(x, c, ctx, c_ctx, ada_w, ada_b, ln1_g, ln1_b, ln2_g, ln2_b, mlp_w1, mlp_w2, conv_w_pw1, conv_b_pw1, conv_w_dw, conv_b_dw, conv_ln_g, conv_ln_b, conv_w_pw2, conv_b_pw2, ml_w_up, ml_w_conv, ml_b_conv, ml_w_q, ml_w_k, ml_w_v, ml_w_gate, ml_b_gate, ml_w_o, ml_b_o, ml_gn_g, ml_skip, ml_w_down):
    bsz, s, d = x.shape
    ctx_len = ctx.shape[1]
    depth = ada_w.shape[0]
    assert depth == 2 and bsz + 1 <= MOD_ROWS, "built for the two-layer block: conv mixer, then mLSTM"
    alpha = (2 * depth) ** 0.25
    nct = ctx_len // TOKEN_TILE

    cvec = jnp.concatenate([c, c_ctx[None], jnp.zeros((MOD_ROWS - bsz - 1, d), F32)], axis=0)
    ada = _adaln(cvec, ada_w, ada_b)
    mod0 = _mod_table(ada[0], bsz, d)
    mod1 = _mod_table(ada[1], bsz, d)

    conv_p = dict(w_pw1=conv_w_pw1[0], b_pw1=conv_b_pw1[0], w_dw=conv_w_dw[0], b_dw=conv_b_dw[0],
                  ln_g=conv_ln_g[0], ln_b=conv_ln_b[0], w_pw2=conv_w_pw2[0], b_pw2=conv_b_pw2[0],
                  ln1_g=ln1_g[0], ln1_b=ln1_b[0])
    hcat = _conv_mixer_layer(x, ctx, mod0, conv_p, alpha)
    hcat = _mlp_layer(hcat, mod0, mlp_w1[0], mlp_w2[0], ln2_g[0], ln2_b[0], alpha, nct)

    xm, xc, z = _mlstm_up(hcat, mod1, ml_w_up[0], ml_w_conv[0], ml_b_conv[0], nct)
    q, k, v, gates = _mlstm_qkv(xc, xm, ml_w_q[0], ml_w_k[0], ml_w_v[0], ml_w_gate[0], ml_b_gate[0])
    hs = _mlstm_scan(q, k, v, gates, ctx_len // CHUNK)
    h = _mlstm_out(hcat, xm, xc, z, hs, mod1, ml_w_o[0], ml_b_o[0], ml_gn_g[0], ml_skip[0],
                   ml_w_down[0], ln1_g[1], ln1_b[1], alpha, nct)
    return _mlp_layer(h, mod1, mlp_w1[1], mlp_w2[1], ln2_g[1], ln2_b[1], alpha, 0)
```

```python
import functools

import jax
import jax.numpy as jnp
from jax import lax
from jax.experimental import pallas as pl
from jax.experimental.pallas import tpu as pltpu

F32 = jnp.float32
BF16 = jnp.bfloat16

GRID_W = 64
CONV_WIDTH = 31
QKV_CONV_WIDTH = 5
MLSTM_HEADS = 4
LN_EPS = 1e-5

V7X_LANES = 128
V7X_SUBLANES = 8
V7X_BF16_ROWS = 16
V7X_MXU_DIM = 256
V7X_VMEM_LIMIT_BYTES = 56 * 1024 * 1024

WIDE_TOKEN_TILE = 512
NARROW_TOKEN_TILE = 256
SCAN_CHUNK = 256
MLP_FF_CHUNK = 1024
CONV_ROW_BLOCK = 64
CONV_PAD_ROWS = 16
MOD_ROWS = 8
NEG_BIG = -1e30


def _resident(shape):
    zeros = (0,) * len(shape)
    return pl.BlockSpec(shape, lambda *_: zeros, pipeline_mode=pl.Buffered(1))


def _params(*semantics):
    return pltpu.CompilerParams(dimension_semantics=semantics,
                                vmem_limit_bytes=V7X_VMEM_LIMIT_BYTES)


def _token_tile(s, preferred):
    t = min(s, preferred)
    assert s % t == 0
    return t


def _layer_norm(r, g, b):
    mu = jnp.mean(r, axis=-1, keepdims=True)
    d = r - mu
    var = jnp.mean(d * d, axis=-1, keepdims=True)
    return d * lax.rsqrt(var + LN_EPS) * g + b


def _silu(x):
    return x * jax.nn.sigmoid(x)


def _dot(a, b):
    return jnp.dot(a, b, preferred_element_type=F32)


def _row(v):
    return v.reshape(1, -1)


def _adaln_kernel(c_ref, w_ref, b_ref, o_ref):
    o_ref[0] = _dot(_silu(c_ref[...]), w_ref[0]) + b_ref[0]


def _adaln(cvec, ada_w, ada_b):
    depth, d, n = ada_w.shape
    tn = n // 4
    return pl.pallas_call(
        _adaln_kernel,
        grid=(depth, n // tn),
        in_specs=[pl.BlockSpec((MOD_ROWS, d), lambda i, j: (0, 0)),
                  pl.BlockSpec((1, d, tn), lambda i, j: (i, 0, j)),
                  pl.BlockSpec((1, 1, tn), lambda i, j: (i, 0, j))],
        out_specs=pl.BlockSpec((1, MOD_ROWS, tn), lambda i, j: (i, 0, j)),
        out_shape=jax.ShapeDtypeStruct((depth, MOD_ROWS, n), F32),
        compiler_params=_params("parallel", "parallel"),
        name="adaln",
    )(cvec, ada_w, ada_b.reshape(depth, 1, n))


def _conv_mixer_kernel(h_ref, mod_ref, wpw1_ref, bpw1_ref, wdw_ref, bdw_ref, cg_ref, cb_ref,
                       wpw2_ref, bpw2_ref, lng_ref, lnb_ref, o_ref, pad_ref, conv_ref, *, alpha):
    c = wpw2_ref.shape[0]
    ncb, nseg, padded, _ = pad_ref.shape
    seg = padded - 2 * CONV_PAD_ROWS
    left = (CONV_WIDTH - 1) // 2
    pair = V7X_MXU_DIM // V7X_LANES
    h = h_ref[0]
    sh, sc, g1 = mod_ref[0, 0:1, :], mod_ref[0, 1:2, :], mod_ref[0, 2:3, :]
    u = (h * (1.0 + sc) + sh).astype(BF16)
    zero_rows = jnp.zeros((CONV_PAD_ROWS, V7X_LANES), F32)

    def glu_slab(p):
        lo = slice(p * V7X_MXU_DIM, (p + 1) * V7X_MXU_DIM)
        hi = slice(c + p * V7X_MXU_DIM, c + (p + 1) * V7X_MXU_DIM)
        a = _dot(u, wpw1_ref[:, lo]) + bpw1_ref[:, lo]
        gate = _dot(u, wpw1_ref[:, hi]) + bpw1_ref[:, hi]
        glu = a * jax.nn.sigmoid(gate)
        for q in range(pair):
            cb = pair * p + q
            for s in range(nseg):
                pad_ref[cb, s, 0:CONV_PAD_ROWS, :] = zero_rows
                pad_ref[cb, s, CONV_PAD_ROWS + seg:, :] = zero_rows
                pad_ref[cb, s, CONV_PAD_ROWS:CONV_PAD_ROWS + seg, :] = (
                    glu[s * seg:(s + 1) * seg, q * V7X_LANES:(q + 1) * V7X_LANES])

    def conv_block(cb):
        for s in range(nseg):
            for r0 in range(0, seg, CONV_ROW_BLOCK):
                acc = jnp.zeros((CONV_ROW_BLOCK, V7X_LANES), F32)
                for k in range(CONV_WIDTH):
                    start = CONV_PAD_ROWS + r0 - left + k
                    acc = acc + pad_ref[cb, s, start:start + CONV_ROW_BLOCK, :] * wdw_ref[cb, k:k + 1, :]
                conv_ref[cb, s * seg + r0:s * seg + r0 + CONV_ROW_BLOCK, :] = acc

    glu_slab(0)
    for p in range(ncb // pair):
        if p + 1 < ncb // pair:
            glu_slab(p + 1)
        for q in range(pair):
            conv_block(pair * p + q)

    conv = jnp.concatenate([conv_ref[cb] for cb in range(ncb)], axis=1) + bdw_ref[...]
    act = _silu(_layer_norm(conv, cg_ref[...], cb_ref[...])).astype(BF16)
    y = _dot(act, wpw2_ref[...]) + bpw2_ref[...]
    o_ref[0] = _layer_norm(alpha * h + g1 * y, lng_ref[...], lnb_ref[...])


def _conv_mixer_layer(h, seg, mod, p, alpha):
    bsz, s, d = h.shape
    t = _token_tile(s, WIDE_TOKEN_TILE)
    assert t % seg == 0 and seg % CONV_ROW_BLOCK == 0
    c = p["w_pw2"].shape[0]
    ncb = c // V7X_LANES
    taps = p["w_dw"].shape[0]
    taps_padded = -(-taps // V7X_SUBLANES) * V7X_SUBLANES
    w_dw = jnp.pad(p["w_dw"], ((0, taps_padded - taps), (0, 0)))
    w_dw = w_dw.reshape(taps_padded, ncb, V7X_LANES).transpose(1, 0, 2)
    return pl.pallas_call(
        functools.partial(_conv_mixer_kernel, alpha=alpha),
        grid=(bsz, s // t),
        in_specs=[pl.BlockSpec((1, t, d), lambda b, j: (b, j, 0)),
                  pl.BlockSpec((1, MOD_ROWS, d), lambda b, j: (b, 0, 0)),
                  _resident((d, 2 * c)), _resident((1, 2 * c)),
                  _resident((ncb, taps_padded, V7X_LANES)), _resident((1, c)),
                  _resident((1, c)), _resident((1, c)),
                  _resident((c, d)), _resident((1, d)), _resident((1, d)), _resident((1, d))],
        out_specs=pl.BlockSpec((1, t, d), lambda b, j: (b, j, 0)),
        out_shape=jax.ShapeDtypeStruct((bsz, s, d), F32),
        scratch_shapes=[pltpu.VMEM((ncb, t // seg, seg + 2 * CONV_PAD_ROWS, V7X_LANES), F32),
                        pltpu.VMEM((ncb, t, V7X_LANES), F32)],
        compiler_params=_params("parallel", "parallel"),
        name="conv_mixer_layer",
    )(h, mod, p["w_pw1"].astype(BF16), _row(p["b_pw1"]), w_dw, _row(p["b_dw"]),
      _row(p["ln_g"]), _row(p["ln_b"]), p["w_pw2"].astype(BF16), _row(p["b_pw2"]),
      _row(p["ln1_g"]), _row(p["ln1_b"]))


def _mlp_kernel(h_ref, mod_ref, w1_ref, w2_ref, g_ref, b_ref, o_ref, *, alpha):
    h = h_ref[0]
    sh, sc, g2 = mod_ref[0, 3:4, :], mod_ref[0, 4:5, :], mod_ref[0, 5:6, :]
    u = (h * (1.0 + sc) + sh).astype(BF16)
    ff = w1_ref.shape[1]
    acc = jnp.zeros(h.shape, F32)
    for f0 in range(0, ff, MLP_FF_CHUNK):
        hid = jnp.maximum(_dot(u, w1_ref[:, f0:f0 + MLP_FF_CHUNK]), 0.0)
        acc = acc + _dot((hid * hid).astype(BF16), w2_ref[f0:f0 + MLP_FF_CHUNK, :])
    o_ref[0] = _layer_norm(alpha * h + g2 * acc, g_ref[...], b_ref[...])


def _mlp_layer(h, mod, w1, w2, ln_g, ln_b, alpha):
    bsz, s, d = h.shape
    t = _token_tile(s, WIDE_TOKEN_TILE)
    ff = w1.shape[1]
    assert ff % MLP_FF_CHUNK == 0
    return pl.pallas_call(
        functools.partial(_mlp_kernel, alpha=alpha),
        grid=(bsz, s // t),
        in_specs=[pl.BlockSpec((1, t, d), lambda b, j: (b, j, 0)),
                  pl.BlockSpec((1, MOD_ROWS, d), lambda b, j: (b, 0, 0)),
                  _resident((d, ff)), _resident((ff, d)), _resident((1, d)), _resident((1, d))],
        out_specs=pl.BlockSpec((1, t, d), lambda b, j: (b, j, 0)),
        out_shape=jax.ShapeDtypeStruct((bsz, s, d), F32),
        compiler_params=_params("parallel", "parallel"),
        name="mlp_layer",
    )(h, mod, w1.astype(BF16), w2.astype(BF16), _row(ln_g), _row(ln_b))


def _mlstm_up_kernel(h_ref, prev_ref, next_ref, mod_ref, wup_ref, wconv_ref, bconv_ref,
                     xm_ref, xc_ref, z_ref, u_ref, xm_all_ref):
    t = h_ref.shape[1]
    e = xm_ref.shape[2]
    halo = prev_ref.shape[1]
    slab = 2 * V7X_MXU_DIM
    per_slab = slab // V7X_LANES
    left = (QKV_CONV_WIDTH - 1) // 2
    j = pl.program_id(1)
    sh, sc = mod_ref[0, 0:1, :], mod_ref[0, 1:2, :]
    modulate = lambda v: (v * (1.0 + sc) + sh).astype(BF16)
    u_ref[0:halo, :] = modulate(prev_ref[0])
    u_ref[halo:halo + t, :] = modulate(h_ref[0])
    u_ref[halo + t:, :] = modulate(next_ref[0])
    keep_prev = jnp.where(j == 0, 0.0, 1.0)
    keep_next = jnp.where(j == pl.num_programs(1) - 1, 0.0, 1.0)

    def xm_slab(p):
        cols = slice(p * slab, (p + 1) * slab)
        xm_all = _dot(u_ref[...], wup_ref[:, cols])
        xm_ref[0, :, cols] = xm_all[halo:halo + t].astype(BF16)
        for q in range(per_slab):
            cb = per_slab * p + q
            lanes = slice(q * V7X_LANES, (q + 1) * V7X_LANES)
            xm_all_ref[cb, 0:halo, :] = xm_all[0:halo, lanes] * keep_prev
            xm_all_ref[cb, halo:halo + t, :] = xm_all[halo:halo + t, lanes]
            xm_all_ref[cb, halo + t:, :] = xm_all[halo + t:, lanes] * keep_next

    def conv_block(cb):
        lanes = slice(cb * V7X_LANES, (cb + 1) * V7X_LANES)
        for r0 in range(0, t, CONV_ROW_BLOCK):
            acc = jnp.zeros((CONV_ROW_BLOCK, V7X_LANES), F32)
            for k in range(QKV_CONV_WIDTH):
                start = halo + r0 - left + k
                acc = acc + xm_all_ref[cb, start:start + CONV_ROW_BLOCK, :] * wconv_ref[k:k + 1, lanes]
            xc_ref[0, r0:r0 + CONV_ROW_BLOCK, lanes] = _silu(acc + bconv_ref[:, lanes]).astype(BF16)

    nslab = e // slab
    xm_slab(0)
    for p in range(nslab):
        if p + 1 < nslab:
            xm_slab(p + 1)
        cols = slice(e + p * slab, e + (p + 1) * slab)
        z_ref[0, :, p * slab:(p + 1) * slab] = _dot(u_ref[halo:halo + t, :], wup_ref[:, cols]).astype(BF16)
        for q in range(per_slab):
            conv_block(per_slab * p + q)


def _mlstm_up(h, mod, w_up, w_conv, b_conv):
    bsz, s, d = h.shape
    e = w_up.shape[1] // 2
    t = _token_tile(s, WIDE_TOKEN_TILE)
    halo = V7X_BF16_ROWS
    assert t % halo == 0 and t % CONV_ROW_BLOCK == 0
    nhb = s // halo
    taps = w_conv.shape[0]
    w_conv = jnp.pad(w_conv, ((0, V7X_SUBLANES - taps), (0, 0)))
    tile = pl.BlockSpec((1, t, e), lambda b, j: (b, j, 0))
    out = jax.ShapeDtypeStruct((bsz, s, e), BF16)
    return pl.pallas_call(
        _mlstm_up_kernel,
        grid=(bsz, s // t),
        in_specs=[pl.BlockSpec((1, t, d), lambda b, j: (b, j, 0)),
                  pl.BlockSpec((1, halo, d), lambda b, j: (b, jnp.maximum(j * (t // halo) - 1, 0), 0)),
                  pl.BlockSpec((1, halo, d), lambda b, j: (b, jnp.minimum((j + 1) * (t // halo), nhb - 1), 0)),
                  pl.BlockSpec((1, MOD_ROWS, d), lambda b, j: (b, 0, 0)),
                  _resident((d, 2 * e)), _resident((V7X_SUBLANES, e)), _resident((1, e))],
        out_specs=[tile, tile, tile],
        out_shape=[out, out, out],
        scratch_shapes=[pltpu.VMEM((t + 2 * halo, d), BF16),
                        pltpu.VMEM((e // V7X_LANES, t + 2 * halo, V7X_LANES), F32)],
        compiler_params=_params("parallel", "parallel"),
        name="mlstm_up",
    )(h, h, h, mod, w_up.astype(BF16), w_conv, _row(b_conv))


def _mlstm_qkv_kernel(xc_ref, xm_ref, wq_ref, wk_ref, wv_ref, wg_ref, bg_ref,
                      q_ref, k_ref, v_ref, g_ref, *, k_scale):
    e = wq_ref.shape[0]
    xc = xc_ref[0]
    q = _dot(xc, wq_ref[...])
    q_ref[0] = q.astype(BF16)
    gates = _dot(q.astype(BF16), wg_ref[0:e, :])
    k = _dot(xc, wk_ref[...])
    k_ref[0] = (k * k_scale).astype(BF16)
    gates = gates + _dot(k.astype(BF16), wg_ref[e:2 * e, :])
    v = _dot(xm_ref[0], wv_ref[...]).astype(BF16)
    v_ref[0] = v
    g_ref[0] = gates + _dot(v, wg_ref[2 * e:, :]) + bg_ref[...]


def _mlstm_qkv(xc, xm, w_q, w_k, w_v, w_gate, b_gate):
    bsz, s, e = xc.shape
    t = _token_tile(s, NARROW_TOKEN_TILE)
    ng = w_gate.shape[1]
    k_scale = float(e // MLSTM_HEADS) ** -0.5
    tile = pl.BlockSpec((1, t, e), lambda b, j: (b, j, 0))
    out = jax.ShapeDtypeStruct((bsz, s, e), BF16)
    return pl.pallas_call(
        functools.partial(_mlstm_qkv_kernel, k_scale=k_scale),
        grid=(bsz, s // t),
        in_specs=[tile, tile, _resident((e, e)), _resident((e, e)), _resident((e, e)),
                  _resident((3 * e, ng)), _resident((1, ng))],
        out_specs=[tile, tile, tile, pl.BlockSpec((1, t, ng), lambda b, j: (b, j, 0))],
        out_shape=[out, out, out, jax.ShapeDtypeStruct((bsz, s, ng), F32)],
        compiler_params=_params("parallel", "parallel"),
        name="mlstm_qkv",
    )(xc, xm, w_q.astype(BF16), w_k.astype(BF16), w_v.astype(BF16), w_gate.astype(BF16), _row(b_gate))


def _log_sigmoid(x):
    return jnp.minimum(x, 0.0) - jnp.log1p(jnp.exp(-jnp.abs(x)))


def _mlstm_scan_kernel(*refs, has_init, emit_h, emit_state):
    refs = list(refs)
    q_ref, k_ref, v_ref, gcol_ref, grow_ref = refs[:5]
    del refs[:5]
    if has_init:
        c0_ref, n0_ref, m0_ref = refs[:3]
        del refs[:3]
    if emit_h:
        o_ref = refs.pop(0)
    if emit_state:
        cf_ref, nf_ref, mf_ref = refs[:3]
        del refs[:3]
    c_ref, n_ref, m_ref = refs

    length = q_ref.shape[1]
    nh, dh = c_ref.shape[0], c_ref.shape[1]
    backward = pl.program_id(1) == 1
    step = pl.program_id(2)

    @pl.when(step == 0)
    def _():
        if has_init:
            c_ref[...] = c0_ref[0, 0]
            n_ref[...] = n0_ref[0, 0]
            m_ref[...] = m0_ref[0, 0]
        else:
            c_ref[...] = jnp.zeros(c_ref.shape, F32)
            n_ref[...] = jnp.zeros(n_ref.shape, F32)
            m_ref[...] = jnp.zeros(m_ref.shape, F32)

    rows = lax.broadcasted_iota(jnp.int32, (length, length), 0)
    cols = lax.broadcasted_iota(jnp.int32, (length, length), 1)
    visible = (rows - cols) * jnp.where(backward, -1, 1) >= 0
    tri = jnp.where(visible, 1.0, 0.0)
    gcol = gcol_ref[0, 0]
    grow = grow_ref[0, 0]
    lf_col = _log_sigmoid(gcol)
    lf_row = _log_sigmoid(grow)
    bcum_col = jnp.dot(tri, lf_col, preferred_element_type=F32, precision=lax.Precision.HIGHEST)
    bcum_row = lax.dot_general(lf_row, tri, (((1,), (1,)), ((), ())),
                               preferred_element_type=F32, precision=lax.Precision.HIGHEST)
    ones_rows = jnp.ones((V7X_SUBLANES, length), BF16)

    for h in range(nh):
        lanes = slice(h * dh, (h + 1) * dh)
        k = k_ref[0, :, lanes]
        v = v_ref[0, :, lanes]
        i_col = gcol[:, h:h + 1]
        i_row = grow[h:h + 1, :]
        b_t = bcum_col[:, nh + h:nh + h + 1]
        b_s = bcum_row[nh + h:nh + h + 1, :]
        total = jnp.sum(lf_row[nh + h:nh + h + 1, :], axis=1, keepdims=True)
        m_old = m_ref[h, 0:1, 0:1]
        if emit_h:
            q = q_ref[0, :, lanes]
            dlog = jnp.where(visible, b_t - b_s + i_row, NEG_BIG)
            inter = b_t + m_old
            m_row = jnp.maximum(inter, jnp.max(dlog, axis=1, keepdims=True))
            qk = lax.dot_general(q, k, (((1,), (1,)), ((), ())), preferred_element_type=F32)
            scores = qk * jnp.exp(dlog - m_row)
            w_inter = jnp.exp(inter - m_row)
            q_c = _dot(q, c_ref[h].astype(BF16))
            q_n = lax.dot_general(q, n_ref[h].astype(BF16), (((1,), (1,)), ((), ())),
                                  preferred_element_type=F32)[:, 0:1]
            num = _dot(scores.astype(BF16), v) + w_inter * q_c
            den = jnp.sum(scores, axis=1, keepdims=True) + w_inter * q_n
            o_ref[0, 0, :, lanes] = (num / jnp.maximum(jnp.abs(den), jnp.exp(-m_row))).astype(o_ref.dtype)

        dk_col = total - b_t + i_col
        dk_row = total - b_s + i_row
        m_new = jnp.maximum(total + m_old, jnp.max(dk_row, axis=1, keepdims=True))
        wk = (jnp.exp(dk_col - m_new) * k.astype(F32)).astype(BF16)
        decay = jnp.exp(total + m_old - m_new)
        c_ref[h] = decay * c_ref[h] + lax.dot_general(wk, v, (((0,), (0,)), ((), ())),
                                                      preferred_element_type=F32)
        n_ref[h] = decay * n_ref[h] + _dot(ones_rows, wk)
        m_ref[h] = jnp.broadcast_to(m_new, m_ref.shape[1:])

    if emit_state:
        @pl.when(step == pl.num_programs(2) - 1)
        def _():
            cf_ref[0, 0] = c_ref[...]
            nf_ref[0, 0] = n_ref[...]
            mf_ref[0, 0] = m_ref[...]


def _mlstm_scan(q, k, v, gates, init_state, emit_h, emit_state):
    bsz, s, e = q.shape
    nh = MLSTM_HEADS
    dh = e // nh
    length = _token_tile(s, SCAN_CHUNK)
    nch = s // length
    g = gates.reshape(bsz, s, 2, 2 * nh).transpose(0, 2, 1, 3)
    g_t = g.transpose(0, 1, 3, 2)
    chunk = lambda d, st: jnp.where(d == 0, st, nch - 1 - st)
    tile = pl.BlockSpec((1, length, e), lambda b, d, st: (b, chunk(d, st), 0))
    state_shapes = [(nh, dh, dh), (nh, V7X_SUBLANES, dh), (nh, V7X_SUBLANES, V7X_LANES)]
    state_specs = [pl.BlockSpec((1, 1) + shp, lambda b, d, st: (b, d, 0, 0, 0)) for shp in state_shapes]

    in_specs = [tile, tile, tile,
                pl.BlockSpec((1, 1, length, 2 * nh), lambda b, d, st: (b, d, chunk(d, st), 0)),
                pl.BlockSpec((1, 1, 2 * nh, length), lambda b, d, st: (b, d, 0, chunk(d, st)))]
    args = [q, k, v, g, g_t]
    if init_state is not None:
        in_specs += state_specs
        args += list(init_state)
    out_specs, out_shape = [], []
    if emit_h:
        out_specs.append(pl.BlockSpec((1, 1, length, e), lambda b, d, st: (d, b, chunk(d, st), 0)))
        out_shape.append(jax.ShapeDtypeStruct((2, bsz, s, e), BF16))
    if emit_state:
        out_specs += state_specs
        out_shape += [jax.ShapeDtypeStruct((bsz, 2) + shp, F32) for shp in state_shapes]
    outs = pl.pallas_call(
        functools.partial(_mlstm_scan_kernel, has_init=init_state is not None, emit_h=emit_h,
                          emit_state=emit_state),
        grid=(bsz, 2, nch),
        in_specs=in_specs,
        out_specs=out_specs,
        out_shape=out_shape,
        scratch_shapes=[pltpu.VMEM(shp, F32) for shp in state_shapes],
        compiler_params=_params("parallel", "parallel", "arbitrary"),
        name="mlstm_scan",
    )(*args)
    outs = list(outs)
    h = outs.pop(0) if emit_h else None
    return h, (tuple(outs) if emit_state else None)


def _mlstm_out_kernel(h_ref, xm_ref, xc_ref, z_ref, hs_ref, mod_ref, wo_ref, bo_ref, gn_ref, skip_ref,
                      wdown_ref, lng_ref, lnb_ref, o_ref, *, alpha):
    e = xm_ref.shape[2]
    dh = e // MLSTM_HEADS
    h = h_ref[0]
    xm = xm_ref[0]
    acc = jnp.zeros(h.shape, F32)
    for hd in range(MLSTM_HEADS):
        lanes = slice(hd * dh, (hd + 1) * dh)
        lanes_b = slice(e + hd * dh, e + (hd + 1) * dh)
        o_f = jax.nn.sigmoid(_dot(xm, wo_ref[:, lanes]) + bo_ref[:, lanes])
        o_b = jax.nn.sigmoid(_dot(xm, wo_ref[:, lanes_b]) + bo_ref[:, lanes_b])
        hsum = o_f * hs_ref[0, 0, :, lanes].astype(F32) + o_b * hs_ref[1, 0, :, lanes].astype(F32)
        mu = jnp.mean(hsum, axis=-1, keepdims=True)
        dlt = hsum - mu
        var = jnp.mean(dlt * dlt, axis=-1, keepdims=True)
        hn = dlt * lax.rsqrt(var + LN_EPS) * gn_ref[:, lanes]
        y = (hn + skip_ref[:, lanes] * xc_ref[0, :, lanes].astype(F32)) * _silu(z_ref[0, :, lanes].astype(F32))
        acc = acc + _dot(y.astype(BF16), wdown_ref[lanes, :])
    g1 = mod_ref[0, 2:3, :]
    o_ref[0] = _layer_norm(alpha * h + g1 * acc, lng_ref[...], lnb_ref[...])


def _mlstm_out(h, xm, xc, z, hs, mod, w_o, b_o, gn_g, skip, w_down, ln_g, ln_b, alpha):
    bsz, s, d = h.shape
    e = xm.shape[2]
    t = _token_tile(s, NARROW_TOKEN_TILE)
    tile = lambda width: pl.BlockSpec((1, t, width), lambda b, j: (b, j, 0))
    return pl.pallas_call(
        functools.partial(_mlstm_out_kernel, alpha=alpha),
        grid=(bsz, s // t),
        in_specs=[tile(d), tile(e), tile(e), tile(e),
                  pl.BlockSpec((2, 1, t, e), lambda b, j: (0, b, j, 0)),
                  pl.BlockSpec((1, MOD_ROWS, d), lambda b, j: (b, 0, 0)),
                  _resident((e, 2 * e)), _resident((1, 2 * e)), _resident((1, e)), _resident((1, e)),
                  _resident((e, d)), _resident((1, d)), _resident((1, d))],
        out_specs=pl.BlockSpec((1, t, d), lambda b, j: (b, j, 0)),
        out_shape=jax.ShapeDtypeStruct((bsz, s, d), F32),
        compiler_params=_params("parallel", "parallel"),
        name="mlstm_out",
    )(h, xm, xc, z, hs, mod, w_o.astype(BF16), _row(b_o), _row(gn_g), _row(skip),
      w_down.astype(BF16), _row(ln_g), _row(ln_b))


def _mod_tables(ada_out_i, bsz, d):
    m = ada_out_i.reshape(MOD_ROWS, 6, d)
    pad = ((0, 0), (0, MOD_ROWS - 6), (0, 0))
    lat = jnp.pad(m[:bsz], pad)
    ctx = jnp.pad(jnp.broadcast_to(m[bsz][None], (bsz, 6, d)), pad)
    return lat, ctx


def kernel(x, c, ctx, c_ctx, ada_w, ada_b, ln1_g, ln1_b, ln2_g, ln2_b, mlp_w1, mlp_w2, conv_w_pw1, conv_b_pw1, conv_w_dw, conv_b_dw, conv_ln_g, conv_ln_b, conv_w_pw2, conv_b_pw2, ml_w_up, ml_w_conv, ml_b_conv, ml_w_q, ml_w_k, ml_w_v, ml_w_gate, ml_b_gate, ml_w_o, ml_b_o, ml_gn_g, ml_skip, ml_w_down):
    bsz, s, d = x.shape
    depth = ada_w.shape[0]
    assert depth == 2 and bsz + 1 <= MOD_ROWS, "built for the two-layer block: conv mixer, then mLSTM"
    alpha = (2 * depth) ** 0.25

    cvec = jnp.concatenate([c, c_ctx[None], jnp.zeros((MOD_ROWS - bsz - 1, d), F32)], axis=0)
    ada = _adaln(cvec, ada_w, ada_b)
    mod0, mod0_ctx = _mod_tables(ada[0], bsz, d)
    mod1, mod1_ctx = _mod_tables(ada[1], bsz, d)

    conv_p = dict(w_pw1=conv_w_pw1[0], b_pw1=conv_b_pw1[0], w_dw=conv_w_dw[0], b_dw=conv_b_dw[0],
                  ln_g=conv_ln_g[0], ln_b=conv_ln_b[0], w_pw2=conv_w_pw2[0], b_pw2=conv_b_pw2[0],
                  ln1_g=ln1_g[0], ln1_b=ln1_b[0])
    h = _conv_mixer_layer(x, GRID_W, mod0, conv_p, alpha)
    h = _mlp_layer(h, mod0, mlp_w1[0], mlp_w2[0], ln2_g[0], ln2_b[0], alpha)
    hc = _conv_mixer_layer(ctx, ctx.shape[1], mod0_ctx, conv_p, alpha)
    hc = _mlp_layer(hc, mod0_ctx, mlp_w1[0], mlp_w2[0], ln2_g[0], ln2_b[0], alpha)

    feats = lambda hh, mm: _mlstm_up(hh, mm, ml_w_up[0], ml_w_conv[0], ml_b_conv[0])
    qkv = lambda xc_, xm_: _mlstm_qkv(xc_, xm_, ml_w_q[0], ml_w_k[0], ml_w_v[0], ml_w_gate[0], ml_b_gate[0])
    xm_c, xc_c, _ = feats(hc, mod1_ctx)
    _, ctx_state = _mlstm_scan(*qkv(xc_c, xm_c), init_state=None, emit_h=False, emit_state=True)
    xm, xc, z = feats(h, mod1)
    hs, _ = _mlstm_scan(*qkv(xc, xm), init_state=ctx_state, emit_h=True, emit_state=False)
    h = _mlstm_out(h, xm, xc, z, hs, mod1, ml_w_o[0], ml_b_o[0], ml_gn_g[0], ml_skip[0],
                   ml_w_down[0], ln1_g[1], ln1_b[1], alpha)
    return _mlp_layer(h, mod1, mlp_w1[1], mlp_w2[1], ln2_g[1], ln2_b[1], alpha)
```

```python
import functools

import jax
import jax.numpy as jnp
import numpy as np
from jax import lax
from jax.experimental import pallas as pl
from jax.experimental.pallas import tpu as pltpu

F32 = jnp.float32
BF16 = jnp.bfloat16

GRID_W = 64
CONV_WIDTH = 31
QKV_CONV_WIDTH = 5
MLSTM_HEADS = 4
GATE_PACK_LANES = 8 * MLSTM_HEADS
LN_EPS = 1e-5

V7X_LANES = 128
V7X_SUBLANES = 8
V7X_BF16_ROWS = 16
V7X_MXU_DIM = 256
V7X_VMEM_LIMIT_BYTES = 56 * 1024 * 1024

WIDE_TOKEN_TILE = 512
NARROW_TOKEN_TILE = 256
SCAN_CHUNK = 256
MLP_FF_CHUNK = 1024
CONV_ROW_BLOCK = 64
CONV_PAD_ROWS = 16
MOD_ROWS = 8
NEG_BIG = -1e30


def _resident(shape):
    zeros = (0,) * len(shape)
    return pl.BlockSpec(shape, lambda *_: zeros, pipeline_mode=pl.Buffered(1))


def _params(*semantics):
    return pltpu.CompilerParams(dimension_semantics=semantics,
                                vmem_limit_bytes=V7X_VMEM_LIMIT_BYTES)


def _token_tile(s, preferred):
    t = min(s, preferred)
    assert s % t == 0
    return t


def _layer_norm(r, g, b):
    mu = jnp.mean(r, axis=-1, keepdims=True)
    d = r - mu
    var = jnp.mean(d * d, axis=-1, keepdims=True)
    return d * lax.rsqrt(var + LN_EPS) * g + b


def _silu(x):
    return x * jax.nn.sigmoid(x)


def _dot(a, b):
    return jnp.dot(a, b, preferred_element_type=F32)


def _row(v):
    return v.reshape(1, -1)


def _adaln_kernel(c_ref, w_ref, b_ref, o_ref):
    o_ref[0] = _dot(_silu(c_ref[...]), w_ref[0]) + b_ref[0]


def _adaln(cvec, ada_w, ada_b):
    depth, d, n = ada_w.shape
    tn = n // 4
    return pl.pallas_call(
        _adaln_kernel,
        grid=(depth, n // tn),
        in_specs=[pl.BlockSpec((MOD_ROWS, d), lambda i, j: (0, 0)),
                  pl.BlockSpec((1, d, tn), lambda i, j: (i, 0, j)),
                  pl.BlockSpec((1, 1, tn), lambda i, j: (i, 0, j))],
        out_specs=pl.BlockSpec((1, MOD_ROWS, tn), lambda i, j: (i, 0, j)),
        out_shape=jax.ShapeDtypeStruct((depth, MOD_ROWS, n), F32),
        compiler_params=_params("parallel", "parallel"),
        name="adaln",
    )(cvec, ada_w, ada_b.reshape(depth, 1, n))


def _conv_mixer_kernel(h_ref, mod_ref, wpw1_ref, bpw1_ref, wdw_ref, bdw_ref, cg_ref, cb_ref,
                       wpw2_ref, bpw2_ref, lng_ref, lnb_ref, o_ref, pad_ref, conv_ref, *, alpha):
    c = wpw2_ref.shape[0]
    ncb, nseg, padded, _ = pad_ref.shape
    seg = padded - 2 * CONV_PAD_ROWS
    left = (CONV_WIDTH - 1) // 2
    pair = V7X_MXU_DIM // V7X_LANES
    h = h_ref[0]
    sh, sc, g1 = mod_ref[0, 0:1, :], mod_ref[0, 1:2, :], mod_ref[0, 2:3, :]
    u = (h * (1.0 + sc) + sh).astype(BF16)
    zero_rows = jnp.zeros((CONV_PAD_ROWS, V7X_LANES), F32)

    def glu_slab(p):
        lo = slice(p * V7X_MXU_DIM, (p + 1) * V7X_MXU_DIM)
        hi = slice(c + p * V7X_MXU_DIM, c + (p + 1) * V7X_MXU_DIM)
        a = _dot(u, wpw1_ref[:, lo]) + bpw1_ref[:, lo]
        gate = _dot(u, wpw1_ref[:, hi]) + bpw1_ref[:, hi]
        glu = a * jax.nn.sigmoid(gate)
        for q in range(pair):
            cb = pair * p + q
            for s in range(nseg):
                pad_ref[cb, s, 0:CONV_PAD_ROWS, :] = zero_rows
                pad_ref[cb, s, CONV_PAD_ROWS + seg:, :] = zero_rows
                pad_ref[cb, s, CONV_PAD_ROWS:CONV_PAD_ROWS + seg, :] = (
                    glu[s * seg:(s + 1) * seg, q * V7X_LANES:(q + 1) * V7X_LANES])

    def conv_block(cb):
        for s in range(nseg):
            for r0 in range(0, seg, CONV_ROW_BLOCK):
                acc = jnp.zeros((CONV_ROW_BLOCK, V7X_LANES), F32)
                for k in range(CONV_WIDTH):
                    start = CONV_PAD_ROWS + r0 - left + k
                    acc = acc + pad_ref[cb, s, start:start + CONV_ROW_BLOCK, :] * wdw_ref[cb, k:k + 1, :]
                conv_ref[cb, s * seg + r0:s * seg + r0 + CONV_ROW_BLOCK, :] = acc

    glu_slab(0)
    for p in range(ncb // pair):
        if p + 1 < ncb // pair:
            glu_slab(p + 1)
        for q in range(pair):
            conv_block(pair * p + q)

    conv = jnp.concatenate([conv_ref[cb] for cb in range(ncb)], axis=1) + bdw_ref[...]
    act = _silu(_layer_norm(conv, cg_ref[...], cb_ref[...])).astype(BF16)
    y = _dot(act, wpw2_ref[...]) + bpw2_ref[...]
    o_ref[0] = _layer_norm(alpha * h + g1 * y, lng_ref[...], lnb_ref[...])


def _conv_mixer_layer(h, seg, mod, p, alpha):
    bsz, s, d = h.shape
    t = _token_tile(s, WIDE_TOKEN_TILE)
    assert t % seg == 0 and seg % CONV_ROW_BLOCK == 0
    c = p["w_pw2"].shape[0]
    ncb = c // V7X_LANES
    taps = p["w_dw"].shape[0]
    taps_padded = -(-taps // V7X_SUBLANES) * V7X_SUBLANES
    w_dw = jnp.pad(p["w_dw"], ((0, taps_padded - taps), (0, 0)))
    w_dw = w_dw.reshape(taps_padded, ncb, V7X_LANES).transpose(1, 0, 2)
    return pl.pallas_call(
        functools.partial(_conv_mixer_kernel, alpha=alpha),
        grid=(bsz, s // t),
        in_specs=[pl.BlockSpec((1, t, d), lambda b, j: (b, j, 0)),
                  pl.BlockSpec((1, MOD_ROWS, d), lambda b, j: (b, 0, 0)),
                  _resident((d, 2 * c)), _resident((1, 2 * c)),
                  _resident((ncb, taps_padded, V7X_LANES)), _resident((1, c)),
                  _resident((1, c)), _resident((1, c)),
                  _resident((c, d)), _resident((1, d)), _resident((1, d)), _resident((1, d))],
        out_specs=pl.BlockSpec((1, t, d), lambda b, j: (b, j, 0)),
        out_shape=jax.ShapeDtypeStruct((bsz, s, d), F32),
        scratch_shapes=[pltpu.VMEM((ncb, t // seg, seg + 2 * CONV_PAD_ROWS, V7X_LANES), F32),
                        pltpu.VMEM((ncb, t, V7X_LANES), F32)],
        compiler_params=_params("parallel", "parallel"),
        name="conv_mixer_layer",
    )(h, mod, p["w_pw1"].astype(BF16), _row(p["b_pw1"]), w_dw, _row(p["b_dw"]),
      _row(p["ln_g"]), _row(p["ln_b"]), p["w_pw2"].astype(BF16), _row(p["b_pw2"]),
      _row(p["ln1_g"]), _row(p["ln1_b"]))


def _mlp_kernel(h_ref, mod_ref, w1_ref, w2_ref, g_ref, b_ref, o_ref, *, alpha):
    h = h_ref[0]
    sh, sc, g2 = mod_ref[0, 3:4, :], mod_ref[0, 4:5, :], mod_ref[0, 5:6, :]
    u = (h * (1.0 + sc) + sh).astype(BF16)
    ff = w1_ref.shape[1]
    acc = jnp.zeros(h.shape, F32)
    for f0 in range(0, ff, MLP_FF_CHUNK):
        hid = jnp.maximum(_dot(u, w1_ref[:, f0:f0 + MLP_FF_CHUNK]), 0.0)
        acc = acc + _dot((hid * hid).astype(BF16), w2_ref[f0:f0 + MLP_FF_CHUNK, :])
    o_ref[0] = _layer_norm(alpha * h + g2 * acc, g_ref[...], b_ref[...])


def _mlp_layer(h, mod, w1, w2, ln_g, ln_b, alpha):
    bsz, s, d = h.shape
    t = _token_tile(s, WIDE_TOKEN_TILE)
    ff = w1.shape[1]
    assert ff % MLP_FF_CHUNK == 0
    return pl.pallas_call(
        functools.partial(_mlp_kernel, alpha=alpha),
        grid=(bsz, s // t),
        in_specs=[pl.BlockSpec((1, t, d), lambda b, j: (b, j, 0)),
                  pl.BlockSpec((1, MOD_ROWS, d), lambda b, j: (b, 0, 0)),
                  _resident((d, ff)), _resident((ff, d)), _resident((1, d)), _resident((1, d))],
        out_specs=pl.BlockSpec((1, t, d), lambda b, j: (b, j, 0)),
        out_shape=jax.ShapeDtypeStruct((bsz, s, d), F32),
        compiler_params=_params("parallel", "parallel"),
        name="mlp_layer",
    )(h, mod, w1.astype(BF16), w2.astype(BF16), _row(ln_g), _row(ln_b))


def _mlstm_up_kernel(h_ref, prev_ref, next_ref, mod_ref, wup_ref, wconv_ref, bconv_ref,
                     xm_ref, xc_ref, z_ref, u_ref, xm_all_ref):
    t = h_ref.shape[1]
    e = xm_ref.shape[2]
    halo = prev_ref.shape[1]
    slab = 2 * V7X_MXU_DIM
    per_slab = slab // V7X_LANES
    left = (QKV_CONV_WIDTH - 1) // 2
    j = pl.program_id(1)
    sh, sc = mod_ref[0, 0:1, :], mod_ref[0, 1:2, :]
    modulate = lambda v: (v * (1.0 + sc) + sh).astype(BF16)
    u_ref[0:halo, :] = modulate(prev_ref[0])
    u_ref[halo:halo + t, :] = modulate(h_ref[0])
    u_ref[halo + t:, :] = modulate(next_ref[0])
    keep_prev = jnp.where(j == 0, 0.0, 1.0)
    keep_next = jnp.where(j == pl.num_programs(1) - 1, 0.0, 1.0)

    def xm_matmul(p):
        return _dot(u_ref[...], wup_ref[:, p * slab:(p + 1) * slab])

    def xm_store(p, xm_all):
        cols = slice(p * slab, (p + 1) * slab)
        xm_ref[0, :, cols] = xm_all[halo:halo + t].astype(BF16)
        for q in range(per_slab):
            cb = per_slab * p + q
            lanes = slice(q * V7X_LANES, (q + 1) * V7X_LANES)
            xm_all_ref[cb, 0:halo, :] = xm_all[0:halo, lanes] * keep_prev
            xm_all_ref[cb, halo:halo + t, :] = xm_all[halo:halo + t, lanes]
            xm_all_ref[cb, halo + t:, :] = xm_all[halo + t:, lanes] * keep_next

    def conv_block(cb):
        lanes = slice(cb * V7X_LANES, (cb + 1) * V7X_LANES)
        for r0 in range(0, t, CONV_ROW_BLOCK):
            acc = jnp.zeros((CONV_ROW_BLOCK, V7X_LANES), F32)
            for k in range(QKV_CONV_WIDTH):
                start = halo + r0 - left + k
                acc = acc + xm_all_ref[cb, start:start + CONV_ROW_BLOCK, :] * wconv_ref[k:k + 1, lanes]
            xc_ref[0, r0:r0 + CONV_ROW_BLOCK, lanes] = _silu(acc + bconv_ref[:, lanes]).astype(BF16)

    nslab = e // slab
    for p in range(nslab):
        xm_store(p, xm_matmul(p))
    for p in range(nslab):
        cols = slice(e + p * slab, e + (p + 1) * slab)
        z_ref[0, :, p * slab:(p + 1) * slab] = _dot(u_ref[halo:halo + t, :], wup_ref[:, cols]).astype(BF16)
        for q in range(per_slab):
            conv_block(per_slab * p + q)


def _mlstm_up(h, mod, w_up, w_conv, b_conv):
    bsz, s, d = h.shape
    e = w_up.shape[1] // 2
    t = _token_tile(s, WIDE_TOKEN_TILE)
    halo = V7X_BF16_ROWS
    assert t % halo == 0 and t % CONV_ROW_BLOCK == 0
    nhb = s // halo
    taps = w_conv.shape[0]
    w_conv = jnp.pad(w_conv, ((0, V7X_SUBLANES - taps), (0, 0)))
    tile = pl.BlockSpec((1, t, e), lambda b, j: (b, j, 0))
    out = jax.ShapeDtypeStruct((bsz, s, e), BF16)
    return pl.pallas_call(
        _mlstm_up_kernel,
        grid=(bsz, s // t),
        in_specs=[pl.BlockSpec((1, t, d), lambda b, j: (b, j, 0)),
                  pl.BlockSpec((1, halo, d), lambda b, j: (b, jnp.maximum(j * (t // halo) - 1, 0), 0)),
                  pl.BlockSpec((1, halo, d), lambda b, j: (b, jnp.minimum((j + 1) * (t // halo), nhb - 1), 0)),
                  pl.BlockSpec((1, MOD_ROWS, d), lambda b, j: (b, 0, 0)),
                  _resident((d, 2 * e)), _resident((V7X_SUBLANES, e)), _resident((1, e))],
        out_specs=[tile, tile, tile],
        out_shape=[out, out, out],
        scratch_shapes=[pltpu.VMEM((t + 2 * halo, d), BF16),
                        pltpu.VMEM((e // V7X_LANES, t + 2 * halo, V7X_LANES), F32)],
        compiler_params=_params("parallel", "parallel"),
        name="mlstm_up",
    )(h, h, h, mod, w_up.astype(BF16), w_conv, _row(b_conv))


def _log_sigmoid(x):
    return jnp.minimum(x, 0.0) - jnp.log1p(jnp.exp(-jnp.abs(x)))


def _split_bf16x3(x):
    hi = x.astype(BF16)
    r1 = x - hi.astype(F32)
    mid = r1.astype(BF16)
    lo = (r1 - mid.astype(F32)).astype(BF16)
    return jnp.concatenate([hi, mid, lo], axis=1)


def _mlstm_qkv_kernel(*refs, k_scale, emit_intra):
    refs = list(refs)
    xc_ref, xm_ref, wq_ref, wk_ref, wv_ref, wg_ref, bg_ref, q_ref, k_ref, v_ref, g_ref = refs[:11]
    h0_ref = refs[11] if emit_intra else None
    e = wq_ref.shape[0]
    t = xc_ref.shape[1]
    nh = MLSTM_HEADS
    dh = e // nh
    half = V7X_LANES // 2
    xc = xc_ref[0]
    q = _dot(xc, wq_ref[...]).astype(BF16)
    q_ref[0] = q
    gates = _dot(q, wg_ref[0:e, :])
    k = _dot(xc, wk_ref[...])
    k_ref[0] = (k * k_scale).astype(BF16)
    gates = gates + _dot(k.astype(BF16), wg_ref[e:2 * e, :])
    v = _dot(xm_ref[0], wv_ref[...]).astype(BF16)
    v_ref[0] = v
    gates = gates + _dot(v, wg_ref[2 * e:, :]) + bg_ref[...]

    lane = lax.broadcasted_iota(jnp.int32, (1, V7X_LANES), 1)
    group = (lane % GATE_PACK_LANES) // nh
    rows = lax.broadcasted_iota(jnp.int32, (t, t), 0)
    cols = lax.broadcasted_iota(jnp.int32, (t, t), 1)
    visible = (rows >= cols, rows <= cols)
    lf = _log_sigmoid(gates)
    parts = _split_bf16x3(lf)
    fold = lambda c3: c3[:, :V7X_LANES] + c3[:, V7X_LANES:2 * V7X_LANES] + c3[:, 2 * V7X_LANES:]
    prefix = fold(_dot(jnp.where(visible[0], 1.0, 0.0).astype(BF16), parts))
    suffix = fold(_dot(jnp.where(visible[1], 1.0, 0.0).astype(BF16), parts))
    b = jnp.where(lane < GATE_PACK_LANES, prefix, suffix)
    a = pltpu.roll(gates, half, 1) - b
    total = jnp.sum(lf, axis=0, keepdims=True)
    amax = jnp.max(a, axis=0, keepdims=True)
    pack = jnp.where(group == 0, b, jnp.where(group == 2, a, jnp.where(group == 4, total,
                                                                         jnp.where(group == 5, amax, 0.0))))
    if emit_intra:
        a_rows = a.T
        for hd in range(nh):
            lanes = slice(hd * dh, (hd + 1) * dh)
            qk = lax.dot_general(q[:, lanes], k_ref[0, :, lanes], (((1,), (1,)), ((), ())),
                                 preferred_element_type=F32)
            for d in range(2):
                base = d * GATE_PACK_LANES + hd
                a_row = a_rows[base + 2 * nh:base + 2 * nh + 1, :]
                masked = jnp.where(visible[d], a_row, NEG_BIG)
                cmax = jnp.max(masked, axis=1, keepdims=True)
                scores = qk * jnp.exp(masked - cmax)
                rowsum = jnp.sum(scores, axis=1, keepdims=True)
                h0_ref[d, 0, :, lanes] = _dot(scores.astype(BF16), v[:, lanes]).astype(BF16)
                pack = jnp.where(lane == base + nh, cmax, jnp.where(lane == base + 3 * nh, rowsum, pack))
    g_ref[0] = pack[:, :g_ref.shape[2]]


def _gate_lane_layout():
    nh = MLSTM_HEADS
    src = np.full((V7X_LANES,), -1, np.int32)
    for d in range(2):
        for g in range(GATE_PACK_LANES // nh):
            for h in range(nh):
                lane = d * GATE_PACK_LANES + g * nh + h
                if g in (0, 2, 4, 5):
                    src[lane] = (2 * d + 1) * nh + h
                if g in (2, 5):
                    src[V7X_LANES // 2 + lane] = 2 * d * nh + h
    return src


def _mlstm_qkv(xc, xm, w_q, w_k, w_v, w_gate, b_gate, emit_intra):
    bsz, s, e = xc.shape
    t = _token_tile(s, SCAN_CHUNK)
    k_scale = float(e // MLSTM_HEADS) ** -0.5
    src = _gate_lane_layout()
    lay = lambda w: jnp.where(src >= 0, w[:, np.maximum(src, 0)], 0.0)
    tile = pl.BlockSpec((1, t, e), lambda b, j: (b, j, 0))
    out = jax.ShapeDtypeStruct((bsz, s, e), BF16)
    out_specs = [tile, tile, tile, pl.BlockSpec((1, t, 2 * GATE_PACK_LANES), lambda b, j: (b, j, 0))]
    out_shape = [out, out, out, jax.ShapeDtypeStruct((bsz, s, 2 * GATE_PACK_LANES), F32)]
    if emit_intra:
        out_specs.append(pl.BlockSpec((2, 1, t, e), lambda b, j: (0, b, j, 0)))
        out_shape.append(jax.ShapeDtypeStruct((2, bsz, s, e), BF16))
    outs = pl.pallas_call(
        functools.partial(_mlstm_qkv_kernel, k_scale=k_scale, emit_intra=emit_intra),
        grid=(bsz, s // t),
        in_specs=[tile, tile, _resident((e, e)), _resident((e, e)), _resident((e, e)),
                  _resident((3 * e, V7X_LANES)), _resident((1, V7X_LANES))],
        out_specs=out_specs,
        out_shape=out_shape,
        compiler_params=_params("parallel", "parallel"),
        name="mlstm_qkv",
    )(xc, xm, w_q.astype(BF16), w_k.astype(BF16), w_v.astype(BF16), lay(w_gate).astype(BF16),
      lay(_row(b_gate)))
    return tuple(outs) if emit_intra else tuple(outs) + (None,)


def _mlstm_scan_kernel(*refs, has_init, emit_h, emit_state):
    refs = list(refs)
    q_ref, k_ref, v_ref, g_ref = refs[:4]
    del refs[:4]
    if emit_h:
        h0_ref = refs.pop(0)
    if has_init:
        c0_ref, n0_ref, m0_ref = refs[:3]
        del refs[:3]
    if emit_h:
        o_ref = refs.pop(0)
    if emit_state:
        cf_ref, nf_ref, mf_ref = refs[:3]
        del refs[:3]
    c_ref, n_ref, m_ref = refs

    length = q_ref.shape[1]
    nh, dh = c_ref.shape[0], c_ref.shape[1]
    step = pl.program_id(2)

    @pl.when(step == 0)
    def _():
        if has_init:
            c_ref[...] = c0_ref[0, 0]
            n_ref[...] = n0_ref[0, 0]
            m_ref[...] = m0_ref[0, 0]
        else:
            c_ref[...] = jnp.zeros(c_ref.shape, F32)
            n_ref[...] = jnp.zeros(n_ref.shape, F32)
            m_ref[...] = jnp.zeros(m_ref.shape, F32)

    g = g_ref[0, 0]
    grp = lambda i: g[:, i * nh:(i + 1) * nh]
    b_t, cmax, a_t, rowsum = grp(0), grp(1), grp(2), grp(3)
    total, amax = grp(4)[0:1], grp(5)[0:1]
    m_old = m_ref[0:1, 0:nh]
    m_new = total + jnp.maximum(m_old, amax)
    w_k = jnp.exp(total + a_t - m_new)
    decay = jnp.exp(total + m_old - m_new)
    if emit_h:
        m_row = jnp.maximum(m_old, cmax)
        r_intra = jnp.exp(cmax - m_row)
        w_inter = jnp.exp(m_old - m_row)
        den_floor = jnp.exp(-(b_t + m_row))
    ones_rows = jnp.ones((V7X_SUBLANES, length), BF16)

    for h in range(nh):
        lanes = slice(h * dh, (h + 1) * dh)
        col = slice(h, h + 1)
        k = k_ref[0, :, lanes]
        v = v_ref[0, :, lanes]
        if emit_h:
            q = q_ref[0, :, lanes]
            q_n = lax.dot_general(q, n_ref[h].astype(BF16), (((1,), (1,)), ((), ())),
                                  preferred_element_type=F32)[:, 0:1]
            num = (r_intra[:, col] * h0_ref[0, 0, :, lanes].astype(F32)
                   + w_inter[:, col] * _dot(q, c_ref[h].astype(BF16)))
            den = r_intra[:, col] * rowsum[:, col] + w_inter[:, col] * q_n
            o_ref[0, 0, :, lanes] = (num / jnp.maximum(jnp.abs(den), den_floor[:, col])).astype(o_ref.dtype)

        wk = (w_k[:, col] * k.astype(F32)).astype(BF16)
        c_ref[h] = decay[:, col] * c_ref[h] + lax.dot_general(wk, v, (((0,), (0,)), ((), ())),
                                                              preferred_element_type=F32)
        n_ref[h] = decay[:, col] * n_ref[h] + _dot(ones_rows, wk)
    m_ref[:, 0:nh] = jnp.broadcast_to(m_new, (m_ref.shape[0], nh))

    if emit_state:
        @pl.when(step == pl.num_programs(2) - 1)
        def _():
            cf_ref[0, 0] = c_ref[...]
            nf_ref[0, 0] = n_ref[...]
            mf_ref[0, 0] = m_ref[...]


def _mlstm_scan(q, k, v, gates, h0, init_state, emit_state):
    bsz, s, e = q.shape
    nh = MLSTM_HEADS
    dh = e // nh
    emit_h = h0 is not None
    length = _token_tile(s, SCAN_CHUNK)
    nch = s // length
    g = gates.reshape(bsz, s, 2, GATE_PACK_LANES).transpose(0, 2, 1, 3)
    chunk = lambda d, st: jnp.where(d == 0, st, nch - 1 - st)
    tile = pl.BlockSpec((1, length, e), lambda b, d, st: (b, chunk(d, st), 0))
    dir_tile = pl.BlockSpec((1, 1, length, e), lambda b, d, st: (d, b, chunk(d, st), 0))
    state_shapes = [(nh, dh, dh), (nh, V7X_SUBLANES, dh), (V7X_SUBLANES, V7X_LANES)]
    state_specs = [pl.BlockSpec((1, 1) + shp, lambda b, d, st, n=len(shp): (b, d) + (0,) * n)
                   for shp in state_shapes]
    scratch = [pltpu.VMEM(shp, F32) for shp in state_shapes]

    in_specs = [tile, tile, tile,
                pl.BlockSpec((1, 1, length, GATE_PACK_LANES), lambda b, d, st: (b, d, chunk(d, st), 0))]
    args = [q, k, v, g]
    if emit_h:
        in_specs.append(dir_tile)
        args.append(h0)
    if init_state is not None:
        in_specs += state_specs
        args += list(init_state)
    out_specs, out_shape = [], []
    if emit_h:
        out_specs.append(dir_tile)
        out_shape.append(jax.ShapeDtypeStruct((2, bsz, s, e), BF16))
    if emit_state:
        out_specs += state_specs
        out_shape += [jax.ShapeDtypeStruct((bsz, 2) + shp, F32) for shp in state_shapes]
    outs = pl.pallas_call(
        functools.partial(_mlstm_scan_kernel, has_init=init_state is not None, emit_h=emit_h,
                          emit_state=emit_state),
        grid=(bsz, 2, nch),
        in_specs=in_specs,
        out_specs=out_specs,
        out_shape=out_shape,
        scratch_shapes=scratch,
        compiler_params=_params("parallel", "parallel", "arbitrary"),
        name="mlstm_scan",
    )(*args)
    outs = list(outs)
    h = outs.pop(0) if emit_h else None
    return h, (tuple(outs) if emit_state else None)


def _mlstm_out_kernel(h_ref, xm_ref, xc_ref, z_ref, hs_ref, mod_ref, wo_ref, bo_ref, gn_ref, skip_ref,
                      wdown_ref, lng_ref, lnb_ref, o_ref, *, alpha):
    e = xm_ref.shape[2]
    dh = e // MLSTM_HEADS
    h = h_ref[0]
    xm = xm_ref[0]
    acc = jnp.zeros(h.shape, F32)

    def gate_logits(hd):
        lanes = slice(hd * dh, (hd + 1) * dh)
        lanes_b = slice(e + hd * dh, e + (hd + 1) * dh)
        return (_dot(xm, wo_ref[:, lanes]) + bo_ref[:, lanes],
                _dot(xm, wo_ref[:, lanes_b]) + bo_ref[:, lanes_b])

    logits = gate_logits(0)
    for hd in range(MLSTM_HEADS):
        lanes = slice(hd * dh, (hd + 1) * dh)
        o_f, o_b = jax.nn.sigmoid(logits[0]), jax.nn.sigmoid(logits[1])
        if hd + 1 < MLSTM_HEADS:
            logits = gate_logits(hd + 1)
        hsum = o_f * hs_ref[0, 0, :, lanes].astype(F32) + o_b * hs_ref[1, 0, :, lanes].astype(F32)
        mu = jnp.mean(hsum, axis=-1, keepdims=True)
        dlt = hsum - mu
        var = jnp.mean(dlt * dlt, axis=-1, keepdims=True)
        hn = dlt * lax.rsqrt(var + LN_EPS) * gn_ref[:, lanes]
        y = (hn + skip_ref[:, lanes] * xc_ref[0, :, lanes].astype(F32)) * _silu(z_ref[0, :, lanes].astype(F32))
        acc = acc + _dot(y.astype(BF16), wdown_ref[lanes, :])
    g1 = mod_ref[0, 2:3, :]
    o_ref[0] = _layer_norm(alpha * h + g1 * acc, lng_ref[...], lnb_ref[...])


def _mlstm_out(h, xm, xc, z, hs, mod, w_o, b_o, gn_g, skip, w_down, ln_g, ln_b, alpha):
    bsz, s, d = h.shape
    e = xm.shape[2]
    t = _token_tile(s, NARROW_TOKEN_TILE)
    tile = lambda width: pl.BlockSpec((1, t, width), lambda b, j: (b, j, 0))
    return pl.pallas_call(
        functools.partial(_mlstm_out_kernel, alpha=alpha),
        grid=(bsz, s // t),
        in_specs=[tile(d), tile(e), tile(e), tile(e),
                  pl.BlockSpec((2, 1, t, e), lambda b, j: (0, b, j, 0)),
                  pl.BlockSpec((1, MOD_ROWS, d), lambda b, j: (b, 0, 0)),
                  _resident((e, 2 * e)), _resident((1, 2 * e)), _resident((1, e)), _resident((1, e)),
                  _resident((e, d)), _resident((1, d)), _resident((1, d))],
        out_specs=pl.BlockSpec((1, t, d), lambda b, j: (b, j, 0)),
        out_shape=jax.ShapeDtypeStruct((bsz, s, d), F32),
        compiler_params=_params("parallel", "parallel"),
        name="mlstm_out",
    )(h, xm, xc, z, hs, mod, w_o.astype(BF16), _row(b_o), _row(gn_g), _row(skip),
      w_down.astype(BF16), _row(ln_g), _row(ln_b))


def _mod_tables(ada_out_i, bsz, d):
    m = ada_out_i.reshape(MOD_ROWS, 6, d)
    pad = ((0, 0), (0, MOD_ROWS - 6), (0, 0))
    lat = jnp.pad(m[:bsz], pad)
    ctx = jnp.pad(jnp.broadcast_to(m[bsz][None], (bsz, 6, d)), pad)
    return lat, ctx


def kernel(x, c, ctx, c_ctx, ada_w, ada_b, ln1_g, ln1_b, ln2_g, ln2_b, mlp_w1, mlp_w2, conv_w_pw1, conv_b_pw1, conv_w_dw, conv_b_dw, conv_ln_g, conv_ln_b, conv_w_pw2, conv_b_pw2, ml_w_up, ml_w_conv, ml_b_conv, ml_w_q, ml_w_k, ml_w_v, ml_w_gate, ml_b_gate, ml_w_o, ml_b_o, ml_gn_g, ml_skip, ml_w_down):
    bsz, s, d = x.shape
    depth = ada_w.shape[0]
    assert depth == 2 and bsz + 1 <= MOD_ROWS, "built for the two-layer block: conv mixer, then mLSTM"
    alpha = (2 * depth) ** 0.25

    cvec = jnp.concatenate([c, c_ctx[None], jnp.zeros((MOD_ROWS - bsz - 1, d), F32)], axis=0)
    ada = _adaln(cvec, ada_w, ada_b)
    mod0, mod0_ctx = _mod_tables(ada[0], bsz, d)
    mod1, mod1_ctx = _mod_tables(ada[1], bsz, d)

    conv_p = dict(w_pw1=conv_w_pw1[0], b_pw1=conv_b_pw1[0], w_dw=conv_w_dw[0], b_dw=conv_b_dw[0],
                  ln_g=conv_ln_g[0], ln_b=conv_ln_b[0], w_pw2=conv_w_pw2[0], b_pw2=conv_b_pw2[0],
                  ln1_g=ln1_g[0], ln1_b=ln1_b[0])
    h = _conv_mixer_layer(x, GRID_W, mod0, conv_p, alpha)
    h = _mlp_layer(h, mod0, mlp_w1[0], mlp_w2[0], ln2_g[0], ln2_b[0], alpha)
    hc = _conv_mixer_layer(ctx, ctx.shape[1], mod0_ctx, conv_p, alpha)
    hc = _mlp_layer(hc, mod0_ctx, mlp_w1[0], mlp_w2[0], ln2_g[0], ln2_b[0], alpha)

    feats = lambda hh, mm: _mlstm_up(hh, mm, ml_w_up[0], ml_w_conv[0], ml_b_conv[0])
    qkv = lambda xc_, xm_, intra: _mlstm_qkv(xc_, xm_, ml_w_q[0], ml_w_k[0], ml_w_v[0], ml_w_gate[0],
                                             ml_b_gate[0], emit_intra=intra)
    xm_c, xc_c, _ = feats(hc, mod1_ctx)
    _, ctx_state = _mlstm_scan(*qkv(xc_c, xm_c, False), init_state=None, emit_state=True)
    xm, xc, z = feats(h, mod1)
    hs, _ = _mlstm_scan(*qkv(xc, xm, True), init_state=ctx_state, emit_state=False)
    h = _mlstm_out(h, xm, xc, z, hs, mod1, ml_w_o[0], ml_b_o[0], ml_gn_g[0], ml_skip[0],
                   ml_w_down[0], ln1_g[1], ln1_b[1], alpha)
    return _mlp_layer(h, mod1, mlp_w1[1], mlp_w2[1], ln2_g[1], ln2_b[1], alpha)
```

```python
import functools

import jax
import jax.numpy as jnp
import numpy as np
from jax import lax
from jax.experimental import pallas as pl
from jax.experimental.pallas import tpu as pltpu

F32 = jnp.float32
BF16 = jnp.bfloat16

GRID_W = 64
CONV_WIDTH = 31
QKV_CONV_WIDTH = 5
MLSTM_HEADS = 4
GATE_PACK_LANES = 8 * MLSTM_HEADS
LN_EPS = 1e-5

V7X_LANES = 128
V7X_SUBLANES = 8
V7X_BF16_ROWS = 16
V7X_MXU_DIM = 256
V7X_VMEM_LIMIT_BYTES = 56 * 1024 * 1024

WIDE_TOKEN_TILE = 512
NARROW_TOKEN_TILE = 256
SCAN_CHUNK = 256
MLP_FF_CHUNK = 1024
CONV_ROW_BLOCK = 64
CONV_PAD_ROWS = 16
MOD_ROWS = 8
NEG_BIG = -1e30


def _resident(shape):
    zeros = (0,) * len(shape)
    return pl.BlockSpec(shape, lambda *_: zeros, pipeline_mode=pl.Buffered(1))


def _params(*semantics):
    return pltpu.CompilerParams(dimension_semantics=semantics,
                                vmem_limit_bytes=V7X_VMEM_LIMIT_BYTES)


def _token_tile(s, preferred):
    t = min(s, preferred)
    assert s % t == 0
    return t


def _layer_norm(r, g, b):
    mu = jnp.mean(r, axis=-1, keepdims=True)
    d = r - mu
    var = jnp.mean(d * d, axis=-1, keepdims=True)
    return d * lax.rsqrt(var + LN_EPS) * g + b


def _silu(x):
    return x * jax.nn.sigmoid(x)


def _dot(a, b):
    return jnp.dot(a, b, preferred_element_type=F32)


def _row(v):
    return v.reshape(1, -1)


def _adaln_kernel(c_ref, w_ref, b_ref, o_ref):
    o_ref[0] = _dot(_silu(c_ref[...]), w_ref[0]) + b_ref[0]


def _adaln(cvec, ada_w, ada_b):
    depth, d, n = ada_w.shape
    tn = n // 4
    return pl.pallas_call(
        _adaln_kernel,
        grid=(depth, n // tn),
        in_specs=[pl.BlockSpec((MOD_ROWS, d), lambda i, j: (0, 0)),
                  pl.BlockSpec((1, d, tn), lambda i, j: (i, 0, j)),
                  pl.BlockSpec((1, 1, tn), lambda i, j: (i, 0, j))],
        out_specs=pl.BlockSpec((1, MOD_ROWS, tn), lambda i, j: (i, 0, j)),
        out_shape=jax.ShapeDtypeStruct((depth, MOD_ROWS, n), F32),
        compiler_params=_params("parallel", "parallel"),
        name="adaln",
    )(cvec, ada_w, ada_b.reshape(depth, 1, n))


def _conv_mixer_kernel(h_ref, mod_ref, wpw1_ref, bpw1_ref, wdw_ref, bdw_ref, cg_ref, cb_ref,
                       wpw2_ref, bpw2_ref, lng_ref, lnb_ref, o_ref, pad_ref, conv_ref, *, alpha):
    c = wpw2_ref.shape[0]
    ncb, nseg, padded, _ = pad_ref.shape
    seg = padded - 2 * CONV_PAD_ROWS
    left = (CONV_WIDTH - 1) // 2
    pair = V7X_MXU_DIM // V7X_LANES
    h = h_ref[0]
    sh, sc, g1 = mod_ref[0, 0:1, :], mod_ref[0, 1:2, :], mod_ref[0, 2:3, :]
    u = (h * (1.0 + sc) + sh).astype(BF16)
    zero_rows = jnp.zeros((CONV_PAD_ROWS, V7X_LANES), F32)

    def glu_slab(p):
        lo = slice(p * V7X_MXU_DIM, (p + 1) * V7X_MXU_DIM)
        hi = slice(c + p * V7X_MXU_DIM, c + (p + 1) * V7X_MXU_DIM)
        a = _dot(u, wpw1_ref[:, lo]) + bpw1_ref[:, lo]
        gate = _dot(u, wpw1_ref[:, hi]) + bpw1_ref[:, hi]
        glu = a * jax.nn.sigmoid(gate)
        for q in range(pair):
            cb = pair * p + q
            for s in range(nseg):
                pad_ref[cb, s, 0:CONV_PAD_ROWS, :] = zero_rows
                pad_ref[cb, s, CONV_PAD_ROWS + seg:, :] = zero_rows
                pad_ref[cb, s, CONV_PAD_ROWS:CONV_PAD_ROWS + seg, :] = (
                    glu[s * seg:(s + 1) * seg, q * V7X_LANES:(q + 1) * V7X_LANES])

    def conv_block(cb):
        for s in range(nseg):
            for r0 in range(0, seg, CONV_ROW_BLOCK):
                acc = jnp.zeros((CONV_ROW_BLOCK, V7X_LANES), F32)
                for k in range(CONV_WIDTH):
                    start = CONV_PAD_ROWS + r0 - left + k
                    acc = acc + pad_ref[cb, s, start:start + CONV_ROW_BLOCK, :] * wdw_ref[cb, k:k + 1, :]
                conv_ref[cb, s * seg + r0:s * seg + r0 + CONV_ROW_BLOCK, :] = acc

    glu_slab(0)
    for p in range(ncb // pair):
        if p + 1 < ncb // pair:
            glu_slab(p + 1)
        for q in range(pair):
            conv_block(pair * p + q)

    conv = jnp.concatenate([conv_ref[cb] for cb in range(ncb)], axis=1) + bdw_ref[...]
    act = _silu(_layer_norm(conv, cg_ref[...], cb_ref[...])).astype(BF16)
    y = _dot(act, wpw2_ref[...]) + bpw2_ref[...]
    o_ref[0] = _layer_norm(alpha * h + g1 * y, lng_ref[...], lnb_ref[...])


def _conv_mixer_layer(h, seg, mod, p, alpha):
    bsz, s, d = h.shape
    t = _token_tile(s, WIDE_TOKEN_TILE)
    assert t % seg == 0 and seg % CONV_ROW_BLOCK == 0
    c = p["w_pw2"].shape[0]
    ncb = c // V7X_LANES
    taps = p["w_dw"].shape[0]
    taps_padded = -(-taps // V7X_SUBLANES) * V7X_SUBLANES
    w_dw = jnp.pad(p["w_dw"], ((0, taps_padded - taps), (0, 0)))
    w_dw = w_dw.reshape(taps_padded, ncb, V7X_LANES).transpose(1, 0, 2)
    return pl.pallas_call(
        functools.partial(_conv_mixer_kernel, alpha=alpha),
        grid=(bsz, s // t),
        in_specs=[pl.BlockSpec((1, t, d), lambda b, j: (b, j, 0)),
                  pl.BlockSpec((1, MOD_ROWS, d), lambda b, j: (b, 0, 0)),
                  _resident((d, 2 * c)), _resident((1, 2 * c)),
                  _resident((ncb, taps_padded, V7X_LANES)), _resident((1, c)),
                  _resident((1, c)), _resident((1, c)),
                  _resident((c, d)), _resident((1, d)), _resident((1, d)), _resident((1, d))],
        out_specs=pl.BlockSpec((1, t, d), lambda b, j: (b, j, 0)),
        out_shape=jax.ShapeDtypeStruct((bsz, s, d), F32),
        scratch_shapes=[pltpu.VMEM((ncb, t // seg, seg + 2 * CONV_PAD_ROWS, V7X_LANES), F32),
                        pltpu.VMEM((ncb, t, V7X_LANES), F32)],
        compiler_params=_params("parallel", "parallel"),
        name="conv_mixer_layer",
    )(h, mod, p["w_pw1"].astype(BF16), _row(p["b_pw1"]), w_dw, _row(p["b_dw"]),
      _row(p["ln_g"]), _row(p["ln_b"]), p["w_pw2"].astype(BF16), _row(p["b_pw2"]),
      _row(p["ln1_g"]), _row(p["ln1_b"]))


def _mlp_kernel(h_ref, mod_ref, w1_ref, w2_ref, g_ref, b_ref, o_ref, *, alpha):
    h = h_ref[0]
    sh, sc, g2 = mod_ref[0, 3:4, :], mod_ref[0, 4:5, :], mod_ref[0, 5:6, :]
    u = (h * (1.0 + sc) + sh).astype(BF16)
    ff = w1_ref.shape[1]
    acc = jnp.zeros(h.shape, F32)
    for f0 in range(0, ff, MLP_FF_CHUNK):
        hid = jnp.maximum(_dot(u, w1_ref[:, f0:f0 + MLP_FF_CHUNK]), 0.0)
        acc = acc + _dot((hid * hid).astype(BF16), w2_ref[f0:f0 + MLP_FF_CHUNK, :])
    o_ref[0] = _layer_norm(alpha * h + g2 * acc, g_ref[...], b_ref[...])


def _mlp_layer(h, mod, w1, w2, ln_g, ln_b, alpha):
    bsz, s, d = h.shape
    t = _token_tile(s, WIDE_TOKEN_TILE)
    ff = w1.shape[1]
    assert ff % MLP_FF_CHUNK == 0
    return pl.pallas_call(
        functools.partial(_mlp_kernel, alpha=alpha),
        grid=(bsz, s // t),
        in_specs=[pl.BlockSpec((1, t, d), lambda b, j: (b, j, 0)),
                  pl.BlockSpec((1, MOD_ROWS, d), lambda b, j: (b, 0, 0)),
                  _resident((d, ff)), _resident((ff, d)), _resident((1, d)), _resident((1, d))],
        out_specs=pl.BlockSpec((1, t, d), lambda b, j: (b, j, 0)),
        out_shape=jax.ShapeDtypeStruct((bsz, s, d), F32),
        compiler_params=_params("parallel", "parallel"),
        name="mlp_layer",
    )(h, mod, w1.astype(BF16), w2.astype(BF16), _row(ln_g), _row(ln_b))


def _mlstm_up_kernel(h_ref, prev_ref, next_ref, mod_ref, wup_ref, wconv_ref, bconv_ref,
                     xm_ref, xc_ref, z_ref, u_ref, xm_all_ref):
    t = h_ref.shape[1]
    e = xm_ref.shape[2]
    halo = prev_ref.shape[1]
    slab = 2 * V7X_MXU_DIM
    per_slab = slab // V7X_LANES
    left = (QKV_CONV_WIDTH - 1) // 2
    j = pl.program_id(1)
    sh, sc = mod_ref[0, 0:1, :], mod_ref[0, 1:2, :]
    modulate = lambda v: (v * (1.0 + sc) + sh).astype(BF16)
    u_ref[0:halo, :] = modulate(prev_ref[0])
    u_ref[halo:halo + t, :] = modulate(h_ref[0])
    u_ref[halo + t:, :] = modulate(next_ref[0])
    keep_prev = jnp.where(j == 0, 0.0, 1.0)
    keep_next = jnp.where(j == pl.num_programs(1) - 1, 0.0, 1.0)

    def xm_matmul(p):
        return _dot(u_ref[...], wup_ref[:, p * slab:(p + 1) * slab])

    def xm_store(p, xm_all):
        cols = slice(p * slab, (p + 1) * slab)
        xm_ref[0, :, cols] = xm_all[halo:halo + t].astype(BF16)
        for q in range(per_slab):
            cb = per_slab * p + q
            lanes = slice(q * V7X_LANES, (q + 1) * V7X_LANES)
            xm_all_ref[cb, 0:halo, :] = xm_all[0:halo, lanes] * keep_prev
            xm_all_ref[cb, halo:halo + t, :] = xm_all[halo:halo + t, lanes]
            xm_all_ref[cb, halo + t:, :] = xm_all[halo + t:, lanes] * keep_next

    def conv_block(cb):
        lanes = slice(cb * V7X_LANES, (cb + 1) * V7X_LANES)
        for r0 in range(0, t, CONV_ROW_BLOCK):
            acc = jnp.zeros((CONV_ROW_BLOCK, V7X_LANES), F32)
            for k in range(QKV_CONV_WIDTH):
                start = halo + r0 - left + k
                acc = acc + xm_all_ref[cb, start:start + CONV_ROW_BLOCK, :] * wconv_ref[k:k + 1, lanes]
            xc_ref[0, r0:r0 + CONV_ROW_BLOCK, lanes] = _silu(acc + bconv_ref[:, lanes]).astype(BF16)

    nslab = e // slab
    for p in range(nslab):
        xm_store(p, xm_matmul(p))
    for p in range(nslab):
        cols = slice(e + p * slab, e + (p + 1) * slab)
        z_ref[0, :, p * slab:(p + 1) * slab] = _dot(u_ref[halo:halo + t, :], wup_ref[:, cols]).astype(BF16)
        for q in range(per_slab):
            conv_block(per_slab * p + q)


def _mlstm_up(h, mod, w_up, w_conv, b_conv):
    bsz, s, d = h.shape
    e = w_up.shape[1] // 2
    t = _token_tile(s, WIDE_TOKEN_TILE)
    halo = V7X_BF16_ROWS
    assert t % halo == 0 and t % CONV_ROW_BLOCK == 0
    nhb = s // halo
    taps = w_conv.shape[0]
    w_conv = jnp.pad(w_conv, ((0, V7X_SUBLANES - taps), (0, 0)))
    tile = pl.BlockSpec((1, t, e), lambda b, j: (b, j, 0))
    out = jax.ShapeDtypeStruct((bsz, s, e), BF16)
    return pl.pallas_call(
        _mlstm_up_kernel,
        grid=(bsz, s // t),
        in_specs=[pl.BlockSpec((1, t, d), lambda b, j: (b, j, 0)),
                  pl.BlockSpec((1, halo, d), lambda b, j: (b, jnp.maximum(j * (t // halo) - 1, 0), 0)),
                  pl.BlockSpec((1, halo, d), lambda b, j: (b, jnp.minimum((j + 1) * (t // halo), nhb - 1), 0)),
                  pl.BlockSpec((1, MOD_ROWS, d), lambda b, j: (b, 0, 0)),
                  _resident((d, 2 * e)), _resident((V7X_SUBLANES, e)), _resident((1, e))],
        out_specs=[tile, tile, tile],
        out_shape=[out, out, out],
        scratch_shapes=[pltpu.VMEM((t + 2 * halo, d), BF16),
                        pltpu.VMEM((e // V7X_LANES, t + 2 * halo, V7X_LANES), F32)],
        compiler_params=_params("parallel", "parallel"),
        name="mlstm_up",
    )(h, h, h, mod, w_up.astype(BF16), w_conv, _row(b_conv))


def _log_sigmoid(x):
    return jnp.minimum(x, 0.0) - jnp.log1p(jnp.exp(-jnp.abs(x)))


def _split_bf16x3(x):
    hi = x.astype(BF16)
    r1 = x - hi.astype(F32)
    mid = r1.astype(BF16)
    lo = (r1 - mid.astype(F32)).astype(BF16)
    return jnp.concatenate([hi, mid, lo], axis=1)


def _mlstm_qkv_kernel(*refs, k_scale, emit_intra):
    refs = list(refs)
    xc_ref, xm_ref, wq_ref, wk_ref, wv_ref, wg_ref, bg_ref, q_ref, kt_ref, v_ref, g_ref = refs[:11]
    h0_ref = refs[11] if emit_intra else None
    e = wq_ref.shape[0]
    t = xc_ref.shape[1]
    nh = MLSTM_HEADS
    dh = e // nh
    half = V7X_LANES // 2
    xc = xc_ref[0]
    q = _dot(xc, wq_ref[...]).astype(BF16)
    q_ref[0] = q
    gates = _dot(q, wg_ref[0:e, :])
    k = _dot(xc, wk_ref[...])
    kt_ref[0] = (k * k_scale).T.astype(BF16)
    gates = gates + _dot(k.astype(BF16), wg_ref[e:2 * e, :])
    v = _dot(xm_ref[0], wv_ref[...]).astype(BF16)
    v_ref[0] = v
    gates = gates + _dot(v, wg_ref[2 * e:, :]) + bg_ref[...]

    lane = lax.broadcasted_iota(jnp.int32, (1, V7X_LANES), 1)
    group = (lane % GATE_PACK_LANES) // nh
    rows = lax.broadcasted_iota(jnp.int32, (t, t), 0)
    cols = lax.broadcasted_iota(jnp.int32, (t, t), 1)
    visible = (rows >= cols, rows <= cols)
    lf = _log_sigmoid(gates)
    parts = _split_bf16x3(lf)
    fold = lambda c3: c3[:, :V7X_LANES] + c3[:, V7X_LANES:2 * V7X_LANES] + c3[:, 2 * V7X_LANES:]
    prefix = fold(_dot(jnp.where(visible[0], 1.0, 0.0).astype(BF16), parts))
    suffix = fold(_dot(jnp.where(visible[1], 1.0, 0.0).astype(BF16), parts))
    b = jnp.where(lane < GATE_PACK_LANES, prefix, suffix)
    a = pltpu.roll(gates, half, 1) - b
    total = jnp.sum(lf, axis=0, keepdims=True)
    amax = jnp.max(a, axis=0, keepdims=True)
    pack = jnp.where(group == 0, b, jnp.where(group == 2, a, jnp.where(group == 4, total,
                                                                         jnp.where(group == 5, amax, 0.0))))
    if emit_intra:
        a_rows = a.T
        qk_of = lambda hd: _dot(q[:, hd * dh:(hd + 1) * dh], kt_ref[0, hd * dh:(hd + 1) * dh, :])
        qk_next = qk_of(0)
        for hd in range(nh):
            lanes = slice(hd * dh, (hd + 1) * dh)
            qk = qk_next
            if hd + 1 < nh:
                qk_next = qk_of(hd + 1)
            for d in range(2):
                base = d * GATE_PACK_LANES + hd
                a_row = a_rows[base + 2 * nh:base + 2 * nh + 1, :]
                masked = jnp.where(visible[d], a_row, NEG_BIG)
                cmax = jnp.max(masked, axis=1, keepdims=True)
                scores = qk * jnp.exp(masked - cmax)
                rowsum = jnp.sum(scores, axis=1, keepdims=True)
                h0_ref[d, 0, :, lanes] = _dot(scores.astype(BF16), v[:, lanes]).astype(BF16)
                pack = jnp.where(lane == base + nh, cmax, jnp.where(lane == base + 3 * nh, rowsum, pack))
    g_ref[0] = pack[:, :g_ref.shape[2]]


def _gate_lane_layout():
    nh = MLSTM_HEADS
    src = np.full((V7X_LANES,), -1, np.int32)
    for d in range(2):
        for g in range(GATE_PACK_LANES // nh):
            for h in range(nh):
                lane = d * GATE_PACK_LANES + g * nh + h
                if g in (0, 2, 4, 5):
                    src[lane] = (2 * d + 1) * nh + h
                if g in (2, 5):
                    src[V7X_LANES // 2 + lane] = 2 * d * nh + h
    return src


def _mlstm_qkv(xc, xm, w_q, w_k, w_v, w_gate, b_gate, emit_intra):
    bsz, s, e = xc.shape
    t = _token_tile(s, SCAN_CHUNK)
    k_scale = float(e // MLSTM_HEADS) ** -0.5
    src = _gate_lane_layout()
    lay = lambda w: jnp.where(src >= 0, w[:, np.maximum(src, 0)], 0.0)
    tile = pl.BlockSpec((1, t, e), lambda b, j: (b, j, 0))
    out = jax.ShapeDtypeStruct((bsz, s, e), BF16)
    out_specs = [tile, pl.BlockSpec((1, e, t), lambda b, j: (b, 0, j)), tile,
                 pl.BlockSpec((1, t, 2 * GATE_PACK_LANES), lambda b, j: (b, j, 0))]
    out_shape = [out, jax.ShapeDtypeStruct((bsz, e, s), BF16), out,
                 jax.ShapeDtypeStruct((bsz, s, 2 * GATE_PACK_LANES), F32)]
    if emit_intra:
        out_specs.append(pl.BlockSpec((2, 1, t, e), lambda b, j: (0, b, j, 0)))
        out_shape.append(jax.ShapeDtypeStruct((2, bsz, s, e), BF16))
    outs = pl.pallas_call(
        functools.partial(_mlstm_qkv_kernel, k_scale=k_scale, emit_intra=emit_intra),
        grid=(bsz, s // t),
        in_specs=[tile, tile, _resident((e, e)), _resident((e, e)), _resident((e, e)),
                  _resident((3 * e, V7X_LANES)), _resident((1, V7X_LANES))],
        out_specs=out_specs,
        out_shape=out_shape,
        compiler_params=_params("parallel", "parallel"),
        name="mlstm_qkv",
    )(xc, xm, w_q.astype(BF16), w_k.astype(BF16), w_v.astype(BF16), lay(w_gate).astype(BF16),
      lay(_row(b_gate)))
    return tuple(outs) if emit_intra else tuple(outs) + (None,)


def _mlstm_scan_kernel(*refs, has_init, emit_h, emit_state):
    refs = list(refs)
    q_ref, kt_ref, v_ref, g_ref, gt_ref = refs[:5]
    del refs[:5]
    if emit_h:
        h0_ref = refs.pop(0)
    if has_init:
        c0_ref, n0_ref, m0_ref = refs[:3]
        del refs[:3]
    if emit_h:
        o_ref = refs.pop(0)
    if emit_state:
        cf_ref, nf_ref, mf_ref = refs[:3]
        del refs[:3]
    c_ref, n_ref, m_ref = refs

    length = q_ref.shape[1]
    nh, dh = c_ref.shape[0], c_ref.shape[1]
    step = pl.program_id(2)

    @pl.when(step == 0)
    def _():
        if has_init:
            c_ref[...] = c0_ref[0, 0]
            n_ref[...] = n0_ref[0, 0]
            m_ref[...] = m0_ref[0, 0]
        else:
            c_ref[...] = jnp.zeros(c_ref.shape, F32)
            n_ref[...] = jnp.zeros(n_ref.shape, F32)
            m_ref[...] = jnp.zeros(m_ref.shape, F32)

    g = g_ref[0, 0]
    gt = gt_ref[0, 0]
    grp = lambda i: g[:, i * nh:(i + 1) * nh]
    grp_t = lambda i: gt[i * nh:(i + 1) * nh, :]
    m_old = m_ref[nh:nh + 1, 0:nh]
    m_old_t = m_ref[0:nh, 0:1]
    total, amax = grp(4)[0:1], grp(5)[0:1]
    m_new = total + jnp.maximum(m_old, amax)
    decay = jnp.exp(total + m_old - m_new)
    m_new_t = grp_t(4) + jnp.maximum(m_old_t, grp_t(5))
    w_k = jnp.exp(grp_t(4) + grp_t(2) - m_new_t)
    if emit_h:
        b_t, cmax, rowsum = grp(0), grp(1), grp(3)
        m_row = jnp.maximum(m_old, cmax)
        r_intra = jnp.exp(cmax - m_row)
        w_inter = jnp.exp(m_old - m_row)
        den_floor = jnp.exp(-(b_t + m_row))
    ones_rows = jnp.ones((V7X_SUBLANES, length), BF16)
    contract_lanes = (((1,), (1,)), ((), ()))

    for h in range(nh):
        lanes = slice(h * dh, (h + 1) * dh)
        col = slice(h, h + 1)
        v = v_ref[0, :, lanes]
        if emit_h:
            q = q_ref[0, :, lanes]
            q_n = lax.dot_general(q, n_ref[h].astype(BF16), contract_lanes,
                                  preferred_element_type=F32)[:, 0:1]
            num = (r_intra[:, col] * h0_ref[0, 0, :, lanes].astype(F32)
                   + w_inter[:, col] * _dot(q, c_ref[h].astype(BF16)))
            den = r_intra[:, col] * rowsum[:, col] + w_inter[:, col] * q_n
            o_ref[0, 0, :, lanes] = (num / jnp.maximum(jnp.abs(den), den_floor[:, col])).astype(o_ref.dtype)

    for h in range(nh):
        lanes = slice(h * dh, (h + 1) * dh)
        col = slice(h, h + 1)
        v = v_ref[0, :, lanes]
        wkt = (kt_ref[0, lanes, :].astype(F32) * w_k[h:h + 1, :]).astype(BF16)
        c_ref[h] = decay[:, col] * c_ref[h] + _dot(wkt, v)
        n_ref[h] = decay[:, col] * n_ref[h] + lax.dot_general(ones_rows, wkt, contract_lanes,
                                                              preferred_element_type=F32)
    m_ref[0:nh, :] = m_new_t[:, 0:V7X_LANES]
    m_ref[nh:nh + 1, 0:nh] = m_new

    if emit_state:
        @pl.when(step == pl.num_programs(2) - 1)
        def _():
            cf_ref[0, 0] = c_ref[...]
            nf_ref[0, 0] = n_ref[...]
            mf_ref[0, 0] = m_ref[...]


def _mlstm_scan(q, kt, v, gates, h0, init_state, emit_state):
    bsz, s, e = q.shape
    nh = MLSTM_HEADS
    dh = e // nh
    emit_h = h0 is not None
    length = _token_tile(s, SCAN_CHUNK)
    nch = s // length
    g = gates.reshape(bsz, s, 2, GATE_PACK_LANES).transpose(0, 2, 1, 3)
    g_t = g.transpose(0, 1, 3, 2)
    chunk = lambda d, st: jnp.where(d == 0, st, nch - 1 - st)
    tile = pl.BlockSpec((1, length, e), lambda b, d, st: (b, chunk(d, st), 0))
    dir_tile = pl.BlockSpec((1, 1, length, e), lambda b, d, st: (d, b, chunk(d, st), 0))
    state_shapes = [(nh, dh, dh), (nh, V7X_SUBLANES, dh), (V7X_SUBLANES, V7X_LANES)]
    state_specs = [pl.BlockSpec((1, 1) + shp, lambda b, d, st, n=len(shp): (b, d) + (0,) * n)
                   for shp in state_shapes]
    scratch = [pltpu.VMEM(shp, F32) for shp in state_shapes]

    in_specs = [tile, pl.BlockSpec((1, e, length), lambda b, d, st: (b, 0, chunk(d, st))), tile,
                pl.BlockSpec((1, 1, length, GATE_PACK_LANES), lambda b, d, st: (b, d, chunk(d, st), 0)),
                pl.BlockSpec((1, 1, GATE_PACK_LANES, length), lambda b, d, st: (b, d, 0, chunk(d, st)))]
    args = [q, kt, v, g, g_t]
    if emit_h:
        in_specs.append(dir_tile)
        args.append(h0)
    if init_state is not None:
        in_specs += state_specs
        args += list(init_state)
    out_specs, out_shape = [], []
    if emit_h:
        out_specs.append(dir_tile)
        out_shape.append(jax.ShapeDtypeStruct((2, bsz, s, e), BF16))
    if emit_state:
        out_specs += state_specs
        out_shape += [jax.ShapeDtypeStruct((bsz, 2) + shp, F32) for shp in state_shapes]
    outs = pl.pallas_call(
        functools.partial(_mlstm_scan_kernel, has_init=init_state is not None, emit_h=emit_h,
                          emit_state=emit_state),
        grid=(bsz, 2, nch),
        in_specs=in_specs,
        out_specs=out_specs,
        out_shape=out_shape,
        scratch_shapes=scratch,
        compiler_params=_params("parallel", "parallel", "arbitrary"),
        name="mlstm_scan",
    )(*args)
    outs = list(outs)
    h = outs.pop(0) if emit_h else None
    return h, (tuple(outs) if emit_state else None)


def _mlstm_out_kernel(h_ref, xm_ref, xc_ref, z_ref, hs_ref, mod_ref, wo_ref, bo_ref, gn_ref, skip_ref,
                      wdown_ref, lng_ref, lnb_ref, o_ref, *, alpha):
    e = xm_ref.shape[2]
    dh = e // MLSTM_HEADS
    h = h_ref[0]
    xm = xm_ref[0]
    acc = jnp.zeros(h.shape, F32)

    def gate_logits(hd):
        lanes = slice(hd * dh, (hd + 1) * dh)
        lanes_b = slice(e + hd * dh, e + (hd + 1) * dh)
        return (_dot(xm, wo_ref[:, lanes]) + bo_ref[:, lanes],
                _dot(xm, wo_ref[:, lanes_b]) + bo_ref[:, lanes_b])

    logits = gate_logits(0)
    for hd in range(MLSTM_HEADS):
        lanes = slice(hd * dh, (hd + 1) * dh)
        o_f, o_b = jax.nn.sigmoid(logits[0]), jax.nn.sigmoid(logits[1])
        if hd + 1 < MLSTM_HEADS:
            logits = gate_logits(hd + 1)
        hsum = o_f * hs_ref[0, 0, :, lanes].astype(F32) + o_b * hs_ref[1, 0, :, lanes].astype(F32)
        mu = jnp.mean(hsum, axis=-1, keepdims=True)
        dlt = hsum - mu
        var = jnp.mean(dlt * dlt, axis=-1, keepdims=True)
        hn = dlt * lax.rsqrt(var + LN_EPS) * gn_ref[:, lanes]
        y = (hn + skip_ref[:, lanes] * xc_ref[0, :, lanes].astype(F32)) * _silu(z_ref[0, :, lanes].astype(F32))
        acc = acc + _dot(y.astype(BF16), wdown_ref[lanes, :])
    g1 = mod_ref[0, 2:3, :]
    o_ref[0] = _layer_norm(alpha * h + g1 * acc, lng_ref[...], lnb_ref[...])


def _mlstm_out(h, xm, xc, z, hs, mod, w_o, b_o, gn_g, skip, w_down, ln_g, ln_b, alpha):
    bsz, s, d = h.shape
    e = xm.shape[2]
    t = _token_tile(s, NARROW_TOKEN_TILE)
    tile = lambda width: pl.BlockSpec((1, t, width), lambda b, j: (b, j, 0))
    return pl.pallas_call(
        functools.partial(_mlstm_out_kernel, alpha=alpha),
        grid=(bsz, s // t),
        in_specs=[tile(d), tile(e), tile(e), tile(e),
                  pl.BlockSpec((2, 1, t, e), lambda b, j: (0, b, j, 0)),
                  pl.BlockSpec((1, MOD_ROWS, d), lambda b, j: (b, 0, 0)),
                  _resident((e, 2 * e)), _resident((1, 2 * e)), _resident((1, e)), _resident((1, e)),
                  _resident((e, d)), _resident((1, d)), _resident((1, d))],
        out_specs=pl.BlockSpec((1, t, d), lambda b, j: (b, j, 0)),
        out_shape=jax.ShapeDtypeStruct((bsz, s, d), F32),
        compiler_params=_params("parallel", "parallel"),
        name="mlstm_out",
    )(h, xm, xc, z, hs, mod, w_o.astype(BF16), _row(b_o), _row(gn_g), _row(skip),
      w_down.astype(BF16), _row(ln_g), _row(ln_b))


def _mod_tables(ada_out_i, bsz, d):
    m = ada_out_i.reshape(MOD_ROWS, 6, d)
    pad = ((0, 0), (0, MOD_ROWS - 6), (0, 0))
    lat = jnp.pad(m[:bsz], pad)
    ctx = jnp.pad(jnp.broadcast_to(m[bsz][None], (bsz, 6, d)), pad)
    return lat, ctx


def kernel(x, c, ctx, c_ctx, ada_w, ada_b, ln1_g, ln1_b, ln2_g, ln2_b, mlp_w1, mlp_w2, conv_w_pw1, conv_b_pw1, conv_w_dw, conv_b_dw, conv_ln_g, conv_ln_b, conv_w_pw2, conv_b_pw2, ml_w_up, ml_w_conv, ml_b_conv, ml_w_q, ml_w_k, ml_w_v, ml_w_gate, ml_b_gate, ml_w_o, ml_b_o, ml_gn_g, ml_skip, ml_w_down):
    bsz, s, d = x.shape
    depth = ada_w.shape[0]
    assert depth == 2 and bsz + 1 <= MOD_ROWS, "built for the two-layer block: conv mixer, then mLSTM"
    alpha = (2 * depth) ** 0.25

    cvec = jnp.concatenate([c, c_ctx[None], jnp.zeros((MOD_ROWS - bsz - 1, d), F32)], axis=0)
    ada = _adaln(cvec, ada_w, ada_b)
    mod0, mod0_ctx = _mod_tables(ada[0], bsz, d)
    mod1, mod1_ctx = _mod_tables(ada[1], bsz, d)

    conv_p = dict(w_pw1=conv_w_pw1[0], b_pw1=conv_b_pw1[0], w_dw=conv_w_dw[0], b_dw=conv_b_dw[0],
                  ln_g=conv_ln_g[0], ln_b=conv_ln_b[0], w_pw2=conv_w_pw2[0], b_pw2=conv_b_pw2[0],
                  ln1_g=ln1_g[0], ln1_b=ln1_b[0])
    h = _conv_mixer_layer(x, GRID_W, mod0, conv_p, alpha)
    h = _mlp_layer(h, mod0, mlp_w1[0], mlp_w2[0], ln2_g[0], ln2_b[0], alpha)
    hc = _conv_mixer_layer(ctx, ctx.shape[1], mod0_ctx, conv_p, alpha)
    hc = _mlp_layer(hc, mod0_ctx, mlp_w1[0], mlp_w2[0], ln2_g[0], ln2_b[0], alpha)

    feats = lambda hh, mm: _mlstm_up(hh, mm, ml_w_up[0], ml_w_conv[0], ml_b_conv[0])
    qkv = lambda xc_, xm_, intra: _mlstm_qkv(xc_, xm_, ml_w_q[0], ml_w_k[0], ml_w_v[0], ml_w_gate[0],
                                             ml_b_gate[0], emit_intra=intra)
    xm_c, xc_c, _ = feats(hc, mod1_ctx)
    _, ctx_state = _mlstm_scan(*qkv(xc_c, xm_c, False), init_state=None, emit_state=True)
    xm, xc, z = feats(h, mod1)
    hs, _ = _mlstm_scan(*qkv(xc, xm, True), init_state=ctx_state, emit_state=False)
    h = _mlstm_out(h, xm, xc, z, hs, mod1, ml_w_o[0], ml_b_o[0], ml_gn_g[0], ml_skip[0],
                   ml_w_down[0], ln1_g[1], ln1_b[1], alpha)
    return _mlp_layer(h, mod1, mlp_w1[1], mlp_w2[1], ln2_g[1], ln2_b[1], alpha)
```

```python
import functools

import jax
import jax.numpy as jnp
import numpy as np
from jax import lax
from jax.experimental import pallas as pl
from jax.experimental.pallas import tpu as pltpu

F32 = jnp.float32
BF16 = jnp.bfloat16

GRID_W = 64
CONV_WIDTH = 31
QKV_CONV_WIDTH = 5
MLSTM_HEADS = 4
GATE_PACK_LANES = 8 * MLSTM_HEADS
LN_EPS = 1e-5

V7X_LANES = 128
V7X_SUBLANES = 8
V7X_BF16_ROWS = 16
V7X_MXU_DIM = 256
V7X_VMEM_LIMIT_BYTES = 56 * 1024 * 1024

WIDE_TOKEN_TILE = 512
NARROW_TOKEN_TILE = 256
SCAN_CHUNK = 256
MLP_FF_CHUNK = 1024
CONV_ROW_BLOCK = 64
MOD_ROWS = 8
NEG_BIG = -1e30


def _resident(shape):
    zeros = (0,) * len(shape)
    return pl.BlockSpec(shape, lambda *_: zeros, pipeline_mode=pl.Buffered(1))


def _params(*semantics):
    return pltpu.CompilerParams(dimension_semantics=semantics,
                                vmem_limit_bytes=V7X_VMEM_LIMIT_BYTES)


def _token_tile(s, preferred):
    t = min(s, preferred)
    assert s % t == 0
    return t


def _layer_norm(r, g, b):
    mu = jnp.mean(r, axis=-1, keepdims=True)
    d = r - mu
    var = jnp.mean(d * d, axis=-1, keepdims=True)
    return d * lax.rsqrt(var + LN_EPS) * g + b


def _silu(x):
    return x * jax.nn.sigmoid(x)


def _dot(a, b):
    return jnp.dot(a, b, preferred_element_type=F32)


def _row(v):
    return v.reshape(1, -1)


def _adaln_kernel(c_ref, w_ref, b_ref, o_ref):
    o_ref[0] = _dot(_silu(c_ref[...]), w_ref[0]) + b_ref[0]


def _adaln(cvec, ada_w, ada_b):
    depth, d, n = ada_w.shape
    tn = n // 4
    return pl.pallas_call(
        _adaln_kernel,
        grid=(depth, n // tn),
        in_specs=[pl.BlockSpec((MOD_ROWS, d), lambda i, j: (0, 0)),
                  pl.BlockSpec((1, d, tn), lambda i, j: (i, 0, j)),
                  pl.BlockSpec((1, 1, tn), lambda i, j: (i, 0, j))],
        out_specs=pl.BlockSpec((1, MOD_ROWS, tn), lambda i, j: (i, 0, j)),
        out_shape=jax.ShapeDtypeStruct((depth, MOD_ROWS, n), F32),
        compiler_params=_params("parallel", "parallel"),
        name="adaln",
    )(cvec, ada_w, ada_b.reshape(depth, 1, n))


def _dft_tables(seg, taps):
    left = (taps - 1) // 2
    nfft = 1 << (seg + taps - 2).bit_length()
    nf = nfft // 2 + 1
    nfp = -(-nf // V7X_SUBLANES) * V7X_SUBLANES
    taps_padded = -(-taps // V7X_SUBLANES) * V7X_SUBLANES
    f = np.arange(nf, dtype=np.float64)[:, None]
    ang = 2.0 * np.pi * f * np.arange(seg)[None, :] / nfft
    fwd = np.zeros((2 * nfp, seg))
    fwd[:nf], fwd[nfp:nfp + nf] = np.cos(ang), -np.sin(ang)
    weight = np.where((f == 0) | (f == nfft // 2), 1.0, 2.0) / nfft
    inv = np.zeros((seg, 2 * nfp))
    inv[:, :nf], inv[:, nfp:nfp + nf] = (weight * np.cos(ang)).T, (-weight * np.sin(ang)).T
    ang_w = 2.0 * np.pi * f * (left - np.arange(taps))[None, :] / nfft
    spec = np.zeros((2 * nfp, taps_padded))
    spec[:nf, :taps], spec[nfp:nfp + nf, :taps] = np.cos(ang_w), -np.sin(ang_w)
    group = max(1, V7X_MXU_DIM // seg)
    eye = np.eye(group)
    fwd = np.concatenate([np.kron(eye, fwd[:nfp]), np.kron(eye, fwd[nfp:])], axis=0)
    inv = np.concatenate([np.kron(eye, inv[:, :nfp]), np.kron(eye, inv[:, nfp:])], axis=1)
    return fwd.astype(np.float32), inv.astype(np.float32), spec.astype(np.float32)


def _filter_spectrum_kernel(tab_ref, w_ref, o_ref):
    o_ref[...] = jnp.dot(tab_ref[...], w_ref[...], preferred_element_type=F32,
                         precision=lax.Precision.HIGHEST)


def _filter_spectrum(spec_tab, w_dw):
    rows, taps_padded = spec_tab.shape
    c = w_dw.shape[1]
    w = jnp.pad(w_dw, ((0, taps_padded - w_dw.shape[0]), (0, 0)))
    return pl.pallas_call(
        _filter_spectrum_kernel,
        out_shape=jax.ShapeDtypeStruct((rows, c), F32),
        name="conv_filter_spectrum",
    )(jnp.asarray(spec_tab), w)


def _conv_mixer_kernel(h_ref, mod_ref, wpw1_ref, bpw1_ref, fwd_ref, inv_ref, gspec_ref, bdw_ref, cg_ref, cb_ref,
                       wpw2_ref, bpw2_ref, lng_ref, lnb_ref, o_ref, *, alpha, seg):
    c = wpw2_ref.shape[0]
    rows = fwd_ref.shape[1]
    nfp = gspec_ref.shape[0] // 2
    t = h_ref.shape[1]
    h = h_ref[0]
    sh, sc, g1 = mod_ref[0, 0:1, :], mod_ref[0, 1:2, :], mod_ref[0, 2:3, :]
    u = (h * (1.0 + sc) + sh).astype(BF16)

    def glu_slab(p):
        lo = slice(p * V7X_MXU_DIM, (p + 1) * V7X_MXU_DIM)
        hi = slice(c + p * V7X_MXU_DIM, c + (p + 1) * V7X_MXU_DIM)
        a = _dot(u, wpw1_ref[:, lo]) + bpw1_ref[:, lo]
        gate = _dot(u, wpw1_ref[:, hi]) + bpw1_ref[:, hi]
        return (a * jax.nn.sigmoid(gate)).astype(BF16)

    def conv_slab(p, glu):
        lo = slice(p * V7X_MXU_DIM, (p + 1) * V7X_MXU_DIM)
        group = rows // seg
        g_re = jnp.concatenate([gspec_ref[0:nfp, lo]] * group, axis=0)
        g_im = jnp.concatenate([gspec_ref[nfp:, lo]] * group, axis=0)
        out = []
        for r0 in range(0, t, rows):
            spec = _dot(fwd_ref[...], glu[r0:r0 + rows])
            a_re, a_im = spec[0:group * nfp], spec[group * nfp:]
            prod = jnp.concatenate([a_re * g_re - a_im * g_im, a_re * g_im + a_im * g_re], axis=0)
            out.append(_dot(inv_ref[...], prod.astype(BF16)))
        return jnp.concatenate(out, axis=0) if len(out) > 1 else out[0]

    nslab = c // V7X_MXU_DIM
    slabs = []
    glu = glu_slab(0)
    for p in range(nslab):
        nxt = glu_slab(p + 1) if p + 1 < nslab else None
        slabs.append(conv_slab(p, glu))
        glu = nxt
    conv = jnp.concatenate(slabs, axis=1) + bdw_ref[...]
    act = _silu(_layer_norm(conv, cg_ref[...], cb_ref[...])).astype(BF16)
    y = _dot(act, wpw2_ref[...]) + bpw2_ref[...]
    o_ref[0] = _layer_norm(alpha * h + g1 * y, lng_ref[...], lnb_ref[...])


def _conv_mixer_layer(h, seg, mod, p, alpha):
    bsz, s, d = h.shape
    t = _token_tile(s, WIDE_TOKEN_TILE)
    assert t % seg == 0 and seg % V7X_BF16_ROWS == 0
    c = p["w_pw2"].shape[0]
    fwd, inv, spec_tab = _dft_tables(seg, p["w_dw"].shape[0])
    gspec = _filter_spectrum(spec_tab, p["w_dw"])
    return pl.pallas_call(
        functools.partial(_conv_mixer_kernel, alpha=alpha, seg=seg),
        grid=(bsz, s // t),
        in_specs=[pl.BlockSpec((1, t, d), lambda b, j: (b, j, 0)),
                  pl.BlockSpec((1, MOD_ROWS, d), lambda b, j: (b, 0, 0)),
                  _resident((d, 2 * c)), _resident((1, 2 * c)),
                  _resident(fwd.shape), _resident(inv.shape), _resident(gspec.shape), _resident((1, c)),
                  _resident((1, c)), _resident((1, c)),
                  _resident((c, d)), _resident((1, d)), _resident((1, d)), _resident((1, d))],
        out_specs=pl.BlockSpec((1, t, d), lambda b, j: (b, j, 0)),
        out_shape=jax.ShapeDtypeStruct((bsz, s, d), F32),
        compiler_params=_params("parallel", "parallel"),
        name="conv_mixer_layer",
    )(h, mod, p["w_pw1"].astype(BF16), _row(p["b_pw1"]), jnp.asarray(fwd, BF16), jnp.asarray(inv, BF16),
      gspec, _row(p["b_dw"]), _row(p["ln_g"]), _row(p["ln_b"]), p["w_pw2"].astype(BF16),
      _row(p["b_pw2"]), _row(p["ln1_g"]), _row(p["ln1_b"]))


def _mlp_kernel(h_ref, mod_ref, w1_ref, w2_ref, g_ref, b_ref, o_ref, *, alpha):
    h = h_ref[0]
    sh, sc, g2 = mod_ref[0, 3:4, :], mod_ref[0, 4:5, :], mod_ref[0, 5:6, :]
    u = (h * (1.0 + sc) + sh).astype(BF16)
    ff = w1_ref.shape[1]
    acc = jnp.zeros(h.shape, F32)
    for f0 in range(0, ff, MLP_FF_CHUNK):
        hid = jnp.maximum(_dot(u, w1_ref[:, f0:f0 + MLP_FF_CHUNK]), 0.0)
        acc = acc + _dot((hid * hid).astype(BF16), w2_ref[f0:f0 + MLP_FF_CHUNK, :])
    o_ref[0] = _layer_norm(alpha * h + g2 * acc, g_ref[...], b_ref[...])


def _mlp_layer(h, mod, w1, w2, ln_g, ln_b, alpha):
    bsz, s, d = h.shape
    t = _token_tile(s, WIDE_TOKEN_TILE)
    ff = w1.shape[1]
    assert ff % MLP_FF_CHUNK == 0
    return pl.pallas_call(
        functools.partial(_mlp_kernel, alpha=alpha),
        grid=(bsz, s // t),
        in_specs=[pl.BlockSpec((1, t, d), lambda b, j: (b, j, 0)),
                  pl.BlockSpec((1, MOD_ROWS, d), lambda b, j: (b, 0, 0)),
                  _resident((d, ff)), _resident((ff, d)), _resident((1, d)), _resident((1, d))],
        out_specs=pl.BlockSpec((1, t, d), lambda b, j: (b, j, 0)),
        out_shape=jax.ShapeDtypeStruct((bsz, s, d), F32),
        compiler_params=_params("parallel", "parallel"),
        name="mlp_layer",
    )(h, mod, w1.astype(BF16), w2.astype(BF16), _row(ln_g), _row(ln_b))


def _mlstm_up_kernel(h_ref, prev_ref, next_ref, mod_ref, wup_ref, wconv_ref, bconv_ref,
                     xm_ref, xc_ref, z_ref, u_ref, xm_all_ref):
    t = h_ref.shape[1]
    e = xm_ref.shape[2]
    halo = prev_ref.shape[1]
    slab = 2 * V7X_MXU_DIM
    per_slab = slab // V7X_LANES
    left = (QKV_CONV_WIDTH - 1) // 2
    j = pl.program_id(1)
    sh, sc = mod_ref[0, 0:1, :], mod_ref[0, 1:2, :]
    modulate = lambda v: (v * (1.0 + sc) + sh).astype(BF16)
    u_ref[0:halo, :] = modulate(prev_ref[0])
    u_ref[halo:halo + t, :] = modulate(h_ref[0])
    u_ref[halo + t:, :] = modulate(next_ref[0])
    keep_prev = jnp.where(j == 0, 0.0, 1.0)
    keep_next = jnp.where(j == pl.num_programs(1) - 1, 0.0, 1.0)

    def xm_matmul(p):
        return _dot(u_ref[...], wup_ref[:, p * slab:(p + 1) * slab])

    def xm_store(p, xm_all):
        cols = slice(p * slab, (p + 1) * slab)
        xm_ref[0, :, cols] = xm_all[halo:halo + t].astype(BF16)
        for q in range(per_slab):
            cb = per_slab * p + q
            lanes = slice(q * V7X_LANES, (q + 1) * V7X_LANES)
            xm_all_ref[cb, 0:halo, :] = xm_all[0:halo, lanes] * keep_prev
            xm_all_ref[cb, halo:halo + t, :] = xm_all[halo:halo + t, lanes]
            xm_all_ref[cb, halo + t:, :] = xm_all[halo + t:, lanes] * keep_next

    def conv_block(cb):
        lanes = slice(cb * V7X_LANES, (cb + 1) * V7X_LANES)
        for r0 in range(0, t, CONV_ROW_BLOCK):
            acc = jnp.zeros((CONV_ROW_BLOCK, V7X_LANES), F32)
            for k in range(QKV_CONV_WIDTH):
                start = halo + r0 - left + k
                acc = acc + xm_all_ref[cb, start:start + CONV_ROW_BLOCK, :] * wconv_ref[k:k + 1, lanes]
            xc_ref[0, r0:r0 + CONV_ROW_BLOCK, lanes] = _silu(acc + bconv_ref[:, lanes]).astype(BF16)

    nslab = e // slab
    for p in range(nslab):
        xm_store(p, xm_matmul(p))
    for p in range(nslab):
        cols = slice(e + p * slab, e + (p + 1) * slab)
        z_ref[0, :, p * slab:(p + 1) * slab] = _dot(u_ref[halo:halo + t, :], wup_ref[:, cols]).astype(BF16)
        for q in range(per_slab):
            conv_block(per_slab * p + q)


def _mlstm_up(h, mod, w_up, w_conv, b_conv):
    bsz, s, d = h.shape
    e = w_up.shape[1] // 2
    t = _token_tile(s, WIDE_TOKEN_TILE)
    halo = V7X_BF16_ROWS
    assert t % halo == 0 and t % CONV_ROW_BLOCK == 0
    nhb = s // halo
    taps = w_conv.shape[0]
    w_conv = jnp.pad(w_conv, ((0, V7X_SUBLANES - taps), (0, 0)))
    tile = pl.BlockSpec((1, t, e), lambda b, j: (b, j, 0))
    out = jax.ShapeDtypeStruct((bsz, s, e), BF16)
    return pl.pallas_call(
        _mlstm_up_kernel,
        grid=(bsz, s // t),
        in_specs=[pl.BlockSpec((1, t, d), lambda b, j: (b, j, 0)),
                  pl.BlockSpec((1, halo, d), lambda b, j: (b, jnp.maximum(j * (t // halo) - 1, 0), 0)),
                  pl.BlockSpec((1, halo, d), lambda b, j: (b, jnp.minimum((j + 1) * (t // halo), nhb - 1), 0)),
                  pl.BlockSpec((1, MOD_ROWS, d), lambda b, j: (b, 0, 0)),
                  _resident((d, 2 * e)), _resident((V7X_SUBLANES, e)), _resident((1, e))],
        out_specs=[tile, tile, tile],
        out_shape=[out, out, out],
        scratch_shapes=[pltpu.VMEM((t + 2 * halo, d), BF16),
                        pltpu.VMEM((e // V7X_LANES, t + 2 * halo, V7X_LANES), F32)],
        compiler_params=_params("parallel", "parallel"),
        name="mlstm_up",
    )(h, h, h, mod, w_up.astype(BF16), w_conv, _row(b_conv))


def _log_sigmoid(x):
    return jnp.minimum(x, 0.0) - jnp.log1p(jnp.exp(-jnp.abs(x)))


def _split_bf16x3(x):
    hi = x.astype(BF16)
    r1 = x - hi.astype(F32)
    mid = r1.astype(BF16)
    lo = (r1 - mid.astype(F32)).astype(BF16)
    return jnp.concatenate([hi, mid, lo], axis=1)


def _mlstm_qkv_kernel(*refs, k_scale, emit_intra):
    refs = list(refs)
    xc_ref, xm_ref, wq_ref, wk_ref, wv_ref, wg_ref, bg_ref, q_ref, kt_ref, v_ref, g_ref = refs[:11]
    h0_ref = refs[11] if emit_intra else None
    e = wq_ref.shape[0]
    t = xc_ref.shape[1]
    nh = MLSTM_HEADS
    dh = e // nh
    half = V7X_LANES // 2
    xc = xc_ref[0]
    q = _dot(xc, wq_ref[...]).astype(BF16)
    q_ref[0] = q
    gates = _dot(q, wg_ref[0:e, :])
    k = _dot(xc, wk_ref[...])
    kt_ref[0, 0] = (k * k_scale).T.astype(BF16)
    gates = gates + _dot(k.astype(BF16), wg_ref[e:2 * e, :])
    v = _dot(xm_ref[0], wv_ref[...]).astype(BF16)
    v_ref[0] = v
    gates = gates + _dot(v, wg_ref[2 * e:, :]) + bg_ref[...]

    lane = lax.broadcasted_iota(jnp.int32, (1, V7X_LANES), 1)
    group = (lane % GATE_PACK_LANES) // nh
    rows = lax.broadcasted_iota(jnp.int32, (t, t), 0)
    cols = lax.broadcasted_iota(jnp.int32, (t, t), 1)
    visible = (rows >= cols, rows <= cols)
    lf = _log_sigmoid(gates)
    parts = _split_bf16x3(lf)
    fold = lambda c3: c3[:, :V7X_LANES] + c3[:, V7X_LANES:2 * V7X_LANES] + c3[:, 2 * V7X_LANES:]
    prefix = fold(_dot(jnp.where(visible[0], 1.0, 0.0).astype(BF16), parts))
    suffix = fold(_dot(jnp.where(visible[1], 1.0, 0.0).astype(BF16), parts))
    b = jnp.where(lane < GATE_PACK_LANES, prefix, suffix)
    a = pltpu.roll(gates, half, 1) - b
    total = jnp.sum(lf, axis=0, keepdims=True)
    amax = jnp.max(a, axis=0, keepdims=True)
    pack = jnp.where(group == 0, b, jnp.where(group == 2, a, jnp.where(group == 4, total,
                                                                         jnp.where(group == 5, amax, 0.0))))
    if emit_intra:
        a_rows = a.T
        qk_of = lambda hd: _dot(q[:, hd * dh:(hd + 1) * dh], kt_ref[0, 0, hd * dh:(hd + 1) * dh, :])
        qk_next = qk_of(0)
        for hd in range(nh):
            lanes = slice(hd * dh, (hd + 1) * dh)
            qk = qk_next
            if hd + 1 < nh:
                qk_next = qk_of(hd + 1)
            for d in range(2):
                base = d * GATE_PACK_LANES + hd
                a_row = a_rows[base + 2 * nh:base + 2 * nh + 1, :]
                masked = jnp.where(visible[d], a_row, NEG_BIG)
                cmax = jnp.max(masked, axis=1, keepdims=True)
                scores = qk * jnp.exp(masked - cmax)
                rowsum = jnp.sum(scores, axis=1, keepdims=True)
                h0_ref[d, 0, :, lanes] = _dot(scores.astype(BF16), v[:, lanes]).astype(BF16)
                pack = jnp.where(lane == base + nh, cmax, jnp.where(lane == base + 3 * nh, rowsum, pack))
    g_ref[0] = pack[:, :g_ref.shape[2]]


def _gate_lane_layout():
    nh = MLSTM_HEADS
    src = np.full((V7X_LANES,), -1, np.int32)
    for d in range(2):
        for g in range(GATE_PACK_LANES // nh):
            for h in range(nh):
                lane = d * GATE_PACK_LANES + g * nh + h
                if g in (0, 2, 4, 5):
                    src[lane] = (2 * d + 1) * nh + h
                if g in (2, 5):
                    src[V7X_LANES // 2 + lane] = 2 * d * nh + h
    return src


def _mlstm_qkv(xc, xm, w_q, w_k, w_v, w_gate, b_gate, emit_intra):
    bsz, s, e = xc.shape
    t = _token_tile(s, SCAN_CHUNK)
    k_scale = float(e // MLSTM_HEADS) ** -0.5
    src = _gate_lane_layout()
    lay = lambda w: jnp.where(src >= 0, w[:, np.maximum(src, 0)], 0.0)
    tile = pl.BlockSpec((1, t, e), lambda b, j: (b, j, 0))
    out = jax.ShapeDtypeStruct((bsz, s, e), BF16)
    out_specs = [tile, pl.BlockSpec((1, 1, e, t), lambda b, j: (b, j, 0, 0)), tile,
                 pl.BlockSpec((1, t, 2 * GATE_PACK_LANES), lambda b, j: (b, j, 0))]
    out_shape = [out, jax.ShapeDtypeStruct((bsz, s // t, e, t), BF16), out,
                 jax.ShapeDtypeStruct((bsz, s, 2 * GATE_PACK_LANES), F32)]
    if emit_intra:
        out_specs.append(pl.BlockSpec((2, 1, t, e), lambda b, j: (0, b, j, 0)))
        out_shape.append(jax.ShapeDtypeStruct((2, bsz, s, e), BF16))
    outs = pl.pallas_call(
        functools.partial(_mlstm_qkv_kernel, k_scale=k_scale, emit_intra=emit_intra),
        grid=(bsz, s // t),
        in_specs=[tile, tile, _resident((e, e)), _resident((e, e)), _resident((e, e)),
                  _resident((3 * e, V7X_LANES)), _resident((1, V7X_LANES))],
        out_specs=out_specs,
        out_shape=out_shape,
        compiler_params=_params("parallel", "parallel"),
        name="mlstm_qkv",
    )(xc, xm, w_q.astype(BF16), w_k.astype(BF16), w_v.astype(BF16), lay(w_gate).astype(BF16),
      lay(_row(b_gate)))
    return tuple(outs) if emit_intra else tuple(outs) + (None,)


def _mlstm_scan_kernel(*refs, has_init, emit_h, emit_state):
    refs = list(refs)
    q_ref, kt_ref, v_ref, g_ref, gt_ref = refs[:5]
    del refs[:5]
    if emit_h:
        h0_ref = refs.pop(0)
    if has_init:
        c0_ref, n0_ref, m0_ref = refs[:3]
        del refs[:3]
    if emit_h:
        o_ref = refs.pop(0)
    if emit_state:
        cf_ref, nf_ref, mf_ref = refs[:3]
        del refs[:3]
    c_ref, n_ref, m_ref = refs

    length = q_ref.shape[1]
    nh, dh = c_ref.shape[0], c_ref.shape[1]
    step = pl.program_id(2)

    @pl.when(step == 0)
    def _():
        if has_init:
            c_ref[...] = c0_ref[0, 0]
            n_ref[...] = n0_ref[0, 0]
            m_ref[...] = m0_ref[0, 0]
        else:
            c_ref[...] = jnp.zeros(c_ref.shape, F32)
            n_ref[...] = jnp.zeros(n_ref.shape, F32)
            m_ref[...] = jnp.zeros(m_ref.shape, F32)

    g = g_ref[0, 0]
    gt = gt_ref[0, 0]
    grp = lambda i: g[:, i * nh:(i + 1) * nh]
    grp_t = lambda i: gt[i * nh:(i + 1) * nh, :]
    m_old = m_ref[nh:nh + 1, 0:nh]
    m_old_t = m_ref[0:nh, 0:1]
    total, amax = grp(4)[0:1], grp(5)[0:1]
    m_new = total + jnp.maximum(m_old, amax)
    decay = jnp.exp(total + m_old - m_new)
    m_new_t = grp_t(4) + jnp.maximum(m_old_t, grp_t(5))
    w_k = jnp.exp(grp_t(4) + grp_t(2) - m_new_t)
    if emit_h:
        b_t, cmax, rowsum = grp(0), grp(1), grp(3)
        m_row = jnp.maximum(m_old, cmax)
        r_intra = jnp.exp(cmax - m_row)
        w_inter = jnp.exp(m_old - m_row)
        den_floor = jnp.exp(-(b_t + m_row))
    ones_rows = jnp.ones((V7X_SUBLANES, length), BF16)
    contract_lanes = (((1,), (1,)), ((), ()))

    for h in range(nh):
        lanes = slice(h * dh, (h + 1) * dh)
        col = slice(h, h + 1)
        v = v_ref[0, :, lanes]
        if emit_h:
            q = q_ref[0, :, lanes]
            q_n = lax.dot_general(q, n_ref[h].astype(BF16), contract_lanes,
                                  preferred_element_type=F32)[:, 0:1]
            num = (r_intra[:, col] * h0_ref[0, 0, :, lanes].astype(F32)
                   + w_inter[:, col] * _dot(q, c_ref[h].astype(BF16)))
            den = r_intra[:, col] * rowsum[:, col] + w_inter[:, col] * q_n
            o_ref[0, 0, :, lanes] = (num / jnp.maximum(jnp.abs(den), den_floor[:, col])).astype(o_ref.dtype)

    for h in range(nh):
        lanes = slice(h * dh, (h + 1) * dh)
        col = slice(h, h + 1)
        v = v_ref[0, :, lanes]
        wkt = (kt_ref[0, 0, lanes, :].astype(F32) * w_k[h:h + 1, :]).astype(BF16)
        c_ref[h] = decay[:, col] * c_ref[h] + _dot(wkt, v)
        n_ref[h] = decay[:, col] * n_ref[h] + lax.dot_general(ones_rows, wkt, contract_lanes,
                                                              preferred_element_type=F32)
    m_ref[0:nh, :] = m_new_t[:, 0:V7X_LANES]
    m_ref[nh:nh + 1, 0:nh] = m_new

    if emit_state:
        @pl.when(step == pl.num_programs(2) - 1)
        def _():
            cf_ref[0, 0] = c_ref[...]
            nf_ref[0, 0] = n_ref[...]
            mf_ref[0, 0] = m_ref[...]


def _mlstm_scan(q, kt, v, gates, h0, init_state, emit_state):
    bsz, s, e = q.shape
    nh = MLSTM_HEADS
    dh = e // nh
    emit_h = h0 is not None
    length = _token_tile(s, SCAN_CHUNK)
    nch = s // length
    g = gates.reshape(bsz, s, 2, GATE_PACK_LANES).transpose(0, 2, 1, 3)
    g_t = g.transpose(0, 1, 3, 2)
    chunk = lambda d, st: jnp.where(d == 0, st, nch - 1 - st)
    tile = pl.BlockSpec((1, length, e), lambda b, d, st: (b, chunk(d, st), 0))
    dir_tile = pl.BlockSpec((1, 1, length, e), lambda b, d, st: (d, b, chunk(d, st), 0))
    state_shapes = [(nh, dh, dh), (nh, V7X_SUBLANES, dh), (V7X_SUBLANES, V7X_LANES)]
    state_specs = [pl.BlockSpec((1, 1) + shp, lambda b, d, st, n=len(shp): (b, d) + (0,) * n)
                   for shp in state_shapes]
    scratch = [pltpu.VMEM(shp, F32) for shp in state_shapes]

    in_specs = [tile, pl.BlockSpec((1, 1, e, length), lambda b, d, st: (b, chunk(d, st), 0, 0)), tile,
                pl.BlockSpec((1, 1, length, GATE_PACK_LANES), lambda b, d, st: (b, d, chunk(d, st), 0)),
                pl.BlockSpec((1, 1, GATE_PACK_LANES, length), lambda b, d, st: (b, d, 0, chunk(d, st)))]
    args = [q, kt, v, g, g_t]
    if emit_h:
        in_specs.append(dir_tile)
        args.append(h0)
    if init_state is not None:
        in_specs += state_specs
        args += list(init_state)
    out_specs, out_shape = [], []
    if emit_h:
        out_specs.append(dir_tile)
        out_shape.append(jax.ShapeDtypeStruct((2, bsz, s, e), BF16))
    if emit_state:
        out_specs += state_specs
        out_shape += [jax.ShapeDtypeStruct((bsz, 2) + shp, F32) for shp in state_shapes]
    outs = pl.pallas_call(
        functools.partial(_mlstm_scan_kernel, has_init=init_state is not None, emit_h=emit_h,
                          emit_state=emit_state),
        grid=(bsz, 2, nch),
        in_specs=in_specs,
        out_specs=out_specs,
        out_shape=out_shape,
        scratch_shapes=scratch,
        compiler_params=_params("parallel", "parallel", "arbitrary"),
        name="mlstm_scan",
    )(*args)
    outs = list(outs)
    h = outs.pop(0) if emit_h else None
    return h, (tuple(outs) if emit_state else None)


def _mlstm_out_kernel(h_ref, xm_ref, xc_ref, z_ref, hs_ref, mod_ref, wo_ref, bo_ref, gn_ref, skip_ref,
                      wdown_ref, lng_ref, lnb_ref, o_ref, *, alpha):
    e = xm_ref.shape[2]
    dh = e // MLSTM_HEADS
    h = h_ref[0]
    xm = xm_ref[0]
    acc = jnp.zeros(h.shape, F32)

    def gate_logits(hd):
        lanes = slice(hd * dh, (hd + 1) * dh)
        lanes_b = slice(e + hd * dh, e + (hd + 1) * dh)
        return (_dot(xm, wo_ref[:, lanes]) + bo_ref[:, lanes],
                _dot(xm, wo_ref[:, lanes_b]) + bo_ref[:, lanes_b])

    logits = gate_logits(0)
    for hd in range(MLSTM_HEADS):
        lanes = slice(hd * dh, (hd + 1) * dh)
        o_f, o_b = jax.nn.sigmoid(logits[0]), jax.nn.sigmoid(logits[1])
        if hd + 1 < MLSTM_HEADS:
            logits = gate_logits(hd + 1)
        hsum = o_f * hs_ref[0, 0, :, lanes].astype(F32) + o_b * hs_ref[1, 0, :, lanes].astype(F32)
        mu = jnp.mean(hsum, axis=-1, keepdims=True)
        dlt = hsum - mu
        var = jnp.mean(dlt * dlt, axis=-1, keepdims=True)
        hn = dlt * lax.rsqrt(var + LN_EPS) * gn_ref[:, lanes]
        y = (hn + skip_ref[:, lanes] * xc_ref[0, :, lanes].astype(F32)) * _silu(z_ref[0, :, lanes].astype(F32))
        acc = acc + _dot(y.astype(BF16), wdown_ref[lanes, :])
    g1 = mod_ref[0, 2:3, :]
    o_ref[0] = _layer_norm(alpha * h + g1 * acc, lng_ref[...], lnb_ref[...])


def _mlstm_out(h, xm, xc, z, hs, mod, w_o, b_o, gn_g, skip, w_down, ln_g, ln_b, alpha):
    bsz, s, d = h.shape
    e = xm.shape[2]
    t = _token_tile(s, NARROW_TOKEN_TILE)
    tile = lambda width: pl.BlockSpec((1, t, width), lambda b, j: (b, j, 0))
    return pl.pallas_call(
        functools.partial(_mlstm_out_kernel, alpha=alpha),
        grid=(bsz, s // t),
        in_specs=[tile(d), tile(e), tile(e), tile(e),
                  pl.BlockSpec((2, 1, t, e), lambda b, j: (0, b, j, 0)),
                  pl.BlockSpec((1, MOD_ROWS, d), lambda b, j: (b, 0, 0)),
                  _resident((e, 2 * e)), _resident((1, 2 * e)), _resident((1, e)), _resident((1, e)),
                  _resident((e, d)), _resident((1, d)), _resident((1, d))],
        out_specs=pl.BlockSpec((1, t, d), lambda b, j: (b, j, 0)),
        out_shape=jax.ShapeDtypeStruct((bsz, s, d), F32),
        compiler_params=_params("parallel", "parallel"),
        name="mlstm_out",
    )(h, xm, xc, z, hs, mod, w_o.astype(BF16), _row(b_o), _row(gn_g), _row(skip),
      w_down.astype(BF16), _row(ln_g), _row(ln_b))


def _mod_tables(ada_out_i, bsz, d):
    m = ada_out_i.reshape(MOD_ROWS, 6, d)
    pad = ((0, 0), (0, MOD_ROWS - 6), (0, 0))
    lat = jnp.pad(m[:bsz], pad)
    ctx = jnp.pad(jnp.broadcast_to(m[bsz][None], (bsz, 6, d)), pad)
    return lat, ctx


def kernel(x, c, ctx, c_ctx, ada_w, ada_b, ln1_g, ln1_b, ln2_g, ln2_b, mlp_w1, mlp_w2, conv_w_pw1, conv_b_pw1, conv_w_dw, conv_b_dw, conv_ln_g, conv_ln_b, conv_w_pw2, conv_b_pw2, ml_w_up, ml_w_conv, ml_b_conv, ml_w_q, ml_w_k, ml_w_v, ml_w_gate, ml_b_gate, ml_w_o, ml_b_o, ml_gn_g, ml_skip, ml_w_down):
    bsz, s, d = x.shape
    depth = ada_w.shape[0]
    assert depth == 2 and bsz + 1 <= MOD_ROWS, "built for the two-layer block: conv mixer, then mLSTM"
    alpha = (2 * depth) ** 0.25

    cvec = jnp.concatenate([c, c_ctx[None], jnp.zeros((MOD_ROWS - bsz - 1, d), F32)], axis=0)
    ada = _adaln(cvec, ada_w, ada_b)
    mod0, mod0_ctx = _mod_tables(ada[0], bsz, d)
    mod1, mod1_ctx = _mod_tables(ada[1], bsz, d)

    conv_p = dict(w_pw1=conv_w_pw1[0], b_pw1=conv_b_pw1[0], w_dw=conv_w_dw[0], b_dw=conv_b_dw[0],
                  ln_g=conv_ln_g[0], ln_b=conv_ln_b[0], w_pw2=conv_w_pw2[0], b_pw2=conv_b_pw2[0],
                  ln1_g=ln1_g[0], ln1_b=ln1_b[0])
    h = _conv_mixer_layer(x, GRID_W, mod0, conv_p, alpha)
    h = _mlp_layer(h, mod0, mlp_w1[0], mlp_w2[0], ln2_g[0], ln2_b[0], alpha)
    hc = _conv_mixer_layer(ctx, ctx.shape[1], mod0_ctx, conv_p, alpha)
    hc = _mlp_layer(hc, mod0_ctx, mlp_w1[0], mlp_w2[0], ln2_g[0], ln2_b[0], alpha)

    feats = lambda hh, mm: _mlstm_up(hh, mm, ml_w_up[0], ml_w_conv[0], ml_b_conv[0])
    qkv = lambda xc_, xm_, intra: _mlstm_qkv(xc_, xm_, ml_w_q[0], ml_w_k[0], ml_w_v[0], ml_w_gate[0],
                                             ml_b_gate[0], emit_intra=intra)
    xm_c, xc_c, _ = feats(hc, mod1_ctx)
    _, ctx_state = _mlstm_scan(*qkv(xc_c, xm_c, False), init_state=None, emit_state=True)
    xm, xc, z = feats(h, mod1)
    hs, _ = _mlstm_scan(*qkv(xc, xm, True), init_state=ctx_state, emit_state=False)
    h = _mlstm_out(h, xm, xc, z, hs, mod1, ml_w_o[0], ml_b_o[0], ml_gn_g[0], ml_skip[0],
                   ml_w_down[0], ln1_g[1], ln1_b[1], alpha)
    return _mlp_layer(h, mod1, mlp_w1[1], mlp_w2[1], ln2_g[1], ln2_b[1], alpha)
```

```python
import functools

import jax
import jax.numpy as jnp
import numpy as np
from jax import lax
from jax.experimental import pallas as pl
from jax.experimental.pallas import tpu as pltpu

F32 = jnp.float32
BF16 = jnp.bfloat16

GRID_W = 64
CONV_WIDTH = 31
QKV_CONV_WIDTH = 5
MLSTM_HEADS = 4
GATE_PACK_LANES = 8 * MLSTM_HEADS
LN_EPS = 1e-5

V7X_LANES = 128
V7X_SUBLANES = 8
V7X_BF16_ROWS = 16
V7X_MXU_DIM = 256
V7X_VMEM_LIMIT_BYTES = 56 * 1024 * 1024

WIDE_TOKEN_TILE = 512
NARROW_TOKEN_TILE = 256
SCAN_CHUNK = 256
SCAN_CHUNKS_PER_STEP = 2
MLP_FF_CHUNK = 1024
CONV_ROW_BLOCK = 64
MOD_ROWS = 8
NEG_BIG = -1e30


def _resident(shape):
    zeros = (0,) * len(shape)
    return pl.BlockSpec(shape, lambda *_: zeros, pipeline_mode=pl.Buffered(1))


def _params(*semantics):
    return pltpu.CompilerParams(dimension_semantics=semantics,
                                vmem_limit_bytes=V7X_VMEM_LIMIT_BYTES)


def _token_tile(s, preferred):
    t = min(s, preferred)
    assert s % t == 0
    return t


def _layer_norm(r, g, b):
    mu = jnp.mean(r, axis=-1, keepdims=True)
    d = r - mu
    var = jnp.mean(d * d, axis=-1, keepdims=True)
    return d * lax.rsqrt(var + LN_EPS) * g + b


def _silu(x):
    return x * jax.nn.sigmoid(x)


def _dot(a, b):
    return jnp.dot(a, b, preferred_element_type=F32)


def _row(v):
    return v.reshape(1, -1)


def _adaln_kernel(c_ref, w_ref, b_ref, o_ref):
    o_ref[0] = _dot(_silu(c_ref[...]), w_ref[0]) + b_ref[0]


def _adaln(cvec, ada_w, ada_b):
    depth, d, n = ada_w.shape
    tn = n // 4
    return pl.pallas_call(
        _adaln_kernel,
        grid=(depth, n // tn),
        in_specs=[pl.BlockSpec((MOD_ROWS, d), lambda i, j: (0, 0)),
                  pl.BlockSpec((1, d, tn), lambda i, j: (i, 0, j)),
                  pl.BlockSpec((1, 1, tn), lambda i, j: (i, 0, j))],
        out_specs=pl.BlockSpec((1, MOD_ROWS, tn), lambda i, j: (i, 0, j)),
        out_shape=jax.ShapeDtypeStruct((depth, MOD_ROWS, n), F32),
        compiler_params=_params("parallel", "parallel"),
        name="adaln",
    )(cvec, ada_w, ada_b.reshape(depth, 1, n))


def _dft_tables(seg, taps):
    left = (taps - 1) // 2
    nfft = 1 << (seg + taps - 2).bit_length()
    nf = nfft // 2 + 1
    nfp = -(-nf // V7X_SUBLANES) * V7X_SUBLANES
    taps_padded = -(-taps // V7X_SUBLANES) * V7X_SUBLANES
    f = np.arange(nf, dtype=np.float64)[:, None]
    ang = 2.0 * np.pi * f * np.arange(seg)[None, :] / nfft
    fwd = np.zeros((2 * nfp, seg))
    fwd[:nf], fwd[nfp:nfp + nf] = np.cos(ang), -np.sin(ang)
    weight = np.where((f == 0) | (f == nfft // 2), 1.0, 2.0) / nfft
    inv = np.zeros((seg, 2 * nfp))
    inv[:, :nf], inv[:, nfp:nfp + nf] = (weight * np.cos(ang)).T, (-weight * np.sin(ang)).T
    ang_w = 2.0 * np.pi * f * (left - np.arange(taps))[None, :] / nfft
    spec = np.zeros((2 * nfp, taps_padded))
    spec[:nf, :taps], spec[nfp:nfp + nf, :taps] = np.cos(ang_w), -np.sin(ang_w)
    group = max(1, V7X_MXU_DIM // seg)
    eye = np.eye(group)
    fwd = np.concatenate([np.kron(eye, fwd[:nfp]), np.kron(eye, fwd[nfp:])], axis=0)
    inv = np.concatenate([np.kron(eye, inv[:, :nfp]), np.kron(eye, inv[:, nfp:])], axis=1)
    return fwd.astype(np.float32), inv.astype(np.float32), spec.astype(np.float32)


def _filter_spectrum_kernel(tab_ref, w_ref, o_ref):
    o_ref[...] = jnp.dot(tab_ref[...], w_ref[...], preferred_element_type=F32,
                         precision=lax.Precision.HIGHEST)


def _filter_spectrum(spec_tab, w_dw):
    rows, taps_padded = spec_tab.shape
    c = w_dw.shape[1]
    w = jnp.pad(w_dw, ((0, taps_padded - w_dw.shape[0]), (0, 0)))
    return pl.pallas_call(
        _filter_spectrum_kernel,
        out_shape=jax.ShapeDtypeStruct((rows, c), F32),
        name="conv_filter_spectrum",
    )(jnp.asarray(spec_tab), w)


def _conv_mixer_kernel(h_ref, mod_ref, wpw1_ref, bpw1_ref, fwd_ref, inv_ref, gspec_ref, bdw_ref, cg_ref, cb_ref,
                       wpw2_ref, bpw2_ref, lng_ref, lnb_ref, o_ref, *, alpha, seg):
    c = wpw2_ref.shape[0]
    rows = fwd_ref.shape[1]
    nfp = gspec_ref.shape[0] // 2
    t = h_ref.shape[1]
    h = h_ref[0]
    sh, sc, g1 = mod_ref[0, 0:1, :], mod_ref[0, 1:2, :], mod_ref[0, 2:3, :]
    u = (h * (1.0 + sc) + sh).astype(BF16)

    def glu_slab(p):
        lo = slice(p * V7X_MXU_DIM, (p + 1) * V7X_MXU_DIM)
        hi = slice(c + p * V7X_MXU_DIM, c + (p + 1) * V7X_MXU_DIM)
        a = _dot(u, wpw1_ref[:, lo]) + bpw1_ref[:, lo]
        gate = _dot(u, wpw1_ref[:, hi]) + bpw1_ref[:, hi]
        return (a * jax.nn.sigmoid(gate)).astype(BF16)

    def conv_slab(p, glu):
        lo = slice(p * V7X_MXU_DIM, (p + 1) * V7X_MXU_DIM)
        group = rows // seg
        g_re = jnp.concatenate([gspec_ref[0:nfp, lo]] * group, axis=0)
        g_im = jnp.concatenate([gspec_ref[nfp:, lo]] * group, axis=0)
        out = []
        for r0 in range(0, t, rows):
            spec = _dot(fwd_ref[...], glu[r0:r0 + rows])
            a_re, a_im = spec[0:group * nfp], spec[group * nfp:]
            prod = jnp.concatenate([a_re * g_re - a_im * g_im, a_re * g_im + a_im * g_re], axis=0)
            out.append(_dot(inv_ref[...], prod.astype(BF16)))
        return jnp.concatenate(out, axis=0) if len(out) > 1 else out[0]

    nslab = c // V7X_MXU_DIM
    slabs = []
    glu = glu_slab(0)
    for p in range(nslab):
        nxt = glu_slab(p + 1) if p + 1 < nslab else None
        slabs.append(conv_slab(p, glu))
        glu = nxt
    conv = jnp.concatenate(slabs, axis=1) + bdw_ref[...]
    act = _silu(_layer_norm(conv, cg_ref[...], cb_ref[...])).astype(BF16)
    y = _dot(act, wpw2_ref[...]) + bpw2_ref[...]
    o_ref[0] = _layer_norm(alpha * h + g1 * y, lng_ref[...], lnb_ref[...])


def _conv_mixer_layer(h, seg, mod, p, alpha):
    bsz, s, d = h.shape
    t = _token_tile(s, WIDE_TOKEN_TILE)
    assert t % seg == 0 and seg % V7X_BF16_ROWS == 0
    c = p["w_pw2"].shape[0]
    fwd, inv, spec_tab = _dft_tables(seg, p["w_dw"].shape[0])
    gspec = _filter_spectrum(spec_tab, p["w_dw"])
    return pl.pallas_call(
        functools.partial(_conv_mixer_kernel, alpha=alpha, seg=seg),
        grid=(bsz, s // t),
        in_specs=[pl.BlockSpec((1, t, d), lambda b, j: (b, j, 0)),
                  pl.BlockSpec((1, MOD_ROWS, d), lambda b, j: (b, 0, 0)),
                  _resident((d, 2 * c)), _resident((1, 2 * c)),
                  _resident(fwd.shape), _resident(inv.shape), _resident(gspec.shape), _resident((1, c)),
                  _resident((1, c)), _resident((1, c)),
                  _resident((c, d)), _resident((1, d)), _resident((1, d)), _resident((1, d))],
        out_specs=pl.BlockSpec((1, t, d), lambda b, j: (b, j, 0)),
        out_shape=jax.ShapeDtypeStruct((bsz, s, d), F32),
        compiler_params=_params("parallel", "parallel"),
        name="conv_mixer_layer",
    )(h, mod, p["w_pw1"].astype(BF16), _row(p["b_pw1"]), jnp.asarray(fwd, BF16), jnp.asarray(inv, BF16),
      gspec, _row(p["b_dw"]), _row(p["ln_g"]), _row(p["ln_b"]), p["w_pw2"].astype(BF16),
      _row(p["b_pw2"]), _row(p["ln1_g"]), _row(p["ln1_b"]))


def _mlp_kernel(h_ref, mod_ref, w1_ref, w2_ref, g_ref, b_ref, o_ref, *, alpha):
    h = h_ref[0]
    sh, sc, g2 = mod_ref[0, 3:4, :], mod_ref[0, 4:5, :], mod_ref[0, 5:6, :]
    u = (h * (1.0 + sc) + sh).astype(BF16)
    ff = w1_ref.shape[1]
    acc = jnp.zeros(h.shape, F32)
    for f0 in range(0, ff, MLP_FF_CHUNK):
        hid = jnp.maximum(_dot(u, w1_ref[:, f0:f0 + MLP_FF_CHUNK]), 0.0)
        acc = acc + _dot((hid * hid).astype(BF16), w2_ref[f0:f0 + MLP_FF_CHUNK, :])
    o_ref[0] = _layer_norm(alpha * h + g2 * acc, g_ref[...], b_ref[...])


def _mlp_layer(h, mod, w1, w2, ln_g, ln_b, alpha):
    bsz, s, d = h.shape
    t = _token_tile(s, WIDE_TOKEN_TILE)
    ff = w1.shape[1]
    assert ff % MLP_FF_CHUNK == 0
    return pl.pallas_call(
        functools.partial(_mlp_kernel, alpha=alpha),
        grid=(bsz, s // t),
        in_specs=[pl.BlockSpec((1, t, d), lambda b, j: (b, j, 0)),
                  pl.BlockSpec((1, MOD_ROWS, d), lambda b, j: (b, 0, 0)),
                  _resident((d, ff)), _resident((ff, d)), _resident((1, d)), _resident((1, d))],
        out_specs=pl.BlockSpec((1, t, d), lambda b, j: (b, j, 0)),
        out_shape=jax.ShapeDtypeStruct((bsz, s, d), F32),
        compiler_params=_params("parallel", "parallel"),
        name="mlp_layer",
    )(h, mod, w1.astype(BF16), w2.astype(BF16), _row(ln_g), _row(ln_b))


def _mlstm_up_kernel(h_ref, prev_ref, next_ref, mod_ref, wup_ref, wconv_ref, bconv_ref,
                     xm_ref, xc_ref, z_ref, u_ref, xm_all_ref):
    t = h_ref.shape[1]
    e = xm_ref.shape[2]
    halo = prev_ref.shape[1]
    slab = 2 * V7X_MXU_DIM
    per_slab = slab // V7X_LANES
    left = (QKV_CONV_WIDTH - 1) // 2
    j = pl.program_id(1)
    sh, sc = mod_ref[0, 0:1, :], mod_ref[0, 1:2, :]
    modulate = lambda v: (v * (1.0 + sc) + sh).astype(BF16)
    u_ref[0:halo, :] = modulate(prev_ref[0])
    u_ref[halo:halo + t, :] = modulate(h_ref[0])
    u_ref[halo + t:, :] = modulate(next_ref[0])
    keep_prev = jnp.where(j == 0, 0.0, 1.0)
    keep_next = jnp.where(j == pl.num_programs(1) - 1, 0.0, 1.0)

    def xm_matmul(p):
        return _dot(u_ref[...], wup_ref[:, p * slab:(p + 1) * slab])

    def xm_store(p, xm_all):
        cols = slice(p * slab, (p + 1) * slab)
        xm_ref[0, :, cols] = xm_all[halo:halo + t].astype(BF16)
        for q in range(per_slab):
            cb = per_slab * p + q
            lanes = slice(q * V7X_LANES, (q + 1) * V7X_LANES)
            xm_all_ref[cb, 0:halo, :] = xm_all[0:halo, lanes] * keep_prev
            xm_all_ref[cb, halo:halo + t, :] = xm_all[halo:halo + t, lanes]
            xm_all_ref[cb, halo + t:, :] = xm_all[halo + t:, lanes] * keep_next

    def conv_block(cb):
        lanes = slice(cb * V7X_LANES, (cb + 1) * V7X_LANES)
        for r0 in range(0, t, CONV_ROW_BLOCK):
            acc = jnp.zeros((CONV_ROW_BLOCK, V7X_LANES), F32)
            for k in range(QKV_CONV_WIDTH):
                start = halo + r0 - left + k
                acc = acc + xm_all_ref[cb, start:start + CONV_ROW_BLOCK, :] * wconv_ref[k:k + 1, lanes]
            xc_ref[0, r0:r0 + CONV_ROW_BLOCK, lanes] = _silu(acc + bconv_ref[:, lanes]).astype(BF16)

    nslab = e // slab
    for p in range(nslab):
        xm_store(p, xm_matmul(p))
    for p in range(nslab):
        cols = slice(e + p * slab, e + (p + 1) * slab)
        z_ref[0, :, p * slab:(p + 1) * slab] = _dot(u_ref[halo:halo + t, :], wup_ref[:, cols]).astype(BF16)
        for q in range(per_slab):
            conv_block(per_slab * p + q)


def _mlstm_up(h, mod, w_up, w_conv, b_conv):
    bsz, s, d = h.shape
    e = w_up.shape[1] // 2
    t = _token_tile(s, WIDE_TOKEN_TILE)
    halo = V7X_BF16_ROWS
    assert t % halo == 0 and t % CONV_ROW_BLOCK == 0
    nhb = s // halo
    taps = w_conv.shape[0]
    w_conv = jnp.pad(w_conv, ((0, V7X_SUBLANES - taps), (0, 0)))
    tile = pl.BlockSpec((1, t, e), lambda b, j: (b, j, 0))
    out = jax.ShapeDtypeStruct((bsz, s, e), BF16)
    return pl.pallas_call(
        _mlstm_up_kernel,
        grid=(bsz, s // t),
        in_specs=[pl.BlockSpec((1, t, d), lambda b, j: (b, j, 0)),
                  pl.BlockSpec((1, halo, d), lambda b, j: (b, jnp.maximum(j * (t // halo) - 1, 0), 0)),
                  pl.BlockSpec((1, halo, d), lambda b, j: (b, jnp.minimum((j + 1) * (t // halo), nhb - 1), 0)),
                  pl.BlockSpec((1, MOD_ROWS, d), lambda b, j: (b, 0, 0)),
                  _resident((d, 2 * e)), _resident((V7X_SUBLANES, e)), _resident((1, e))],
        out_specs=[tile, tile, tile],
        out_shape=[out, out, out],
        scratch_shapes=[pltpu.VMEM((t + 2 * halo, d), BF16),
                        pltpu.VMEM((e // V7X_LANES, t + 2 * halo, V7X_LANES), F32)],
        compiler_params=_params("parallel", "parallel"),
        name="mlstm_up",
    )(h, h, h, mod, w_up.astype(BF16), w_conv, _row(b_conv))


def _log_sigmoid(x):
    return jnp.minimum(x, 0.0) - jnp.log1p(jnp.exp(-jnp.abs(x)))


def _split_bf16x3(x):
    hi = x.astype(BF16)
    r1 = x - hi.astype(F32)
    mid = r1.astype(BF16)
    lo = (r1 - mid.astype(F32)).astype(BF16)
    return jnp.concatenate([hi, mid, lo], axis=1)


def _mlstm_qkv_kernel(*refs, k_scale, emit_intra):
    refs = list(refs)
    xc_ref, xm_ref, wq_ref, wk_ref, wv_ref, wg_ref, bg_ref, q_ref, kt_ref, v_ref, g_ref = refs[:11]
    h0_ref = refs[11] if emit_intra else None
    e = wq_ref.shape[0]
    t = xc_ref.shape[1]
    nh = MLSTM_HEADS
    dh = e // nh
    half = V7X_LANES // 2
    xc = xc_ref[0]
    q = _dot(xc, wq_ref[...]).astype(BF16)
    q_ref[0] = q
    gates = _dot(q, wg_ref[0:e, :])
    k = _dot(xc, wk_ref[...])
    kt_ref[0, 0] = (k * k_scale).T.astype(BF16)
    gates = gates + _dot(k.astype(BF16), wg_ref[e:2 * e, :])
    v = _dot(xm_ref[0], wv_ref[...]).astype(BF16)
    v_ref[0] = v
    gates = gates + _dot(v, wg_ref[2 * e:, :]) + bg_ref[...]

    lane = lax.broadcasted_iota(jnp.int32, (1, V7X_LANES), 1)
    group = (lane % GATE_PACK_LANES) // nh
    rows = lax.broadcasted_iota(jnp.int32, (t, t), 0)
    cols = lax.broadcasted_iota(jnp.int32, (t, t), 1)
    visible = (rows >= cols, rows <= cols)
    lf = _log_sigmoid(gates)
    parts = _split_bf16x3(lf)
    fold = lambda c3: c3[:, :V7X_LANES] + c3[:, V7X_LANES:2 * V7X_LANES] + c3[:, 2 * V7X_LANES:]
    prefix = fold(_dot(jnp.where(visible[0], 1.0, 0.0).astype(BF16), parts))
    suffix = fold(_dot(jnp.where(visible[1], 1.0, 0.0).astype(BF16), parts))
    b = jnp.where(lane < GATE_PACK_LANES, prefix, suffix)
    a = pltpu.roll(gates, half, 1) - b
    total = jnp.sum(lf, axis=0, keepdims=True)
    amax = jnp.max(a, axis=0, keepdims=True)
    pack = jnp.where(group == 0, b, jnp.where(group == 2, a, jnp.where(group == 4, total,
                                                                         jnp.where(group == 5, amax, 0.0))))
    if emit_intra:
        a_rows = a.T
        qk_of = lambda hd: _dot(q[:, hd * dh:(hd + 1) * dh], kt_ref[0, 0, hd * dh:(hd + 1) * dh, :])
        qk_next = qk_of(0)
        for hd in range(nh):
            lanes = slice(hd * dh, (hd + 1) * dh)
            qk = qk_next
            if hd + 1 < nh:
                qk_next = qk_of(hd + 1)
            for d in range(2):
                base = d * GATE_PACK_LANES + hd
                a_row = a_rows[base + 2 * nh:base + 2 * nh + 1, :]
                masked = jnp.where(visible[d], a_row, NEG_BIG)
                cmax = jnp.max(masked, axis=1, keepdims=True)
                scores = qk * jnp.exp(masked - cmax)
                rowsum = jnp.sum(scores, axis=1, keepdims=True)
                h0_ref[d, 0, :, lanes] = _dot(scores.astype(BF16), v[:, lanes]).astype(BF16)
                pack = jnp.where(lane == base + nh, cmax, jnp.where(lane == base + 3 * nh, rowsum, pack))
    g_ref[0] = pack[:, :g_ref.shape[2]]


def _gate_lane_layout():
    nh = MLSTM_HEADS
    src = np.full((V7X_LANES,), -1, np.int32)
    for d in range(2):
        for g in range(GATE_PACK_LANES // nh):
            for h in range(nh):
                lane = d * GATE_PACK_LANES + g * nh + h
                if g in (0, 2, 4, 5):
                    src[lane] = (2 * d + 1) * nh + h
                if g in (2, 5):
                    src[V7X_LANES // 2 + lane] = 2 * d * nh + h
    return src


def _mlstm_qkv(xc, xm, w_q, w_k, w_v, w_gate, b_gate, emit_intra):
    bsz, s, e = xc.shape
    t = _token_tile(s, SCAN_CHUNK)
    k_scale = float(e // MLSTM_HEADS) ** -0.5
    src = _gate_lane_layout()
    lay = lambda w: jnp.where(src >= 0, w[:, np.maximum(src, 0)], 0.0)
    tile = pl.BlockSpec((1, t, e), lambda b, j: (b, j, 0))
    out = jax.ShapeDtypeStruct((bsz, s, e), BF16)
    out_specs = [tile, pl.BlockSpec((1, 1, e, t), lambda b, j: (b, j, 0, 0)), tile,
                 pl.BlockSpec((1, t, 2 * GATE_PACK_LANES), lambda b, j: (b, j, 0))]
    out_shape = [out, jax.ShapeDtypeStruct((bsz, s // t, e, t), BF16), out,
                 jax.ShapeDtypeStruct((bsz, s, 2 * GATE_PACK_LANES), F32)]
    if emit_intra:
        out_specs.append(pl.BlockSpec((2, 1, t, e), lambda b, j: (0, b, j, 0)))
        out_shape.append(jax.ShapeDtypeStruct((2, bsz, s, e), BF16))
    outs = pl.pallas_call(
        functools.partial(_mlstm_qkv_kernel, k_scale=k_scale, emit_intra=emit_intra),
        grid=(bsz, s // t),
        in_specs=[tile, tile, _resident((e, e)), _resident((e, e)), _resident((e, e)),
                  _resident((3 * e, V7X_LANES)), _resident((1, V7X_LANES))],
        out_specs=out_specs,
        out_shape=out_shape,
        compiler_params=_params("parallel", "parallel"),
        name="mlstm_qkv",
    )(xc, xm, w_q.astype(BF16), w_k.astype(BF16), w_v.astype(BF16), lay(w_gate).astype(BF16),
      lay(_row(b_gate)))
    return tuple(outs) if emit_intra else tuple(outs) + (None,)


def _mlstm_scan_kernel(*refs, has_init, emit_h, emit_state):
    refs = list(refs)
    q_ref, kt_ref, v_ref, g_ref, gt_ref = refs[:5]
    del refs[:5]
    if emit_h:
        h0_ref = refs.pop(0)
    if has_init:
        c0_ref, n0_ref, m0_ref = refs[:3]
        del refs[:3]
    if emit_h:
        o_ref = refs.pop(0)
    if emit_state:
        cf_ref, nf_ref, mf_ref = refs[:3]
        del refs[:3]
    c_ref, n_ref, m_ref = refs

    length = kt_ref.shape[3]
    chunks = kt_ref.shape[1]
    nh, dh = c_ref.shape[0], c_ref.shape[1]
    backward = pl.program_id(1) == 1
    step = pl.program_id(2)

    @pl.when(step == 0)
    def _():
        if has_init:
            c_ref[...] = c0_ref[0, 0]
            n_ref[...] = n0_ref[0, 0]
            m_ref[...] = m0_ref[0, 0]
        else:
            c_ref[...] = jnp.zeros(c_ref.shape, F32)
            n_ref[...] = jnp.zeros(n_ref.shape, F32)
            m_ref[...] = jnp.zeros(m_ref.shape, F32)

    ones_rows = jnp.ones((V7X_SUBLANES, length), BF16)
    contract_lanes = (((1,), (1,)), ((), ()))

    def one_chunk(i, carry):
        sub = jnp.where(backward, chunks - 1 - i, i)
        rows = pl.ds(pl.multiple_of(sub * length, length), length)
        g = g_ref[0, 0, rows, :]
        gt = gt_ref[0, 0, sub]
        grp = lambda i: g[:, i * nh:(i + 1) * nh]
        grp_t = lambda i: gt[i * nh:(i + 1) * nh, :]
        m_old = m_ref[nh:nh + 1, 0:nh]
        m_old_t = m_ref[0:nh, 0:1]
        total, amax = grp(4)[0:1], grp(5)[0:1]
        m_new = total + jnp.maximum(m_old, amax)
        decay = jnp.exp(total + m_old - m_new)
        m_new_t = grp_t(4) + jnp.maximum(m_old_t, grp_t(5))
        w_k = jnp.exp(grp_t(4) + grp_t(2) - m_new_t)
        if emit_h:
            b_t, cmax, rowsum = grp(0), grp(1), grp(3)
            m_row = jnp.maximum(m_old, cmax)
            r_intra = jnp.exp(cmax - m_row)
            w_inter = jnp.exp(m_old - m_row)
            den_floor = jnp.exp(-(b_t + m_row))

            for h in range(nh):
                lanes = slice(h * dh, (h + 1) * dh)
                col = slice(h, h + 1)
                q = q_ref[0, rows, lanes]
                q_n = lax.dot_general(q, n_ref[h].astype(BF16), contract_lanes,
                                      preferred_element_type=F32)[:, 0:1]
                num = (r_intra[:, col] * h0_ref[0, 0, rows, lanes].astype(F32)
                       + w_inter[:, col] * _dot(q, c_ref[h].astype(BF16)))
                den = r_intra[:, col] * rowsum[:, col] + w_inter[:, col] * q_n
                o_ref[0, 0, rows, lanes] = (num / jnp.maximum(jnp.abs(den), den_floor[:, col])).astype(o_ref.dtype)

        for h in range(nh):
            lanes = slice(h * dh, (h + 1) * dh)
            col = slice(h, h + 1)
            wkt = (kt_ref[0, sub, lanes, :].astype(F32) * w_k[h:h + 1, :]).astype(BF16)
            c_ref[h] = decay[:, col] * c_ref[h] + _dot(wkt, v_ref[0, rows, lanes])
            n_ref[h] = decay[:, col] * n_ref[h] + lax.dot_general(ones_rows, wkt, contract_lanes,
                                                                  preferred_element_type=F32)
        m_ref[0:nh, :] = m_new_t[:, 0:V7X_LANES]
        m_ref[nh:nh + 1, 0:nh] = m_new
        return carry

    lax.fori_loop(0, chunks, one_chunk, 0)

    if emit_state:
        @pl.when(step == pl.num_programs(2) - 1)
        def _():
            cf_ref[0, 0] = c_ref[...]
            nf_ref[0, 0] = n_ref[...]
            mf_ref[0, 0] = m_ref[...]


def _mlstm_scan(q, kt, v, gates, h0, init_state, emit_state):
    bsz, s, e = q.shape
    nh = MLSTM_HEADS
    dh = e // nh
    emit_h = h0 is not None
    length = _token_tile(s, SCAN_CHUNK)
    nch = s // length
    per_step = SCAN_CHUNKS_PER_STEP if nch % SCAN_CHUNKS_PER_STEP == 0 else 1
    nst = nch // per_step
    span = per_step * length
    g = gates.reshape(bsz, s, 2, GATE_PACK_LANES).transpose(0, 2, 1, 3)
    g_t = g.reshape(bsz, 2, nch, length, GATE_PACK_LANES).transpose(0, 1, 2, 4, 3)
    chunk = lambda d, st: jnp.where(d == 0, st, nst - 1 - st)
    tile = pl.BlockSpec((1, span, e), lambda b, d, st: (b, chunk(d, st), 0))
    dir_tile = pl.BlockSpec((1, 1, span, e), lambda b, d, st: (d, b, chunk(d, st), 0))
    state_shapes = [(nh, dh, dh), (nh, V7X_SUBLANES, dh), (V7X_SUBLANES, V7X_LANES)]
    state_specs = [pl.BlockSpec((1, 1) + shp, lambda b, d, st, n=len(shp): (b, d) + (0,) * n)
                   for shp in state_shapes]
    scratch = [pltpu.VMEM(shp, F32) for shp in state_shapes]

    in_specs = [tile, pl.BlockSpec((1, per_step, e, length), lambda b, d, st: (b, chunk(d, st), 0, 0)), tile,
                pl.BlockSpec((1, 1, span, GATE_PACK_LANES), lambda b, d, st: (b, d, chunk(d, st), 0)),
                pl.BlockSpec((1, 1, per_step, GATE_PACK_LANES, length),
                             lambda b, d, st: (b, d, chunk(d, st), 0, 0))]
    args = [q, kt, v, g, g_t]
    if emit_h:
        in_specs.append(dir_tile)
        args.append(h0)
    if init_state is not None:
        in_specs += state_specs
        args += list(init_state)
    out_specs, out_shape = [], []
    if emit_h:
        out_specs.append(dir_tile)
        out_shape.append(jax.ShapeDtypeStruct((2, bsz, s, e), BF16))
    if emit_state:
        out_specs += state_specs
        out_shape += [jax.ShapeDtypeStruct((bsz, 2) + shp, F32) for shp in state_shapes]
    outs = pl.pallas_call(
        functools.partial(_mlstm_scan_kernel, has_init=init_state is not None, emit_h=emit_h,
                          emit_state=emit_state),
        grid=(bsz, 2, nst),
        in_specs=in_specs,
        out_specs=out_specs,
        out_shape=out_shape,
        scratch_shapes=scratch,
        compiler_params=_params("parallel", "parallel", "arbitrary"),
        name="mlstm_scan",
    )(*args)
    outs = list(outs)
    h = outs.pop(0) if emit_h else None
    return h, (tuple(outs) if emit_state else None)


def _mlstm_out_kernel(h_ref, xm_ref, xc_ref, z_ref, hs_ref, mod_ref, wo_ref, bo_ref, gn_ref, skip_ref,
                      wdown_ref, lng_ref, lnb_ref, o_ref, *, alpha):
    e = xm_ref.shape[2]
    dh = e // MLSTM_HEADS
    h = h_ref[0]
    xm = xm_ref[0]
    acc = jnp.zeros(h.shape, F32)

    def gate_logits(hd):
        lanes = slice(hd * dh, (hd + 1) * dh)
        lanes_b = slice(e + hd * dh, e + (hd + 1) * dh)
        return (_dot(xm, wo_ref[:, lanes]) + bo_ref[:, lanes],
                _dot(xm, wo_ref[:, lanes_b]) + bo_ref[:, lanes_b])

    logits = gate_logits(0)
    for hd in range(MLSTM_HEADS):
        lanes = slice(hd * dh, (hd + 1) * dh)
        o_f, o_b = jax.nn.sigmoid(logits[0]), jax.nn.sigmoid(logits[1])
        if hd + 1 < MLSTM_HEADS:
            logits = gate_logits(hd + 1)
        hsum = o_f * hs_ref[0, 0, :, lanes].astype(F32) + o_b * hs_ref[1, 0, :, lanes].astype(F32)
        mu = jnp.mean(hsum, axis=-1, keepdims=True)
        dlt = hsum - mu
        var = jnp.mean(dlt * dlt, axis=-1, keepdims=True)
        hn = dlt * lax.rsqrt(var + LN_EPS) * gn_ref[:, lanes]
        y = (hn + skip_ref[:, lanes] * xc_ref[0, :, lanes].astype(F32)) * _silu(z_ref[0, :, lanes].astype(F32))
        acc = acc + _dot(y.astype(BF16), wdown_ref[lanes, :])
    g1 = mod_ref[0, 2:3, :]
    o_ref[0] = _layer_norm(alpha * h + g1 * acc, lng_ref[...], lnb_ref[...])


def _mlstm_out(h, xm, xc, z, hs, mod, w_o, b_o, gn_g, skip, w_down, ln_g, ln_b, alpha):
    bsz, s, d = h.shape
    e = xm.shape[2]
    t = _token_tile(s, NARROW_TOKEN_TILE)
    tile = lambda width: pl.BlockSpec((1, t, width), lambda b, j: (b, j, 0))
    return pl.pallas_call(
        functools.partial(_mlstm_out_kernel, alpha=alpha),
        grid=(bsz, s // t),
        in_specs=[tile(d), tile(e), tile(e), tile(e),
                  pl.BlockSpec((2, 1, t, e), lambda b, j: (0, b, j, 0)),
                  pl.BlockSpec((1, MOD_ROWS, d), lambda b, j: (b, 0, 0)),
                  _resident((e, 2 * e)), _resident((1, 2 * e)), _resident((1, e)), _resident((1, e)),
                  _resident((e, d)), _resident((1, d)), _resident((1, d))],
        out_specs=pl.BlockSpec((1, t, d), lambda b, j: (b, j, 0)),
        out_shape=jax.ShapeDtypeStruct((bsz, s, d), F32),
        compiler_params=_params("parallel", "parallel"),
        name="mlstm_out",
    )(h, xm, xc, z, hs, mod, w_o.astype(BF16), _row(b_o), _row(gn_g), _row(skip),
      w_down.astype(BF16), _row(ln_g), _row(ln_b))


def _mod_tables(ada_out_i, bsz, d):
    m = ada_out_i.reshape(MOD_ROWS, 6, d)
    pad = ((0, 0), (0, MOD_ROWS - 6), (0, 0))
    lat = jnp.pad(m[:bsz], pad)
    ctx = jnp.pad(jnp.broadcast_to(m[bsz][None], (bsz, 6, d)), pad)
    return lat, ctx


def kernel(x, c, ctx, c_ctx, ada_w, ada_b, ln1_g, ln1_b, ln2_g, ln2_b, mlp_w1, mlp_w2, conv_w_pw1, conv_b_pw1, conv_w_dw, conv_b_dw, conv_ln_g, conv_ln_b, conv_w_pw2, conv_b_pw2, ml_w_up, ml_w_conv, ml_b_conv, ml_w_q, ml_w_k, ml_w_v, ml_w_gate, ml_b_gate, ml_w_o, ml_b_o, ml_gn_g, ml_skip, ml_w_down):
    bsz, s, d = x.shape
    depth = ada_w.shape[0]
    assert depth == 2 and bsz + 1 <= MOD_ROWS, "built for the two-layer block: conv mixer, then mLSTM"
    alpha = (2 * depth) ** 0.25

    cvec = jnp.concatenate([c, c_ctx[None], jnp.zeros((MOD_ROWS - bsz - 1, d), F32)], axis=0)
    ada = _adaln(cvec, ada_w, ada_b)
    mod0, mod0_ctx = _mod_tables(ada[0], bsz, d)
    mod1, mod1_ctx = _mod_tables(ada[1], bsz, d)

    conv_p = dict(w_pw1=conv_w_pw1[0], b_pw1=conv_b_pw1[0], w_dw=conv_w_dw[0], b_dw=conv_b_dw[0],
                  ln_g=conv_ln_g[0], ln_b=conv_ln_b[0], w_pw2=conv_w_pw2[0], b_pw2=conv_b_pw2[0],
                  ln1_g=ln1_g[0], ln1_b=ln1_b[0])
    h = _conv_mixer_layer(x, GRID_W, mod0, conv_p, alpha)
    h = _mlp_layer(h, mod0, mlp_w1[0], mlp_w2[0], ln2_g[0], ln2_b[0], alpha)
    hc = _conv_mixer_layer(ctx, ctx.shape[1], mod0_ctx, conv_p, alpha)
    hc = _mlp_layer(hc, mod0_ctx, mlp_w1[0], mlp_w2[0], ln2_g[0], ln2_b[0], alpha)

    feats = lambda hh, mm: _mlstm_up(hh, mm, ml_w_up[0], ml_w_conv[0], ml_b_conv[0])
    qkv = lambda xc_, xm_, intra: _mlstm_qkv(xc_, xm_, ml_w_q[0], ml_w_k[0], ml_w_v[0], ml_w_gate[0],
                                             ml_b_gate[0], emit_intra=intra)
    xm_c, xc_c, _ = feats(hc, mod1_ctx)
    _, ctx_state = _mlstm_scan(*qkv(xc_c, xm_c, False), init_state=None, emit_state=True)
    xm, xc, z = feats(h, mod1)
    hs, _ = _mlstm_scan(*qkv(xc, xm, True), init_state=ctx_state, emit_state=False)
    h = _mlstm_out(h, xm, xc, z, hs, mod1, ml_w_o[0], ml_b_o[0], ml_gn_g[0], ml_skip[0],
                   ml_w_down[0], ln1_g[1], ln1_b[1], alpha)
    return _mlp_layer(h, mod1, mlp_w1[1], mlp_w2[1], ln2_g[1], ln2_b[1], alpha)
```

```python
import functools

import jax
import jax.numpy as jnp
import numpy as np
from jax import lax
from jax.experimental import pallas as pl
from jax.experimental.pallas import tpu as pltpu

F32 = jnp.float32
BF16 = jnp.bfloat16

GRID_W = 64
CONV_WIDTH = 31
QKV_CONV_WIDTH = 5
MLSTM_HEADS = 4
GATE_PACK_LANES = 8 * MLSTM_HEADS
LN_EPS = 1e-5

V7X_LANES = 128
V7X_SUBLANES = 8
V7X_BF16_ROWS = 16
V7X_MXU_DIM = 256
V7X_VMEM_LIMIT_BYTES = 56 * 1024 * 1024

WIDE_TOKEN_TILE = 512
NARROW_TOKEN_TILE = 256
SCAN_CHUNK = 256
SCAN_CHUNKS_PER_STEP = 2
MLP_FF_CHUNK = 1024
CONV_ROW_BLOCK = 64
MOD_ROWS = 8
NEG_BIG = -1e30


def _resident(shape):
    zeros = (0,) * len(shape)
    return pl.BlockSpec(shape, lambda *_: zeros, pipeline_mode=pl.Buffered(1))


def _params(*semantics):
    return pltpu.CompilerParams(dimension_semantics=semantics,
                                vmem_limit_bytes=V7X_VMEM_LIMIT_BYTES)


def _token_tile(s, preferred):
    t = min(s, preferred)
    assert s % t == 0
    return t


def _layer_norm(r, g, b):
    mu = jnp.mean(r, axis=-1, keepdims=True)
    d = r - mu
    var = jnp.mean(d * d, axis=-1, keepdims=True)
    return d * lax.rsqrt(var + LN_EPS) * g + b


def _silu(x):
    return x * jax.nn.sigmoid(x)


def _dot(a, b):
    return jnp.dot(a, b, preferred_element_type=F32)


def _row(v):
    return v.reshape(1, -1)


def _adaln_kernel(c_ref, w_ref, b_ref, o_ref):
    o_ref[0] = _dot(_silu(c_ref[...]), w_ref[0]) + b_ref[0]


def _adaln(cvec, ada_w, ada_b):
    depth, d, n = ada_w.shape
    tn = n // 4
    return pl.pallas_call(
        _adaln_kernel,
        grid=(depth, n // tn),
        in_specs=[pl.BlockSpec((MOD_ROWS, d), lambda i, j: (0, 0)),
                  pl.BlockSpec((1, d, tn), lambda i, j: (i, 0, j)),
                  pl.BlockSpec((1, 1, tn), lambda i, j: (i, 0, j))],
        out_specs=pl.BlockSpec((1, MOD_ROWS, tn), lambda i, j: (i, 0, j)),
        out_shape=jax.ShapeDtypeStruct((depth, MOD_ROWS, n), F32),
        compiler_params=_params("parallel", "parallel"),
        name="adaln",
    )(cvec, ada_w, ada_b.reshape(depth, 1, n))


def _dft_tables(seg, taps):
    left = (taps - 1) // 2
    nfft = 1 << (seg + taps - 2).bit_length()
    nf = nfft // 2 + 1
    nfp = -(-nf // V7X_SUBLANES) * V7X_SUBLANES
    taps_padded = -(-taps // V7X_SUBLANES) * V7X_SUBLANES
    f = np.arange(nf, dtype=np.float64)[:, None]
    ang = 2.0 * np.pi * f * np.arange(seg)[None, :] / nfft
    fwd = np.zeros((2 * nfp, seg))
    fwd[:nf], fwd[nfp:nfp + nf] = np.cos(ang), -np.sin(ang)
    weight = np.where((f == 0) | (f == nfft // 2), 1.0, 2.0) / nfft
    inv = np.zeros((seg, 2 * nfp))
    inv[:, :nf], inv[:, nfp:nfp + nf] = (weight * np.cos(ang)).T, (-weight * np.sin(ang)).T
    ang_w = 2.0 * np.pi * f * (left - np.arange(taps))[None, :] / nfft
    spec = np.zeros((2 * nfp, taps_padded))
    spec[:nf, :taps], spec[nfp:nfp + nf, :taps] = np.cos(ang_w), -np.sin(ang_w)
    group = max(1, V7X_MXU_DIM // seg)
    eye = np.eye(group)
    fwd = np.concatenate([np.kron(eye, fwd[:nfp]), np.kron(eye, fwd[nfp:])], axis=0)
    inv = np.concatenate([np.kron(eye, inv[:, :nfp]), np.kron(eye, inv[:, nfp:])], axis=1)
    return fwd.astype(np.float32), inv.astype(np.float32), spec.astype(np.float32)


def _filter_spectrum_kernel(tab_ref, w_ref, o_ref):
    o_ref[...] = jnp.dot(tab_ref[...], w_ref[...], preferred_element_type=F32,
                         precision=lax.Precision.HIGHEST)


def _filter_spectrum(spec_tab, w_dw):
    rows, taps_padded = spec_tab.shape
    c = w_dw.shape[1]
    w = jnp.pad(w_dw, ((0, taps_padded - w_dw.shape[0]), (0, 0)))
    return pl.pallas_call(
        _filter_spectrum_kernel,
        out_shape=jax.ShapeDtypeStruct((rows, c), F32),
        name="conv_filter_spectrum",
    )(jnp.asarray(spec_tab), w)


def _conv_mixer_kernel(h_ref, mod_ref, wpw1_ref, bpw1_ref, fwd_ref, inv_ref, gspec_ref, bdw_ref, cg_ref, cb_ref,
                       wpw2_ref, bpw2_ref, lng_ref, lnb_ref, o_ref, *, alpha, seg):
    c = wpw2_ref.shape[0]
    rows = fwd_ref.shape[1]
    nfp = gspec_ref.shape[0] // 2
    t = h_ref.shape[1]
    h = h_ref[0]
    sh, sc, g1 = mod_ref[0, 0:1, :], mod_ref[0, 1:2, :], mod_ref[0, 2:3, :]
    u = (h * (1.0 + sc) + sh).astype(BF16)

    def glu_slab(p):
        lo = slice(p * V7X_MXU_DIM, (p + 1) * V7X_MXU_DIM)
        hi = slice(c + p * V7X_MXU_DIM, c + (p + 1) * V7X_MXU_DIM)
        a = _dot(u, wpw1_ref[:, lo]) + bpw1_ref[:, lo]
        gate = _dot(u, wpw1_ref[:, hi]) + bpw1_ref[:, hi]
        return (a * jax.nn.sigmoid(gate)).astype(BF16)

    def conv_slab(p, glu):
        lo = slice(p * V7X_MXU_DIM, (p + 1) * V7X_MXU_DIM)
        group = rows // seg
        g_re = jnp.concatenate([gspec_ref[0:nfp, lo]] * group, axis=0)
        g_im = jnp.concatenate([gspec_ref[nfp:, lo]] * group, axis=0)
        out = []
        for r0 in range(0, t, rows):
            spec = _dot(fwd_ref[...], glu[r0:r0 + rows])
            a_re, a_im = spec[0:group * nfp], spec[group * nfp:]
            prod = jnp.concatenate([a_re * g_re - a_im * g_im, a_re * g_im + a_im * g_re], axis=0)
            out.append(_dot(inv_ref[...], prod.astype(BF16)))
        return jnp.concatenate(out, axis=0) if len(out) > 1 else out[0]

    nslab = c // V7X_MXU_DIM
    slabs = []
    glu = glu_slab(0)
    for p in range(nslab):
        nxt = glu_slab(p + 1) if p + 1 < nslab else None
        slabs.append(conv_slab(p, glu))
        glu = nxt
    conv = jnp.concatenate(slabs, axis=1) + bdw_ref[...]
    act = _silu(_layer_norm(conv, cg_ref[...], cb_ref[...])).astype(BF16)
    y = _dot(act, wpw2_ref[...]) + bpw2_ref[...]
    o_ref[0] = _layer_norm(alpha * h + g1 * y, lng_ref[...], lnb_ref[...])


def _conv_mixer_layer(h, seg, mod, p, alpha):
    bsz, s, d = h.shape
    t = _token_tile(s, WIDE_TOKEN_TILE)
    assert t % seg == 0 and seg % V7X_BF16_ROWS == 0
    c = p["w_pw2"].shape[0]
    fwd, inv, spec_tab = _dft_tables(seg, p["w_dw"].shape[0])
    gspec = _filter_spectrum(spec_tab, p["w_dw"])
    return pl.pallas_call(
        functools.partial(_conv_mixer_kernel, alpha=alpha, seg=seg),
        grid=(bsz, s // t),
        in_specs=[pl.BlockSpec((1, t, d), lambda b, j: (b, j, 0)),
                  pl.BlockSpec((1, MOD_ROWS, d), lambda b, j: (b, 0, 0)),
                  _resident((d, 2 * c)), _resident((1, 2 * c)),
                  _resident(fwd.shape), _resident(inv.shape), _resident(gspec.shape), _resident((1, c)),
                  _resident((1, c)), _resident((1, c)),
                  _resident((c, d)), _resident((1, d)), _resident((1, d)), _resident((1, d))],
        out_specs=pl.BlockSpec((1, t, d), lambda b, j: (b, j, 0)),
        out_shape=jax.ShapeDtypeStruct((bsz, s, d), F32),
        compiler_params=_params("parallel", "parallel"),
        name="conv_mixer_layer",
    )(h, mod, p["w_pw1"].astype(BF16), _row(p["b_pw1"]), jnp.asarray(fwd, BF16), jnp.asarray(inv, BF16),
      gspec, _row(p["b_dw"]), _row(p["ln_g"]), _row(p["ln_b"]), p["w_pw2"].astype(BF16),
      _row(p["b_pw2"]), _row(p["ln1_g"]), _row(p["ln1_b"]))


def _mlp_kernel(h_ref, mod_ref, w1_ref, w2_ref, g_ref, b_ref, o_ref, *, alpha):
    h = h_ref[0]
    sh, sc, g2 = mod_ref[0, 3:4, :], mod_ref[0, 4:5, :], mod_ref[0, 5:6, :]
    u = (h * (1.0 + sc) + sh).astype(BF16)
    ff = w1_ref.shape[1]
    acc = jnp.zeros(h.shape, F32)
    for f0 in range(0, ff, MLP_FF_CHUNK):
        hid = jnp.maximum(_dot(u, w1_ref[:, f0:f0 + MLP_FF_CHUNK]), 0.0)
        acc = acc + _dot((hid * hid).astype(BF16), w2_ref[f0:f0 + MLP_FF_CHUNK, :])
    o_ref[0] = _layer_norm(alpha * h + g2 * acc, g_ref[...], b_ref[...])


def _mlp_layer(h, mod, w1, w2, ln_g, ln_b, alpha):
    bsz, s, d = h.shape
    t = _token_tile(s, WIDE_TOKEN_TILE)
    ff = w1.shape[1]
    assert ff % MLP_FF_CHUNK == 0
    return pl.pallas_call(
        functools.partial(_mlp_kernel, alpha=alpha),
        grid=(bsz, s // t),
        in_specs=[pl.BlockSpec((1, t, d), lambda b, j: (b, j, 0)),
                  pl.BlockSpec((1, MOD_ROWS, d), lambda b, j: (b, 0, 0)),
                  _resident((d, ff)), _resident((ff, d)), _resident((1, d)), _resident((1, d))],
        out_specs=pl.BlockSpec((1, t, d), lambda b, j: (b, j, 0)),
        out_shape=jax.ShapeDtypeStruct((bsz, s, d), F32),
        compiler_params=_params("parallel", "parallel"),
        name="mlp_layer",
    )(h, mod, w1.astype(BF16), w2.astype(BF16), _row(ln_g), _row(ln_b))


def _mlstm_up_kernel(h_ref, prev_ref, next_ref, mod_ref, wup_ref, wconv_ref, bconv_ref,
                     xm_ref, xc_ref, z_ref, u_ref, xm_all_ref):
    t = h_ref.shape[1]
    e = xm_ref.shape[2]
    halo = prev_ref.shape[1]
    slab = 2 * V7X_MXU_DIM
    per_slab = slab // V7X_LANES
    left = (QKV_CONV_WIDTH - 1) // 2
    j = pl.program_id(1)
    sh, sc = mod_ref[0, 0:1, :], mod_ref[0, 1:2, :]
    modulate = lambda v: (v * (1.0 + sc) + sh).astype(BF16)
    u_ref[0:halo, :] = modulate(prev_ref[0])
    u_ref[halo:halo + t, :] = modulate(h_ref[0])
    u_ref[halo + t:, :] = modulate(next_ref[0])
    keep_prev = jnp.where(j == 0, 0.0, 1.0)
    keep_next = jnp.where(j == pl.num_programs(1) - 1, 0.0, 1.0)

    def xm_matmul(p):
        return _dot(u_ref[...], wup_ref[:, p * slab:(p + 1) * slab])

    def xm_store(p, xm_all):
        cols = slice(p * slab, (p + 1) * slab)
        xm_ref[0, :, cols] = xm_all[halo:halo + t].astype(BF16)
        for q in range(per_slab):
            cb = per_slab * p + q
            lanes = slice(q * V7X_LANES, (q + 1) * V7X_LANES)
            xm_all_ref[cb, 0:halo, :] = xm_all[0:halo, lanes] * keep_prev
            xm_all_ref[cb, halo:halo + t, :] = xm_all[halo:halo + t, lanes]
            xm_all_ref[cb, halo + t:, :] = xm_all[halo + t:, lanes] * keep_next

    def conv_block(cb):
        lanes = slice(cb * V7X_LANES, (cb + 1) * V7X_LANES)
        for r0 in range(0, t, CONV_ROW_BLOCK):
            acc = jnp.zeros((CONV_ROW_BLOCK, V7X_LANES), F32)
            for k in range(QKV_CONV_WIDTH):
                start = halo + r0 - left + k
                acc = acc + xm_all_ref[cb, start:start + CONV_ROW_BLOCK, :] * wconv_ref[k:k + 1, lanes]
            xc_ref[0, r0:r0 + CONV_ROW_BLOCK, lanes] = _silu(acc + bconv_ref[:, lanes]).astype(BF16)

    nslab = e // slab
    for p in range(nslab):
        xm_store(p, xm_matmul(p))
    for p in range(nslab):
        cols = slice(e + p * slab, e + (p + 1) * slab)
        z_ref[0, :, p * slab:(p + 1) * slab] = _dot(u_ref[halo:halo + t, :], wup_ref[:, cols]).astype(BF16)
        for q in range(per_slab):
            conv_block(per_slab * p + q)


def _mlstm_up(h, mod, w_up, w_conv, b_conv):
    bsz, s, d = h.shape
    e = w_up.shape[1] // 2
    t = _token_tile(s, WIDE_TOKEN_TILE)
    halo = V7X_BF16_ROWS
    assert t % halo == 0 and t % CONV_ROW_BLOCK == 0
    nhb = s // halo
    taps = w_conv.shape[0]
    w_conv = jnp.pad(w_conv, ((0, V7X_SUBLANES - taps), (0, 0)))
    tile = pl.BlockSpec((1, t, e), lambda b, j: (b, j, 0))
    out = jax.ShapeDtypeStruct((bsz, s, e), BF16)
    return pl.pallas_call(
        _mlstm_up_kernel,
        grid=(bsz, s // t),
        in_specs=[pl.BlockSpec((1, t, d), lambda b, j: (b, j, 0)),
                  pl.BlockSpec((1, halo, d), lambda b, j: (b, jnp.maximum(j * (t // halo) - 1, 0), 0)),
                  pl.BlockSpec((1, halo, d), lambda b, j: (b, jnp.minimum((j + 1) * (t // halo), nhb - 1), 0)),
                  pl.BlockSpec((1, MOD_ROWS, d), lambda b, j: (b, 0, 0)),
                  _resident((d, 2 * e)), _resident((V7X_SUBLANES, e)), _resident((1, e))],
        out_specs=[tile, tile, tile],
        out_shape=[out, out, out],
        scratch_shapes=[pltpu.VMEM((t + 2 * halo, d), BF16),
                        pltpu.VMEM((e // V7X_LANES, t + 2 * halo, V7X_LANES), F32)],
        compiler_params=_params("parallel", "parallel"),
        name="mlstm_up",
    )(h, h, h, mod, w_up.astype(BF16), w_conv, _row(b_conv))


def _log_sigmoid(x):
    return jnp.minimum(x, 0.0) - jnp.log1p(jnp.exp(-jnp.abs(x)))


def _split_bf16x3(x):
    hi = x.astype(BF16)
    r1 = x - hi.astype(F32)
    mid = r1.astype(BF16)
    lo = (r1 - mid.astype(F32)).astype(BF16)
    return jnp.concatenate([hi, mid, lo], axis=1)


def _mlstm_qkv_kernel(*refs, k_scale, emit_intra):
    refs = list(refs)
    xc_ref, xm_ref, wq_ref, wk_ref, wv_ref, wg_ref, bg_ref, q_ref, kt_ref, v_ref, g_ref = refs[:11]
    h0_ref = refs[11] if emit_intra else None
    e = wq_ref.shape[0]
    t = xc_ref.shape[1]
    nh = MLSTM_HEADS
    dh = e // nh
    half = V7X_LANES // 2
    xc = xc_ref[0]
    q = _dot(xc, wq_ref[...]).astype(BF16)
    q_ref[0] = q
    gates = _dot(q, wg_ref[0:e, :])
    k = _dot(xc, wk_ref[...])
    kt_ref[0, 0] = (k * k_scale).T.astype(BF16)
    gates = gates + _dot(k.astype(BF16), wg_ref[e:2 * e, :])
    v = _dot(xm_ref[0], wv_ref[...]).astype(BF16)
    v_ref[0] = v
    gates = gates + _dot(v, wg_ref[2 * e:, :]) + bg_ref[...]

    lane = lax.broadcasted_iota(jnp.int32, (1, V7X_LANES), 1)
    group = (lane % GATE_PACK_LANES) // nh
    rows = lax.broadcasted_iota(jnp.int32, (t, t), 0)
    cols = lax.broadcasted_iota(jnp.int32, (t, t), 1)
    visible = (rows >= cols, rows <= cols)
    lf = _log_sigmoid(gates)
    parts = _split_bf16x3(lf)
    fold = lambda c3: c3[:, :V7X_LANES] + c3[:, V7X_LANES:2 * V7X_LANES] + c3[:, 2 * V7X_LANES:]
    prefix = fold(_dot(jnp.where(visible[0], 1.0, 0.0).astype(BF16), parts))
    suffix = fold(_dot(jnp.where(visible[1], 1.0, 0.0).astype(BF16), parts))
    b = jnp.where(lane < GATE_PACK_LANES, prefix, suffix)
    a = pltpu.roll(gates, half, 1) - b
    total = jnp.sum(lf, axis=0, keepdims=True)
    amax = jnp.max(a, axis=0, keepdims=True)
    pack = jnp.where(group == 0, b, jnp.where(group == 2, a, jnp.where(group == 4, total,
                                                                         jnp.where(group == 5, amax, 0.0))))
    if emit_intra:
        a_rows = a.T
        qk_of = lambda hd: _dot(q[:, hd * dh:(hd + 1) * dh], kt_ref[0, 0, hd * dh:(hd + 1) * dh, :])
        qk_next = qk_of(0)
        for hd in range(nh):
            lanes = slice(hd * dh, (hd + 1) * dh)
            qk = qk_next
            if hd + 1 < nh:
                qk_next = qk_of(hd + 1)
            for d in range(2):
                base = d * GATE_PACK_LANES + hd
                a_row = a_rows[base + 2 * nh:base + 2 * nh + 1, :]
                masked = jnp.where(visible[d], a_row, NEG_BIG)
                cmax = jnp.max(masked, axis=1, keepdims=True)
                scores = qk * jnp.exp(masked - cmax)
                rowsum = jnp.sum(scores, axis=1, keepdims=True)
                h0_ref[d, 0, :, lanes] = _dot(scores.astype(BF16), v[:, lanes]).astype(BF16)
                pack = jnp.where(lane == base + nh, cmax, jnp.where(lane == base + 3 * nh, rowsum, pack))
    g_ref[0] = pack[:, :g_ref.shape[2]]


def _gate_lane_layout():
    nh = MLSTM_HEADS
    src = np.full((V7X_LANES,), -1, np.int32)
    for d in range(2):
        for g in range(GATE_PACK_LANES // nh):
            for h in range(nh):
                lane = d * GATE_PACK_LANES + g * nh + h
                if g in (0, 2, 4, 5):
                    src[lane] = (2 * d + 1) * nh + h
                if g in (2, 5):
                    src[V7X_LANES // 2 + lane] = 2 * d * nh + h
    return src


def _mlstm_qkv(xc, xm, w_q, w_k, w_v, w_gate, b_gate, emit_intra):
    bsz, s, e = xc.shape
    t = _token_tile(s, SCAN_CHUNK)
    k_scale = float(e // MLSTM_HEADS) ** -0.5
    src = _gate_lane_layout()
    lay = lambda w: jnp.where(src >= 0, w[:, np.maximum(src, 0)], 0.0)
    tile = pl.BlockSpec((1, t, e), lambda b, j: (b, j, 0))
    out = jax.ShapeDtypeStruct((bsz, s, e), BF16)
    out_specs = [tile, pl.BlockSpec((1, 1, e, t), lambda b, j: (b, j, 0, 0)), tile,
                 pl.BlockSpec((1, t, 2 * GATE_PACK_LANES), lambda b, j: (b, j, 0))]
    out_shape = [out, jax.ShapeDtypeStruct((bsz, s // t, e, t), BF16), out,
                 jax.ShapeDtypeStruct((bsz, s, 2 * GATE_PACK_LANES), F32)]
    if emit_intra:
        out_specs.append(pl.BlockSpec((2, 1, t, e), lambda b, j: (0, b, j, 0)))
        out_shape.append(jax.ShapeDtypeStruct((2, bsz, s, e), BF16))
    outs = pl.pallas_call(
        functools.partial(_mlstm_qkv_kernel, k_scale=k_scale, emit_intra=emit_intra),
        grid=(bsz, s // t),
        in_specs=[tile, tile, _resident((e, e)), _resident((e, e)), _resident((e, e)),
                  _resident((3 * e, V7X_LANES)), _resident((1, V7X_LANES))],
        out_specs=out_specs,
        out_shape=out_shape,
        compiler_params=_params("parallel", "parallel"),
        name="mlstm_qkv",
    )(xc, xm, w_q.astype(BF16), w_k.astype(BF16), w_v.astype(BF16), lay(w_gate).astype(BF16),
      lay(_row(b_gate)))
    return tuple(outs) if emit_intra else tuple(outs) + (None,)


def _mlstm_scan_kernel(*refs, has_init, emit_h, emit_state):
    refs = list(refs)
    q_ref, kt_ref, v_ref, g_ref, gt_ref = refs[:5]
    del refs[:5]
    if emit_h:
        h0_ref = refs.pop(0)
    if has_init:
        c0_ref, n0_ref, m0_ref = refs[:3]
        del refs[:3]
    if emit_h:
        o_ref = refs.pop(0)
    if emit_state:
        cf_ref, nf_ref, mf_ref = refs[:3]
        del refs[:3]
    c_ref, n_ref, m_ref = refs

    length = kt_ref.shape[3]
    chunks = kt_ref.shape[1]
    nh, dh = c_ref.shape[0], c_ref.shape[1]
    backward = pl.program_id(1) == 1
    step = pl.program_id(2)

    @pl.when(step == 0)
    def _():
        if has_init:
            c_ref[...] = c0_ref[0, 0]
            n_ref[...] = n0_ref[0, 0]
            m_ref[...] = m0_ref[0, 0]
        else:
            c_ref[...] = jnp.zeros(c_ref.shape, F32)
            n_ref[...] = jnp.zeros(n_ref.shape, F32)
            m_ref[...] = jnp.zeros(m_ref.shape, F32)

    ones_rows = jnp.ones((V7X_SUBLANES, length), BF16)
    contract_lanes = (((1,), (1,)), ((), ()))

    def one_chunk(i, carry):
        sub = jnp.where(backward, chunks - 1 - i, i)
        rows = pl.ds(pl.multiple_of(sub * length, length), length)
        g = g_ref[0, 0, rows, :]
        gt = gt_ref[0, 0, sub]
        grp = lambda i: g[:, i * nh:(i + 1) * nh]
        grp_t = lambda i: gt[i * nh:(i + 1) * nh, :]
        m_old = m_ref[nh:nh + 1, 0:nh]
        m_old_t = m_ref[0:nh, 0:1]
        total, amax = grp(4)[0:1], grp(5)[0:1]
        m_new = total + jnp.maximum(m_old, amax)
        decay = jnp.exp(total + m_old - m_new)
        m_new_t = grp_t(4) + jnp.maximum(m_old_t, grp_t(5))
        w_k = jnp.exp(grp_t(4) + grp_t(2) - m_new_t)
        if emit_h:
            b_t, cmax, rowsum = grp(0), grp(1), grp(3)
            m_row = jnp.maximum(m_old, cmax)
            r_intra = jnp.exp(cmax - m_row)
            w_inter = jnp.exp(m_old - m_row)
            den_floor = jnp.exp(-(b_t + m_row))

            for h in range(nh):
                lanes = slice(h * dh, (h + 1) * dh)
                col = slice(h, h + 1)
                q = q_ref[0, rows, lanes]
                q_n = lax.dot_general(q, n_ref[h].astype(BF16), contract_lanes,
                                      preferred_element_type=F32)[:, 0:1]
                num = (r_intra[:, col] * h0_ref[0, 0, rows, lanes].astype(F32)
                       + w_inter[:, col] * _dot(q, c_ref[h].astype(BF16)))
                den = r_intra[:, col] * rowsum[:, col] + w_inter[:, col] * q_n
                o_ref[0, 0, rows, lanes] = (num / jnp.maximum(jnp.abs(den), den_floor[:, col])).astype(o_ref.dtype)

        for h in range(nh):
            lanes = slice(h * dh, (h + 1) * dh)
            col = slice(h, h + 1)
            wkt = (kt_ref[0, sub, lanes, :].astype(F32) * w_k[h:h + 1, :]).astype(BF16)
            c_ref[h] = decay[:, col] * c_ref[h] + _dot(wkt, v_ref[0, rows, lanes])
            n_ref[h] = decay[:, col] * n_ref[h] + lax.dot_general(ones_rows, wkt, contract_lanes,
                                                                  preferred_element_type=F32)
        m_ref[0:nh, :] = m_new_t[:, 0:V7X_LANES]
        m_ref[nh:nh + 1, 0:nh] = m_new
        return carry

    lax.fori_loop(0, chunks, one_chunk, 0)

    if emit_state:
        @pl.when(step == pl.num_programs(2) - 1)
        def _():
            cf_ref[0, 0] = c_ref[...]
            nf_ref[0, 0] = n_ref[...]
            mf_ref[0, 0] = m_ref[...]


def _mlstm_scan(q, kt, v, gates, h0, init_state, emit_state):
    bsz, s, e = q.shape
    nh = MLSTM_HEADS
    dh = e // nh
    emit_h = h0 is not None
    length = _token_tile(s, SCAN_CHUNK)
    nch = s // length
    per_step = SCAN_CHUNKS_PER_STEP if nch % SCAN_CHUNKS_PER_STEP == 0 else 1
    nst = nch // per_step
    span = per_step * length
    g = gates.reshape(bsz, s, 2, GATE_PACK_LANES).transpose(0, 2, 1, 3)
    g_t = g.reshape(bsz, 2, nch, length, GATE_PACK_LANES).transpose(0, 1, 2, 4, 3)
    chunk = lambda d, st: jnp.where(d == 0, st, nst - 1 - st)
    tile = pl.BlockSpec((1, span, e), lambda b, d, st: (b, chunk(d, st), 0))
    dir_tile = pl.BlockSpec((1, 1, span, e), lambda b, d, st: (d, b, chunk(d, st), 0))
    state_shapes = [(nh, dh, dh), (nh, V7X_SUBLANES, dh), (V7X_SUBLANES, V7X_LANES)]
    state_specs = [pl.BlockSpec((1, 1) + shp, lambda b, d, st, n=len(shp): (b, d) + (0,) * n)
                   for shp in state_shapes]
    scratch = [pltpu.VMEM(shp, F32) for shp in state_shapes]

    in_specs = [tile, pl.BlockSpec((1, per_step, e, length), lambda b, d, st: (b, chunk(d, st), 0, 0)), tile,
                pl.BlockSpec((1, 1, span, GATE_PACK_LANES), lambda b, d, st: (b, d, chunk(d, st), 0)),
                pl.BlockSpec((1, 1, per_step, GATE_PACK_LANES, length),
                             lambda b, d, st: (b, d, chunk(d, st), 0, 0))]
    args = [q, kt, v, g, g_t]
    if emit_h:
        in_specs.append(dir_tile)
        args.append(h0)
    if init_state is not None:
        in_specs += state_specs
        args += list(init_state)
    out_specs, out_shape = [], []
    if emit_h:
        out_specs.append(dir_tile)
        out_shape.append(jax.ShapeDtypeStruct((2, bsz, s, e), BF16))
    if emit_state:
        out_specs += state_specs
        out_shape += [jax.ShapeDtypeStruct((bsz, 2) + shp, F32) for shp in state_shapes]
    outs = pl.pallas_call(
        functools.partial(_mlstm_scan_kernel, has_init=init_state is not None, emit_h=emit_h,
                          emit_state=emit_state),
        grid=(bsz, 2, nst),
        in_specs=in_specs,
        out_specs=out_specs,
        out_shape=out_shape,
        scratch_shapes=scratch,
        compiler_params=_params("parallel", "parallel", "arbitrary"),
        name="mlstm_scan",
    )(*args)
    outs = list(outs)
    h = outs.pop(0) if emit_h else None
    return h, (tuple(outs) if emit_state else None)


def _mlstm_out_kernel(h_ref, xm_ref, xc_ref, z_ref, hs_ref, mod_ref, wo_ref, bo_ref, gn_ref, skip_ref,
                      wdown_ref, lng_ref, lnb_ref, o_ref, *, alpha):
    e = xm_ref.shape[2]
    dh = e // MLSTM_HEADS
    sub_rows = NARROW_TOKEN_TILE
    g1 = mod_ref[0, 2:3, :]

    def sub_tile(i, carry):
        rows = pl.ds(pl.multiple_of(i * sub_rows, sub_rows), sub_rows)
        h = h_ref[0, rows, :]
        xm = xm_ref[0, rows, :]
        acc = jnp.zeros(h.shape, F32)

        def gate_logits(hd):
            lanes = slice(hd * dh, (hd + 1) * dh)
            lanes_b = slice(e + hd * dh, e + (hd + 1) * dh)
            return (_dot(xm, wo_ref[:, lanes]) + bo_ref[:, lanes],
                    _dot(xm, wo_ref[:, lanes_b]) + bo_ref[:, lanes_b])

        logits = gate_logits(0)
        for hd in range(MLSTM_HEADS):
            lanes = slice(hd * dh, (hd + 1) * dh)
            o_f, o_b = jax.nn.sigmoid(logits[0]), jax.nn.sigmoid(logits[1])
            if hd + 1 < MLSTM_HEADS:
                logits = gate_logits(hd + 1)
            hsum = (o_f * hs_ref[0, 0, rows, lanes].astype(F32)
                    + o_b * hs_ref[1, 0, rows, lanes].astype(F32))
            mu = jnp.mean(hsum, axis=-1, keepdims=True)
            dlt = hsum - mu
            var = jnp.mean(dlt * dlt, axis=-1, keepdims=True)
            hn = dlt * lax.rsqrt(var + LN_EPS) * gn_ref[:, lanes]
            y = ((hn + skip_ref[:, lanes] * xc_ref[0, rows, lanes].astype(F32))
                 * _silu(z_ref[0, rows, lanes].astype(F32)))
            acc = acc + _dot(y.astype(BF16), wdown_ref[lanes, :])
        o_ref[0, rows, :] = _layer_norm(alpha * h + g1 * acc, lng_ref[...], lnb_ref[...])
        return carry

    lax.fori_loop(0, h_ref.shape[1] // sub_rows, sub_tile, 0)


def _mlstm_out(h, xm, xc, z, hs, mod, w_o, b_o, gn_g, skip, w_down, ln_g, ln_b, alpha):
    bsz, s, d = h.shape
    e = xm.shape[2]
    t = _token_tile(s, WIDE_TOKEN_TILE)
    assert t % NARROW_TOKEN_TILE == 0
    tile = lambda width: pl.BlockSpec((1, t, width), lambda b, j: (b, j, 0))
    return pl.pallas_call(
        functools.partial(_mlstm_out_kernel, alpha=alpha),
        grid=(bsz, s // t),
        in_specs=[tile(d), tile(e), tile(e), tile(e),
                  pl.BlockSpec((2, 1, t, e), lambda b, j: (0, b, j, 0)),
                  pl.BlockSpec((1, MOD_ROWS, d), lambda b, j: (b, 0, 0)),
                  _resident((e, 2 * e)), _resident((1, 2 * e)), _resident((1, e)), _resident((1, e)),
                  _resident((e, d)), _resident((1, d)), _resident((1, d))],
        out_specs=pl.BlockSpec((1, t, d), lambda b, j: (b, j, 0)),
        out_shape=jax.ShapeDtypeStruct((bsz, s, d), F32),
        compiler_params=_params("parallel", "parallel"),
        name="mlstm_out",
    )(h, xm, xc, z, hs, mod, w_o.astype(BF16), _row(b_o), _row(gn_g), _row(skip),
      w_down.astype(BF16), _row(ln_g), _row(ln_b))


def _mod_tables(ada_out_i, bsz, d):
    m = ada_out_i.reshape(MOD_ROWS, 6, d)
    pad = ((0, 0), (0, MOD_ROWS - 6), (0, 0))
    lat = jnp.pad(m[:bsz], pad)
    ctx = jnp.pad(jnp.broadcast_to(m[bsz][None], (bsz, 6, d)), pad)
    return lat, ctx


def kernel(x, c, ctx, c_ctx, ada_w, ada_b, ln1_g, ln1_b, ln2_g, ln2_b, mlp_w1, mlp_w2, conv_w_pw1, conv_b_pw1, conv_w_dw, conv_b_dw, conv_ln_g, conv_ln_b, conv_w_pw2, conv_b_pw2, ml_w_up, ml_w_conv, ml_b_conv, ml_w_q, ml_w_k, ml_w_v, ml_w_gate, ml_b_gate, ml_w_o, ml_b_o, ml_gn_g, ml_skip, ml_w_down):
    bsz, s, d = x.shape
    depth = ada_w.shape[0]
    assert depth == 2 and bsz + 1 <= MOD_ROWS, "built for the two-layer block: conv mixer, then mLSTM"
    alpha = (2 * depth) ** 0.25

    cvec = jnp.concatenate([c, c_ctx[None], jnp.zeros((MOD_ROWS - bsz - 1, d), F32)], axis=0)
    ada = _adaln(cvec, ada_w, ada_b)
    mod0, mod0_ctx = _mod_tables(ada[0], bsz, d)
    mod1, mod1_ctx = _mod_tables(ada[1], bsz, d)

    conv_p = dict(w_pw1=conv_w_pw1[0], b_pw1=conv_b_pw1[0], w_dw=conv_w_dw[0], b_dw=conv_b_dw[0],
                  ln_g=conv_ln_g[0], ln_b=conv_ln_b[0], w_pw2=conv_w_pw2[0], b_pw2=conv_b_pw2[0],
                  ln1_g=ln1_g[0], ln1_b=ln1_b[0])
    h = _conv_mixer_layer(x, GRID_W, mod0, conv_p, alpha)
    h = _mlp_layer(h, mod0, mlp_w1[0], mlp_w2[0], ln2_g[0], ln2_b[0], alpha)
    hc = _conv_mixer_layer(ctx, ctx.shape[1], mod0_ctx, conv_p, alpha)
    hc = _mlp_layer(hc, mod0_ctx, mlp_w1[0], mlp_w2[0], ln2_g[0], ln2_b[0], alpha)

    feats = lambda hh, mm: _mlstm_up(hh, mm, ml_w_up[0], ml_w_conv[0], ml_b_conv[0])
    qkv = lambda xc_, xm_, intra: _mlstm_qkv(xc_, xm_, ml_w_q[0], ml_w_k[0], ml_w_v[0], ml_w_gate[0],
                                             ml_b_gate[0], emit_intra=intra)
    xm_c, xc_c, _ = feats(hc, mod1_ctx)
    _, ctx_state = _mlstm_scan(*qkv(xc_c, xm_c, False), init_state=None, emit_state=True)
    xm, xc, z = feats(h, mod1)
    hs, _ = _mlstm_scan(*qkv(xc, xm, True), init_state=ctx_state, emit_state=False)
    h = _mlstm_out(h, xm, xc, z, hs, mod1, ml_w_o[0], ml_b_o[0], ml_gn_g[0], ml_skip[0],
                   ml_w_down[0], ln1_g[1], ln1_b[1], alpha)
    return _mlp_layer(h, mod1, mlp_w1[1], mlp_w2[1], ln2_g[1], ln2_b[1], alpha)
```

```python
import functools

import jax
import jax.numpy as jnp
import numpy as np
from jax import lax
from jax.experimental import pallas as pl
from jax.experimental.pallas import tpu as pltpu

F32 = jnp.float32
BF16 = jnp.bfloat16

GRID_W = 64
QKV_CONV_WIDTH = 5
MLSTM_HEADS = 4
GATE_PACK_LANES = 8 * MLSTM_HEADS
LN_EPS = 1e-5

V7X_LANES = 128
V7X_SUBLANES = 8
V7X_BF16_ROWS = 16
V7X_MXU_DIM = 256
V7X_VMEM_LIMIT_BYTES = 56 * 1024 * 1024

WIDE_TOKEN_TILE = 512
NARROW_TOKEN_TILE = 256
SCAN_CHUNK = 256
SCAN_CHUNKS_PER_STEP = 4
MLP_FF_CHUNK = 1024
ADALN_COLUMN_BLOCKS = 4
CONV_ROW_BLOCK = 64
MOD_ROWS = 8
NEG_BIG = -1e30


def _resident(shape):
    zeros = (0,) * len(shape)
    return pl.BlockSpec(shape, lambda *_: zeros, pipeline_mode=pl.Buffered(1))


def _params(*semantics):
    return pltpu.CompilerParams(dimension_semantics=semantics,
                                vmem_limit_bytes=V7X_VMEM_LIMIT_BYTES)


def _token_tile(s, preferred):
    t = min(s, preferred)
    assert s % t == 0
    return t


def _layer_norm(r, g, b):
    mu = jnp.mean(r, axis=-1, keepdims=True)
    d = r - mu
    var = jnp.mean(d * d, axis=-1, keepdims=True)
    return d * lax.rsqrt(var + LN_EPS) * g + b


def _silu(x):
    return x * jax.nn.sigmoid(x)


def _dot(a, b):
    return jnp.dot(a, b, preferred_element_type=F32)


def _row(v):
    return v.reshape(1, -1)


def _adaln_kernel(c_ref, w_ref, b_ref, o_ref):
    o_ref[0] = _dot(_silu(c_ref[...]), w_ref[0]) + b_ref[0]


def _adaln(cvec, ada_w, ada_b):
    depth, d, n = ada_w.shape
    assert n % (ADALN_COLUMN_BLOCKS * V7X_LANES) == 0
    tn = n // ADALN_COLUMN_BLOCKS
    return pl.pallas_call(
        _adaln_kernel,
        grid=(depth, n // tn),
        in_specs=[pl.BlockSpec((MOD_ROWS, d), lambda i, j: (0, 0)),
                  pl.BlockSpec((1, d, tn), lambda i, j: (i, 0, j)),
                  pl.BlockSpec((1, 1, tn), lambda i, j: (i, 0, j))],
        out_specs=pl.BlockSpec((1, MOD_ROWS, tn), lambda i, j: (i, 0, j)),
        out_shape=jax.ShapeDtypeStruct((depth, MOD_ROWS, n), F32),
        compiler_params=_params("parallel", "parallel"),
        name="adaln",
    )(cvec, ada_w, ada_b.reshape(depth, 1, n))


def _dft_tables(seg, taps):
    left = (taps - 1) // 2
    nfft = 1 << (seg + taps - 2).bit_length()
    nf = nfft // 2 + 1
    nfp = -(-nf // V7X_SUBLANES) * V7X_SUBLANES
    taps_padded = -(-taps // V7X_SUBLANES) * V7X_SUBLANES
    f = np.arange(nf, dtype=np.float64)[:, None]
    ang = 2.0 * np.pi * f * np.arange(seg)[None, :] / nfft
    fwd = np.zeros((2 * nfp, seg))
    fwd[:nf], fwd[nfp:nfp + nf] = np.cos(ang), -np.sin(ang)
    weight = np.where((f == 0) | (f == nfft // 2), 1.0, 2.0) / nfft
    inv = np.zeros((seg, 2 * nfp))
    inv[:, :nf], inv[:, nfp:nfp + nf] = (weight * np.cos(ang)).T, (-weight * np.sin(ang)).T
    ang_w = 2.0 * np.pi * f * (left - np.arange(taps))[None, :] / nfft
    spec = np.zeros((2 * nfp, taps_padded))
    spec[:nf, :taps], spec[nfp:nfp + nf, :taps] = np.cos(ang_w), -np.sin(ang_w)
    group = max(1, V7X_MXU_DIM // seg)
    eye = np.eye(group)
    fwd = np.concatenate([np.kron(eye, fwd[:nfp]), np.kron(eye, fwd[nfp:])], axis=0)
    inv = np.concatenate([np.kron(eye, inv[:, :nfp]), np.kron(eye, inv[:, nfp:])], axis=1)
    return fwd.astype(np.float32), inv.astype(np.float32), spec.astype(np.float32)


def _filter_spectrum_kernel(tab_ref, w_ref, o_ref):
    o_ref[...] = jnp.dot(tab_ref[...], w_ref[...], preferred_element_type=F32,
                         precision=lax.Precision.HIGHEST)


def _filter_spectrum(spec_tab, w_dw):
    rows, taps_padded = spec_tab.shape
    c = w_dw.shape[1]
    w = jnp.pad(w_dw, ((0, taps_padded - w_dw.shape[0]), (0, 0)))
    return pl.pallas_call(
        _filter_spectrum_kernel,
        out_shape=jax.ShapeDtypeStruct((rows, c), F32),
        name="conv_filter_spectrum",
    )(jnp.asarray(spec_tab), w)


def _conv_mixer_kernel(h_ref, mod_ref, wpw1_ref, bpw1_ref, fwd_ref, inv_ref, gspec_ref, bdw_ref, cg_ref, cb_ref,
                       wpw2_ref, bpw2_ref, lng_ref, lnb_ref, o_ref, *, alpha, seg):
    c = wpw2_ref.shape[0]
    rows = fwd_ref.shape[1]
    nfp = gspec_ref.shape[0] // 2
    t = h_ref.shape[1]
    h = h_ref[0]
    sh, sc, g1 = mod_ref[0, 0:1, :], mod_ref[0, 1:2, :], mod_ref[0, 2:3, :]
    u = (h * (1.0 + sc) + sh).astype(BF16)

    def glu_slab(p):
        lo = slice(p * V7X_MXU_DIM, (p + 1) * V7X_MXU_DIM)
        hi = slice(c + p * V7X_MXU_DIM, c + (p + 1) * V7X_MXU_DIM)
        a = _dot(u, wpw1_ref[:, lo]) + bpw1_ref[:, lo]
        gate = _dot(u, wpw1_ref[:, hi]) + bpw1_ref[:, hi]
        return (a * jax.nn.sigmoid(gate)).astype(BF16)

    def conv_slab(p, glu):
        lo = slice(p * V7X_MXU_DIM, (p + 1) * V7X_MXU_DIM)
        group = rows // seg
        g_re = jnp.concatenate([gspec_ref[0:nfp, lo]] * group, axis=0)
        g_im = jnp.concatenate([gspec_ref[nfp:, lo]] * group, axis=0)
        out = []
        for r0 in range(0, t, rows):
            spec = _dot(fwd_ref[...], glu[r0:r0 + rows])
            a_re, a_im = spec[0:group * nfp], spec[group * nfp:]
            prod = jnp.concatenate([a_re * g_re - a_im * g_im, a_re * g_im + a_im * g_re], axis=0)
            out.append(_dot(inv_ref[...], prod.astype(BF16)))
        return jnp.concatenate(out, axis=0) if len(out) > 1 else out[0]

    nslab = c // V7X_MXU_DIM
    slabs = []
    glu = glu_slab(0)
    for p in range(nslab):
        nxt = glu_slab(p + 1) if p + 1 < nslab else None
        slabs.append(conv_slab(p, glu))
        glu = nxt
    conv = jnp.concatenate(slabs, axis=1) + bdw_ref[...]
    act = _silu(_layer_norm(conv, cg_ref[...], cb_ref[...])).astype(BF16)
    y = _dot(act, wpw2_ref[...]) + bpw2_ref[...]
    o_ref[0] = _layer_norm(alpha * h + g1 * y, lng_ref[...], lnb_ref[...])


def _conv_mixer_layer(h, seg, mod, p, alpha):
    bsz, s, d = h.shape
    t = _token_tile(s, WIDE_TOKEN_TILE)
    assert t % seg == 0 and seg % V7X_BF16_ROWS == 0
    c = p["w_pw2"].shape[0]
    fwd, inv, spec_tab = _dft_tables(seg, p["w_dw"].shape[0])
    assert t % fwd.shape[1] == 0 and c % V7X_MXU_DIM == 0
    gspec = _filter_spectrum(spec_tab, p["w_dw"])
    return pl.pallas_call(
        functools.partial(_conv_mixer_kernel, alpha=alpha, seg=seg),
        grid=(bsz, s // t),
        in_specs=[pl.BlockSpec((1, t, d), lambda b, j: (b, j, 0)),
                  pl.BlockSpec((1, MOD_ROWS, d), lambda b, j: (b, 0, 0)),
                  _resident((d, 2 * c)), _resident((1, 2 * c)),
                  _resident(fwd.shape), _resident(inv.shape), _resident(gspec.shape), _resident((1, c)),
                  _resident((1, c)), _resident((1, c)),
                  _resident((c, d)), _resident((1, d)), _resident((1, d)), _resident((1, d))],
        out_specs=pl.BlockSpec((1, t, d), lambda b, j: (b, j, 0)),
        out_shape=jax.ShapeDtypeStruct((bsz, s, d), F32),
        compiler_params=_params("parallel", "parallel"),
        name="conv_mixer_layer",
    )(h, mod, p["w_pw1"].astype(BF16), _row(p["b_pw1"]), jnp.asarray(fwd, BF16), jnp.asarray(inv, BF16),
      gspec, _row(p["b_dw"]), _row(p["ln_g"]), _row(p["ln_b"]), p["w_pw2"].astype(BF16),
      _row(p["b_pw2"]), _row(p["ln1_g"]), _row(p["ln1_b"]))


def _mlp_kernel(h_ref, mod_ref, w1_ref, w2_ref, g_ref, b_ref, o_ref, *, alpha):
    h = h_ref[0]
    sh, sc, g2 = mod_ref[0, 3:4, :], mod_ref[0, 4:5, :], mod_ref[0, 5:6, :]
    u = (h * (1.0 + sc) + sh).astype(BF16)
    ff = w1_ref.shape[1]
    acc = jnp.zeros(h.shape, F32)
    for f0 in range(0, ff, MLP_FF_CHUNK):
        hid = jnp.maximum(_dot(u, w1_ref[:, f0:f0 + MLP_FF_CHUNK]), 0.0)
        acc = acc + _dot((hid * hid).astype(BF16), w2_ref[f0:f0 + MLP_FF_CHUNK, :])
    o_ref[0] = _layer_norm(alpha * h + g2 * acc, g_ref[...], b_ref[...])


def _mlp_layer(h, mod, w1, w2, ln_g, ln_b, alpha):
    bsz, s, d = h.shape
    t = _token_tile(s, WIDE_TOKEN_TILE)
    ff = w1.shape[1]
    assert ff % MLP_FF_CHUNK == 0
    return pl.pallas_call(
        functools.partial(_mlp_kernel, alpha=alpha),
        grid=(bsz, s // t),
        in_specs=[pl.BlockSpec((1, t, d), lambda b, j: (b, j, 0)),
                  pl.BlockSpec((1, MOD_ROWS, d), lambda b, j: (b, 0, 0)),
                  _resident((d, ff)), _resident((ff, d)), _resident((1, d)), _resident((1, d))],
        out_specs=pl.BlockSpec((1, t, d), lambda b, j: (b, j, 0)),
        out_shape=jax.ShapeDtypeStruct((bsz, s, d), F32),
        compiler_params=_params("parallel", "parallel"),
        name="mlp_layer",
    )(h, mod, w1.astype(BF16), w2.astype(BF16), _row(ln_g), _row(ln_b))


def _mlstm_up_kernel(h_ref, prev_ref, next_ref, mod_ref, wup_ref, wconv_ref, bconv_ref,
                     xm_ref, xc_ref, z_ref, u_ref, xm_all_ref):
    t = h_ref.shape[1]
    e = xm_ref.shape[2]
    halo = prev_ref.shape[1]
    slab = 2 * V7X_MXU_DIM
    per_slab = slab // V7X_LANES
    left = (QKV_CONV_WIDTH - 1) // 2
    j = pl.program_id(1)
    sh, sc = mod_ref[0, 0:1, :], mod_ref[0, 1:2, :]
    modulate = lambda v: (v * (1.0 + sc) + sh).astype(BF16)
    u_ref[0:halo, :] = modulate(prev_ref[0])
    u_ref[halo:halo + t, :] = modulate(h_ref[0])
    u_ref[halo + t:, :] = modulate(next_ref[0])
    keep_prev = jnp.where(j == 0, 0.0, 1.0)
    keep_next = jnp.where(j == pl.num_programs(1) - 1, 0.0, 1.0)

    def xm_matmul(p):
        return _dot(u_ref[...], wup_ref[:, p * slab:(p + 1) * slab])

    def xm_store(p, xm_all):
        cols = slice(p * slab, (p + 1) * slab)
        xm_ref[0, :, cols] = xm_all[halo:halo + t].astype(BF16)
        for q in range(per_slab):
            cb = per_slab * p + q
            lanes = slice(q * V7X_LANES, (q + 1) * V7X_LANES)
            xm_all_ref[cb, 0:halo, :] = xm_all[0:halo, lanes] * keep_prev
            xm_all_ref[cb, halo:halo + t, :] = xm_all[halo:halo + t, lanes]
            xm_all_ref[cb, halo + t:, :] = xm_all[halo + t:, lanes] * keep_next

    def conv_block(cb):
        lanes = slice(cb * V7X_LANES, (cb + 1) * V7X_LANES)
        for r0 in range(0, t, CONV_ROW_BLOCK):
            acc = jnp.zeros((CONV_ROW_BLOCK, V7X_LANES), F32)
            for k in range(QKV_CONV_WIDTH):
                start = halo + r0 - left + k
                acc = acc + xm_all_ref[cb, start:start + CONV_ROW_BLOCK, :] * wconv_ref[k:k + 1, lanes]
            xc_ref[0, r0:r0 + CONV_ROW_BLOCK, lanes] = _silu(acc + bconv_ref[:, lanes]).astype(BF16)

    nslab = e // slab
    for p in range(nslab):
        xm_store(p, xm_matmul(p))
    for p in range(nslab):
        cols = slice(e + p * slab, e + (p + 1) * slab)
        z_ref[0, :, p * slab:(p + 1) * slab] = _dot(u_ref[halo:halo + t, :], wup_ref[:, cols]).astype(BF16)
        for q in range(per_slab):
            conv_block(per_slab * p + q)


def _mlstm_up(h, mod, w_up, w_conv, b_conv):
    bsz, s, d = h.shape
    e = w_up.shape[1] // 2
    t = _token_tile(s, WIDE_TOKEN_TILE)
    halo = V7X_BF16_ROWS
    assert t % halo == 0 and t % CONV_ROW_BLOCK == 0
    nhb = s // halo
    taps = w_conv.shape[0]
    w_conv = jnp.pad(w_conv, ((0, V7X_SUBLANES - taps), (0, 0)))
    tile = pl.BlockSpec((1, t, e), lambda b, j: (b, j, 0))
    out = jax.ShapeDtypeStruct((bsz, s, e), BF16)
    return pl.pallas_call(
        _mlstm_up_kernel,
        grid=(bsz, s // t),
        in_specs=[pl.BlockSpec((1, t, d), lambda b, j: (b, j, 0)),
                  pl.BlockSpec((1, halo, d), lambda b, j: (b, jnp.maximum(j * (t // halo) - 1, 0), 0)),
                  pl.BlockSpec((1, halo, d), lambda b, j: (b, jnp.minimum((j + 1) * (t // halo), nhb - 1), 0)),
                  pl.BlockSpec((1, MOD_ROWS, d), lambda b, j: (b, 0, 0)),
                  _resident((d, 2 * e)), _resident((V7X_SUBLANES, e)), _resident((1, e))],
        out_specs=[tile, tile, tile],
        out_shape=[out, out, out],
        scratch_shapes=[pltpu.VMEM((t + 2 * halo, d), BF16),
                        pltpu.VMEM((e // V7X_LANES, t + 2 * halo, V7X_LANES), F32)],
        compiler_params=_params("parallel", "parallel"),
        name="mlstm_up",
    )(h, h, h, mod, w_up.astype(BF16), w_conv, _row(b_conv))


def _log_sigmoid(x):
    return jnp.minimum(x, 0.0) - jnp.log1p(jnp.exp(-jnp.abs(x)))


def _split_bf16x3(x):
    hi = x.astype(BF16)
    r1 = x - hi.astype(F32)
    mid = r1.astype(BF16)
    lo = (r1 - mid.astype(F32)).astype(BF16)
    return jnp.concatenate([hi, mid, lo], axis=1)


def _mlstm_qkv_kernel(*refs, k_scale, emit_intra):
    refs = list(refs)
    xc_ref, xm_ref, wq_ref, wk_ref, wv_ref, wg_ref, bg_ref, q_ref, kt_ref, v_ref, g_ref = refs[:11]
    h0_ref = refs[11] if emit_intra else None
    e = wq_ref.shape[0]
    t = xc_ref.shape[1]
    nh = MLSTM_HEADS
    dh = e // nh
    half = V7X_LANES // 2
    xc = xc_ref[0]
    q = _dot(xc, wq_ref[...]).astype(BF16)
    q_ref[0] = q
    gates = _dot(q, wg_ref[0:e, :])
    k = _dot(xc, wk_ref[...])
    kt_ref[0, 0] = (k * k_scale).T.astype(BF16)
    gates = gates + _dot(k.astype(BF16), wg_ref[e:2 * e, :])
    v = _dot(xm_ref[0], wv_ref[...]).astype(BF16)
    v_ref[0] = v
    gates = gates + _dot(v, wg_ref[2 * e:, :]) + bg_ref[...]

    lane = lax.broadcasted_iota(jnp.int32, (1, V7X_LANES), 1)
    group = (lane % GATE_PACK_LANES) // nh
    rows = lax.broadcasted_iota(jnp.int32, (t, t), 0)
    cols = lax.broadcasted_iota(jnp.int32, (t, t), 1)
    visible = (rows >= cols, rows <= cols)
    lf = _log_sigmoid(gates)
    parts = _split_bf16x3(lf)
    fold = lambda c3: c3[:, :V7X_LANES] + c3[:, V7X_LANES:2 * V7X_LANES] + c3[:, 2 * V7X_LANES:]
    prefix = fold(_dot(jnp.where(visible[0], 1.0, 0.0).astype(BF16), parts))
    suffix = fold(_dot(jnp.where(visible[1], 1.0, 0.0).astype(BF16), parts))
    b = jnp.where(lane < GATE_PACK_LANES, prefix, suffix)
    a = pltpu.roll(gates, half, 1) - b
    total = jnp.sum(lf, axis=0, keepdims=True)
    amax = jnp.max(a, axis=0, keepdims=True)
    pack = jnp.where(group == 0, b, jnp.where(group == 2, a, jnp.where(group == 4, total,
                                                                         jnp.where(group == 5, amax, 0.0))))
    if emit_intra:
        a_rows = a.T
        qk_of = lambda hd: _dot(q[:, hd * dh:(hd + 1) * dh], kt_ref[0, 0, hd * dh:(hd + 1) * dh, :])
        qk_next = qk_of(0)
        for hd in range(nh):
            lanes = slice(hd * dh, (hd + 1) * dh)
            qk = qk_next
            if hd + 1 < nh:
                qk_next = qk_of(hd + 1)
            for d in range(2):
                base = d * GATE_PACK_LANES + hd
                a_row = a_rows[base + 2 * nh:base + 2 * nh + 1, :]
                masked = jnp.where(visible[d], a_row, NEG_BIG)
                cmax = jnp.max(masked, axis=1, keepdims=True)
                scores = qk * jnp.exp(masked - cmax)
                rowsum = jnp.sum(scores, axis=1, keepdims=True)
                h0_ref[d, 0, :, lanes] = _dot(scores.astype(BF16), v[:, lanes]).astype(BF16)
                pack = jnp.where(lane == base + nh, cmax, jnp.where(lane == base + 3 * nh, rowsum, pack))
    g_ref[0] = pack[:, :g_ref.shape[2]]


def _gate_lane_layout():
    nh = MLSTM_HEADS
    src = np.full((V7X_LANES,), -1, np.int32)
    for d in range(2):
        for g in range(GATE_PACK_LANES // nh):
            for h in range(nh):
                lane = d * GATE_PACK_LANES + g * nh + h
                if g in (0, 2, 4, 5):
                    src[lane] = (2 * d + 1) * nh + h
                if g in (2, 5):
                    src[V7X_LANES // 2 + lane] = 2 * d * nh + h
    return src


def _mlstm_qkv(xc, xm, w_q, w_k, w_v, w_gate, b_gate, emit_intra):
    bsz, s, e = xc.shape
    assert 2 * GATE_PACK_LANES == V7X_LANES // 2
    t = _token_tile(s, SCAN_CHUNK)
    k_scale = float(e // MLSTM_HEADS) ** -0.5
    src = _gate_lane_layout()
    lay = lambda w: jnp.where(src >= 0, w[:, np.maximum(src, 0)], 0.0)
    tile = pl.BlockSpec((1, t, e), lambda b, j: (b, j, 0))
    out = jax.ShapeDtypeStruct((bsz, s, e), BF16)
    out_specs = [tile, pl.BlockSpec((1, 1, e, t), lambda b, j: (b, j, 0, 0)), tile,
                 pl.BlockSpec((1, t, 2 * GATE_PACK_LANES), lambda b, j: (b, j, 0))]
    out_shape = [out, jax.ShapeDtypeStruct((bsz, s // t, e, t), BF16), out,
                 jax.ShapeDtypeStruct((bsz, s, 2 * GATE_PACK_LANES), F32)]
    if emit_intra:
        out_specs.append(pl.BlockSpec((2, 1, t, e), lambda b, j: (0, b, j, 0)))
        out_shape.append(jax.ShapeDtypeStruct((2, bsz, s, e), BF16))
    outs = pl.pallas_call(
        functools.partial(_mlstm_qkv_kernel, k_scale=k_scale, emit_intra=emit_intra),
        grid=(bsz, s // t),
        in_specs=[tile, tile, _resident((e, e)), _resident((e, e)), _resident((e, e)),
                  _resident((3 * e, V7X_LANES)), _resident((1, V7X_LANES))],
        out_specs=out_specs,
        out_shape=out_shape,
        compiler_params=_params("parallel", "parallel"),
        name="mlstm_qkv",
    )(xc, xm, w_q.astype(BF16), w_k.astype(BF16), w_v.astype(BF16), lay(w_gate).astype(BF16),
      lay(_row(b_gate)))
    return tuple(outs) if emit_intra else tuple(outs) + (None,)


def _mlstm_scan_kernel(*refs, has_init, emit_h, emit_state):
    refs = list(refs)
    q_ref, kt_ref, v_ref, g_ref, gt_ref = refs[:5]
    del refs[:5]
    if emit_h:
        h0_ref = refs.pop(0)
    if has_init:
        c0_ref, n0_ref, m0_ref = refs[:3]
        del refs[:3]
    if emit_h:
        o_ref = refs.pop(0)
    if emit_state:
        cf_ref, nf_ref, mf_ref = refs[:3]
        del refs[:3]
    c_ref, n_ref, m_ref = refs

    length = kt_ref.shape[3]
    chunks = kt_ref.shape[1]
    nh, dh = c_ref.shape[0], c_ref.shape[1]
    backward = pl.program_id(1) == 1
    step = pl.program_id(2)

    @pl.when(step == 0)
    def _():
        if has_init:
            c_ref[...] = c0_ref[0, 0]
            n_ref[...] = n0_ref[0, 0]
            m_ref[...] = m0_ref[0, 0]
        else:
            c_ref[...] = jnp.zeros(c_ref.shape, F32)
            n_ref[...] = jnp.zeros(n_ref.shape, F32)
            m_ref[...] = jnp.zeros(m_ref.shape, F32)

    ones_rows = jnp.ones((V7X_SUBLANES, length), BF16)
    contract_lanes = (((1,), (1,)), ((), ()))

    def one_chunk(i, carry):
        sub = jnp.where(backward, chunks - 1 - i, i)
        rows = pl.ds(pl.multiple_of(sub * length, length), length)
        g = g_ref[0, 0, rows, :]
        gt = gt_ref[0, 0, sub]
        grp = lambda i: g[:, i * nh:(i + 1) * nh]
        grp_t = lambda i: gt[i * nh:(i + 1) * nh, :]
        m_old = m_ref[nh:nh + 1, 0:nh]
        m_old_t = m_ref[0:nh, 0:1]
        total, amax = grp(4)[0:1], grp(5)[0:1]
        m_new = total + jnp.maximum(m_old, amax)
        decay = jnp.exp(total + m_old - m_new)
        m_new_t = grp_t(4) + jnp.maximum(m_old_t, grp_t(5))
        w_k = jnp.exp(grp_t(4) + grp_t(2) - m_new_t)
        if emit_h:
            b_t, cmax, rowsum = grp(0), grp(1), grp(3)
            m_row = jnp.maximum(m_old, cmax)
            r_intra = jnp.exp(cmax - m_row)
            w_inter = jnp.exp(m_old - m_row)
            den_floor = jnp.exp(-(b_t + m_row))

            for h in range(nh):
                lanes = slice(h * dh, (h + 1) * dh)
                col = slice(h, h + 1)
                q = q_ref[0, rows, lanes]
                q_n = lax.dot_general(q, n_ref[h].astype(BF16), contract_lanes,
                                      preferred_element_type=F32)[:, 0:1]
                num = (r_intra[:, col] * h0_ref[0, 0, rows, lanes].astype(F32)
                       + w_inter[:, col] * _dot(q, c_ref[h].astype(BF16)))
                den = r_intra[:, col] * rowsum[:, col] + w_inter[:, col] * q_n
                o_ref[0, 0, rows, lanes] = (num / jnp.maximum(jnp.abs(den), den_floor[:, col])).astype(o_ref.dtype)

        for h in range(nh):
            lanes = slice(h * dh, (h + 1) * dh)
            col = slice(h, h + 1)
            wkt = (kt_ref[0, sub, lanes, :].astype(F32) * w_k[h:h + 1, :]).astype(BF16)
            c_ref[h] = decay[:, col] * c_ref[h] + _dot(wkt, v_ref[0, rows, lanes])
            n_ref[h] = decay[:, col] * n_ref[h] + lax.dot_general(ones_rows, wkt, contract_lanes,
                                                                  preferred_element_type=F32)
        m_ref[0:nh, :] = m_new_t[:, 0:V7X_LANES]
        m_ref[nh:nh + 1, 0:nh] = m_new
        return carry

    lax.fori_loop(0, chunks, one_chunk, 0)

    if emit_state:
        @pl.when(step == pl.num_programs(2) - 1)
        def _():
            cf_ref[0, 0] = c_ref[...]
            nf_ref[0, 0] = n_ref[...]
            mf_ref[0, 0] = m_ref[...]


def _mlstm_scan(q, kt, v, gates, h0, init_state, emit_state):
    bsz, s, e = q.shape
    nh = MLSTM_HEADS
    dh = e // nh
    emit_h = h0 is not None
    length = _token_tile(s, SCAN_CHUNK)
    nch = s // length
    per_step = SCAN_CHUNKS_PER_STEP if nch % SCAN_CHUNKS_PER_STEP == 0 else 1
    nst = nch // per_step
    span = per_step * length
    g = gates.reshape(bsz, s, 2, GATE_PACK_LANES).transpose(0, 2, 1, 3)
    g_t = g.reshape(bsz, 2, nch, length, GATE_PACK_LANES).transpose(0, 1, 2, 4, 3)
    chunk = lambda d, st: jnp.where(d == 0, st, nst - 1 - st)
    tile = pl.BlockSpec((1, span, e), lambda b, d, st: (b, chunk(d, st), 0))
    dir_tile = pl.BlockSpec((1, 1, span, e), lambda b, d, st: (d, b, chunk(d, st), 0))
    state_shapes = [(nh, dh, dh), (nh, V7X_SUBLANES, dh), (V7X_SUBLANES, V7X_LANES)]
    state_specs = [pl.BlockSpec((1, 1) + shp, lambda b, d, st, n=len(shp): (b, d) + (0,) * n)
                   for shp in state_shapes]
    scratch = [pltpu.VMEM(shp, F32) for shp in state_shapes]

    in_specs = [tile, pl.BlockSpec((1, per_step, e, length), lambda b, d, st: (b, chunk(d, st), 0, 0)), tile,
                pl.BlockSpec((1, 1, span, GATE_PACK_LANES), lambda b, d, st: (b, d, chunk(d, st), 0)),
                pl.BlockSpec((1, 1, per_step, GATE_PACK_LANES, length),
                             lambda b, d, st: (b, d, chunk(d, st), 0, 0))]
    args = [q, kt, v, g, g_t]
    if emit_h:
        in_specs.append(dir_tile)
        args.append(h0)
    if init_state is not None:
        in_specs += state_specs
        args += list(init_state)
    out_specs, out_shape = [], []
    if emit_h:
        out_specs.append(dir_tile)
        out_shape.append(jax.ShapeDtypeStruct((2, bsz, s, e), BF16))
    if emit_state:
        out_specs += state_specs
        out_shape += [jax.ShapeDtypeStruct((bsz, 2) + shp, F32) for shp in state_shapes]
    outs = pl.pallas_call(
        functools.partial(_mlstm_scan_kernel, has_init=init_state is not None, emit_h=emit_h,
                          emit_state=emit_state),
        grid=(bsz, 2, nst),
        in_specs=in_specs,
        out_specs=out_specs,
        out_shape=out_shape,
        scratch_shapes=scratch,
        compiler_params=_params("parallel", "parallel", "arbitrary"),
        name="mlstm_scan",
    )(*args)
    outs = list(outs)
    h = outs.pop(0) if emit_h else None
    return h, (tuple(outs) if emit_state else None)


def _mlstm_out_kernel(h_ref, xm_ref, xc_ref, z_ref, hs_ref, mod_ref, wo_ref, bo_ref, gn_ref, skip_ref,
                      wdown_ref, lng_ref, lnb_ref, o_ref, *, alpha):
    e = xm_ref.shape[2]
    dh = e // MLSTM_HEADS
    sub_rows = NARROW_TOKEN_TILE
    g1 = mod_ref[0, 2:3, :]

    def sub_tile(i, carry):
        rows = pl.ds(pl.multiple_of(i * sub_rows, sub_rows), sub_rows)
        h = h_ref[0, rows, :]
        xm = xm_ref[0, rows, :]
        acc = jnp.zeros(h.shape, F32)

        def gate_logits(hd):
            lanes = slice(hd * dh, (hd + 1) * dh)
            lanes_b = slice(e + hd * dh, e + (hd + 1) * dh)
            return (_dot(xm, wo_ref[:, lanes]) + bo_ref[:, lanes],
                    _dot(xm, wo_ref[:, lanes_b]) + bo_ref[:, lanes_b])

        logits = gate_logits(0)
        for hd in range(MLSTM_HEADS):
            lanes = slice(hd * dh, (hd + 1) * dh)
            o_f, o_b = jax.nn.sigmoid(logits[0]), jax.nn.sigmoid(logits[1])
            if hd + 1 < MLSTM_HEADS:
                logits = gate_logits(hd + 1)
            hsum = (o_f * hs_ref[0, 0, rows, lanes].astype(F32)
                    + o_b * hs_ref[1, 0, rows, lanes].astype(F32))
            mu = jnp.mean(hsum, axis=-1, keepdims=True)
            dlt = hsum - mu
            var = jnp.mean(dlt * dlt, axis=-1, keepdims=True)
            hn = dlt * lax.rsqrt(var + LN_EPS) * gn_ref[:, lanes]
            y = ((hn + skip_ref[:, lanes] * xc_ref[0, rows, lanes].astype(F32))
                 * _silu(z_ref[0, rows, lanes].astype(F32)))
            acc = acc + _dot(y.astype(BF16), wdown_ref[lanes, :])
        o_ref[0, rows, :] = _layer_norm(alpha * h + g1 * acc, lng_ref[...], lnb_ref[...])
        return carry

    lax.fori_loop(0, h_ref.shape[1] // sub_rows, sub_tile, 0)


def _mlstm_out(h, xm, xc, z, hs, mod, w_o, b_o, gn_g, skip, w_down, ln_g, ln_b, alpha):
    bsz, s, d = h.shape
    e = xm.shape[2]
    t = _token_tile(s, WIDE_TOKEN_TILE)
    assert t % NARROW_TOKEN_TILE == 0
    tile = lambda width: pl.BlockSpec((1, t, width), lambda b, j: (b, j, 0))
    return pl.pallas_call(
        functools.partial(_mlstm_out_kernel, alpha=alpha),
        grid=(bsz, s // t),
        in_specs=[tile(d), tile(e), tile(e), tile(e),
                  pl.BlockSpec((2, 1, t, e), lambda b, j: (0, b, j, 0)),
                  pl.BlockSpec((1, MOD_ROWS, d), lambda b, j: (b, 0, 0)),
                  _resident((e, 2 * e)), _resident((1, 2 * e)), _resident((1, e)), _resident((1, e)),
                  _resident((e, d)), _resident((1, d)), _resident((1, d))],
        out_specs=pl.BlockSpec((1, t, d), lambda b, j: (b, j, 0)),
        out_shape=jax.ShapeDtypeStruct((bsz, s, d), F32),
        compiler_params=_params("parallel", "parallel"),
        name="mlstm_out",
    )(h, xm, xc, z, hs, mod, w_o.astype(BF16), _row(b_o), _row(gn_g), _row(skip),
      w_down.astype(BF16), _row(ln_g), _row(ln_b))


def _mod_tables(ada_out_i, bsz, d):
    m = ada_out_i.reshape(MOD_ROWS, 6, d)
    pad = ((0, 0), (0, MOD_ROWS - 6), (0, 0))
    lat = jnp.pad(m[:bsz], pad)
    ctx = jnp.pad(jnp.broadcast_to(m[bsz][None], (bsz, 6, d)), pad)
    return lat, ctx


def kernel(x, c, ctx, c_ctx, ada_w, ada_b, ln1_g, ln1_b, ln2_g, ln2_b, mlp_w1, mlp_w2, conv_w_pw1, conv_b_pw1, conv_w_dw, conv_b_dw, conv_ln_g, conv_ln_b, conv_w_pw2, conv_b_pw2, ml_w_up, ml_w_conv, ml_b_conv, ml_w_q, ml_w_k, ml_w_v, ml_w_gate, ml_b_gate, ml_w_o, ml_b_o, ml_gn_g, ml_skip, ml_w_down):
    bsz, s, d = x.shape
    depth = ada_w.shape[0]
    assert depth == 2 and bsz + 1 <= MOD_ROWS, "built for the two-layer block: conv mixer, then mLSTM"
    alpha = (2 * depth) ** 0.25

    cvec = jnp.concatenate([c, c_ctx[None], jnp.zeros((MOD_ROWS - bsz - 1, d), F32)], axis=0)
    ada = _adaln(cvec, ada_w, ada_b)
    mod0, mod0_ctx = _mod_tables(ada[0], bsz, d)
    mod1, mod1_ctx = _mod_tables(ada[1], bsz, d)

    conv_p = dict(w_pw1=conv_w_pw1[0], b_pw1=conv_b_pw1[0], w_dw=conv_w_dw[0], b_dw=conv_b_dw[0],
                  ln_g=conv_ln_g[0], ln_b=conv_ln_b[0], w_pw2=conv_w_pw2[0], b_pw2=conv_b_pw2[0],
                  ln1_g=ln1_g[0], ln1_b=ln1_b[0])
    h = _conv_mixer_layer(x, GRID_W, mod0, conv_p, alpha)
    h = _mlp_layer(h, mod0, mlp_w1[0], mlp_w2[0], ln2_g[0], ln2_b[0], alpha)
    hc = _conv_mixer_layer(ctx, ctx.shape[1], mod0_ctx, conv_p, alpha)
    hc = _mlp_layer(hc, mod0_ctx, mlp_w1[0], mlp_w2[0], ln2_g[0], ln2_b[0], alpha)

    feats = lambda hh, mm: _mlstm_up(hh, mm, ml_w_up[0], ml_w_conv[0], ml_b_conv[0])
    qkv = lambda xc_, xm_, intra: _mlstm_qkv(xc_, xm_, ml_w_q[0], ml_w_k[0], ml_w_v[0], ml_w_gate[0],
                                             ml_b_gate[0], emit_intra=intra)
    xm_c, xc_c, _ = feats(hc, mod1_ctx)
    _, ctx_state = _mlstm_scan(*qkv(xc_c, xm_c, False), init_state=None, emit_state=True)
    xm, xc, z = feats(h, mod1)
    hs, _ = _mlstm_scan(*qkv(xc, xm, True), init_state=ctx_state, emit_state=False)
    h = _mlstm_out(h, xm, xc, z, hs, mod1, ml_w_o[0], ml_b_o[0], ml_gn_g[0], ml_skip[0],
                   ml_w_down[0], ln1_g[1], ln1_b[1], alpha)
    return _mlp_layer(h, mod1, mlp_w1[1], mlp_w2[1], ln2_g[1], ln2_b[1], alpha)
```

```python
import functools

import jax
import jax.numpy as jnp
import numpy as np
from jax import lax
from jax.experimental import pallas as pl
from jax.experimental.pallas import tpu as pltpu

F32 = jnp.float32
BF16 = jnp.bfloat16

GRID_W = 64
QKV_CONV_WIDTH = 5
MLSTM_HEADS = 4
GATE_PACK_LANES = 8 * MLSTM_HEADS
LN_EPS = 1e-5

V7X_LANES = 128
V7X_SUBLANES = 8
V7X_BF16_ROWS = 16
V7X_MXU_DIM = 256
V7X_VMEM_LIMIT_BYTES = 56 * 1024 * 1024

WIDE_TOKEN_TILE = 512
NARROW_TOKEN_TILE = 256
SCAN_CHUNK = 256
SCAN_CHUNKS_PER_STEP = 4
MLP_FF_CHUNK = 1024
ADALN_COLUMN_BLOCKS = 4
CONV_ROW_BLOCK = 64
MOD_ROWS = 8
NEG_BIG = -1e30


def _resident(shape):
    zeros = (0,) * len(shape)
    return pl.BlockSpec(shape, lambda *_: zeros, pipeline_mode=pl.Buffered(1))


def _params(*semantics):
    return pltpu.CompilerParams(dimension_semantics=semantics,
                                vmem_limit_bytes=V7X_VMEM_LIMIT_BYTES)


def _token_tile(s, preferred):
    t = min(s, preferred)
    assert s % t == 0
    return t


def _layer_norm(r, g, b):
    mu = jnp.mean(r, axis=-1, keepdims=True)
    d = r - mu
    var = jnp.mean(d * d, axis=-1, keepdims=True)
    return d * lax.rsqrt(var + LN_EPS) * g + b


def _silu(x):
    return x * jax.nn.sigmoid(x)


def _dot(a, b):
    return jnp.dot(a, b, preferred_element_type=F32)


def _row(v):
    return v.reshape(1, -1)


def _adaln_kernel(c_ref, w_ref, b_ref, o_ref):
    o_ref[0] = _dot(_silu(c_ref[...]), w_ref[0]) + b_ref[0]


def _adaln(cvec, ada_w, ada_b):
    depth, d, n = ada_w.shape
    assert n % (ADALN_COLUMN_BLOCKS * V7X_LANES) == 0
    tn = n // ADALN_COLUMN_BLOCKS
    return pl.pallas_call(
        _adaln_kernel,
        grid=(depth, n // tn),
        in_specs=[pl.BlockSpec((MOD_ROWS, d), lambda i, j: (0, 0)),
                  pl.BlockSpec((1, d, tn), lambda i, j: (i, 0, j)),
                  pl.BlockSpec((1, 1, tn), lambda i, j: (i, 0, j))],
        out_specs=pl.BlockSpec((1, MOD_ROWS, tn), lambda i, j: (i, 0, j)),
        out_shape=jax.ShapeDtypeStruct((depth, MOD_ROWS, n), F32),
        compiler_params=_params("parallel", "parallel"),
        name="adaln",
    )(cvec, ada_w, ada_b.reshape(depth, 1, n))


def _dft_tables(seg, taps):
    left = (taps - 1) // 2
    nfft = 1 << (seg + taps - 2).bit_length()
    nf = nfft // 2 + 1
    nfp = -(-nf // V7X_SUBLANES) * V7X_SUBLANES
    taps_padded = -(-taps // V7X_SUBLANES) * V7X_SUBLANES
    f = np.arange(nf, dtype=np.float64)[:, None]
    ang = 2.0 * np.pi * f * np.arange(seg)[None, :] / nfft
    fwd = np.zeros((2 * nfp, seg))
    fwd[:nf], fwd[nfp:nfp + nf] = np.cos(ang), -np.sin(ang)
    weight = np.where((f == 0) | (f == nfft // 2), 1.0, 2.0) / nfft
    inv = np.zeros((seg, 2 * nfp))
    inv[:, :nf], inv[:, nfp:nfp + nf] = (weight * np.cos(ang)).T, (-weight * np.sin(ang)).T
    ang_w = 2.0 * np.pi * f * (left - np.arange(taps))[None, :] / nfft
    spec = np.zeros((2 * nfp, taps_padded))
    spec[:nf, :taps], spec[nfp:nfp + nf, :taps] = np.cos(ang_w), -np.sin(ang_w)
    group = max(1, V7X_MXU_DIM // seg)
    eye = np.eye(group)
    fwd = np.concatenate([np.kron(eye, fwd[:nfp]), np.kron(eye, fwd[nfp:])], axis=0)
    inv = np.concatenate([np.kron(eye, inv[:, :nfp]), np.kron(eye, inv[:, nfp:])], axis=1)
    return fwd.astype(np.float32), inv.astype(np.float32), spec.astype(np.float32)


def _filter_spectrum_kernel(tab_ref, w_ref, o_ref):
    o_ref[...] = jnp.dot(tab_ref[...], w_ref[...], preferred_element_type=F32,
                         precision=lax.Precision.HIGHEST)


def _filter_spectrum(spec_tab, w_dw):
    rows, taps_padded = spec_tab.shape
    c = w_dw.shape[1]
    w = jnp.pad(w_dw, ((0, taps_padded - w_dw.shape[0]), (0, 0)))
    return pl.pallas_call(
        _filter_spectrum_kernel,
        out_shape=jax.ShapeDtypeStruct((rows, c), F32),
        name="conv_filter_spectrum",
    )(jnp.asarray(spec_tab), w)


def _conv_mixer_kernel(h_ref, mod_ref, wpw1_ref, bpw1_ref, fwd_ref, inv_ref, gspec_ref, bdw_ref, cg_ref, cb_ref,
                       wpw2_ref, bpw2_ref, lng_ref, lnb_ref, o_ref, *, alpha, seg):
    c = wpw2_ref.shape[0]
    rows = fwd_ref.shape[1]
    nfp = gspec_ref.shape[0] // 2
    t = h_ref.shape[1]
    h = h_ref[0]
    sh, sc, g1 = mod_ref[0, 0:1, :], mod_ref[0, 1:2, :], mod_ref[0, 2:3, :]
    u = (h * (1.0 + sc) + sh).astype(BF16)

    def glu_slab(p):
        lo = slice(p * V7X_MXU_DIM, (p + 1) * V7X_MXU_DIM)
        hi = slice(c + p * V7X_MXU_DIM, c + (p + 1) * V7X_MXU_DIM)
        a = _dot(u, wpw1_ref[:, lo]) + bpw1_ref[:, lo]
        gate = _dot(u, wpw1_ref[:, hi]) + bpw1_ref[:, hi]
        return (a * jax.nn.sigmoid(gate)).astype(BF16)

    def conv_slab(p, glu):
        lo = slice(p * V7X_MXU_DIM, (p + 1) * V7X_MXU_DIM)
        group = rows // seg
        g_re = jnp.concatenate([gspec_ref[0:nfp, lo]] * group, axis=0)
        g_im = jnp.concatenate([gspec_ref[nfp:, lo]] * group, axis=0)
        out = []
        for r0 in range(0, t, rows):
            spec = _dot(fwd_ref[...], glu[r0:r0 + rows])
            a_re, a_im = spec[0:group * nfp], spec[group * nfp:]
            prod = jnp.concatenate([a_re * g_re - a_im * g_im, a_re * g_im + a_im * g_re], axis=0)
            out.append(_dot(inv_ref[...], prod.astype(BF16)))
        return jnp.concatenate(out, axis=0) if len(out) > 1 else out[0]

    nslab = c // V7X_MXU_DIM
    slabs = []
    glu = glu_slab(0)
    for p in range(nslab):
        nxt = glu_slab(p + 1) if p + 1 < nslab else None
        slabs.append(conv_slab(p, glu))
        glu = nxt
    conv = jnp.concatenate(slabs, axis=1) + bdw_ref[...]
    act = _silu(_layer_norm(conv, cg_ref[...], cb_ref[...])).astype(BF16)
    y = _dot(act, wpw2_ref[...]) + bpw2_ref[...]
    o_ref[0] = _layer_norm(alpha * h + g1 * y, lng_ref[...], lnb_ref[...])


def _conv_mixer_layer(h, seg, mod, p, alpha):
    bsz, s, d = h.shape
    t = _token_tile(s, WIDE_TOKEN_TILE)
    assert t % seg == 0 and seg % V7X_BF16_ROWS == 0
    c = p["w_pw2"].shape[0]
    fwd, inv, spec_tab = _dft_tables(seg, p["w_dw"].shape[0])
    assert t % fwd.shape[1] == 0 and c % V7X_MXU_DIM == 0
    gspec = _filter_spectrum(spec_tab, p["w_dw"])
    return pl.pallas_call(
        functools.partial(_conv_mixer_kernel, alpha=alpha, seg=seg),
        grid=(bsz, s // t),
        in_specs=[pl.BlockSpec((1, t, d), lambda b, j: (b, j, 0)),
                  pl.BlockSpec((1, MOD_ROWS, d), lambda b, j: (b, 0, 0)),
                  _resident((d, 2 * c)), _resident((1, 2 * c)),
                  _resident(fwd.shape), _resident(inv.shape), _resident(gspec.shape), _resident((1, c)),
                  _resident((1, c)), _resident((1, c)),
                  _resident((c, d)), _resident((1, d)), _resident((1, d)), _resident((1, d))],
        out_specs=pl.BlockSpec((1, t, d), lambda b, j: (b, j, 0)),
        out_shape=jax.ShapeDtypeStruct((bsz, s, d), F32),
        compiler_params=_params("parallel", "parallel"),
        name="conv_mixer_layer",
    )(h, mod, p["w_pw1"].astype(BF16), _row(p["b_pw1"]), jnp.asarray(fwd, BF16), jnp.asarray(inv, BF16),
      gspec, _row(p["b_dw"]), _row(p["ln_g"]), _row(p["ln_b"]), p["w_pw2"].astype(BF16),
      _row(p["b_pw2"]), _row(p["ln1_g"]), _row(p["ln1_b"]))


def _mlp_kernel(h_ref, mod_ref, w1_ref, w2_ref, g_ref, b_ref, o_ref, *, alpha):
    h = h_ref[0]
    sh, sc, g2 = mod_ref[0, 3:4, :], mod_ref[0, 4:5, :], mod_ref[0, 5:6, :]
    u = (h * (1.0 + sc) + sh).astype(BF16)
    ff = w1_ref.shape[1]
    acc = jnp.zeros(h.shape, F32)
    for f0 in range(0, ff, MLP_FF_CHUNK):
        hid = jnp.maximum(_dot(u, w1_ref[:, f0:f0 + MLP_FF_CHUNK]), 0.0)
        acc = acc + _dot((hid * hid).astype(BF16), w2_ref[f0:f0 + MLP_FF_CHUNK, :])
    o_ref[0] = _layer_norm(alpha * h + g2 * acc, g_ref[...], b_ref[...])


def _mlp_layer(h, mod, w1, w2, ln_g, ln_b, alpha):
    bsz, s, d = h.shape
    t = _token_tile(s, WIDE_TOKEN_TILE)
    ff = w1.shape[1]
    assert ff % MLP_FF_CHUNK == 0
    return pl.pallas_call(
        functools.partial(_mlp_kernel, alpha=alpha),
        grid=(bsz, s // t),
        in_specs=[pl.BlockSpec((1, t, d), lambda b, j: (b, j, 0)),
                  pl.BlockSpec((1, MOD_ROWS, d), lambda b, j: (b, 0, 0)),
                  _resident((d, ff)), _resident((ff, d)), _resident((1, d)), _resident((1, d))],
        out_specs=pl.BlockSpec((1, t, d), lambda b, j: (b, j, 0)),
        out_shape=jax.ShapeDtypeStruct((bsz, s, d), F32),
        compiler_params=_params("parallel", "parallel"),
        name="mlp_layer",
    )(h, mod, w1.astype(BF16), w2.astype(BF16), _row(ln_g), _row(ln_b))


def _mlstm_up_kernel(h_ref, prev_ref, next_ref, mod_ref, wup_ref, wconv_ref, bconv_ref,
                     xm_ref, xc_ref, z_ref, u_ref, xm_all_ref):
    t = h_ref.shape[1]
    e = xm_ref.shape[2]
    halo = prev_ref.shape[1]
    slab = 2 * V7X_MXU_DIM
    per_slab = slab // V7X_LANES
    left = (QKV_CONV_WIDTH - 1) // 2
    j = pl.program_id(1)
    sh, sc = mod_ref[0, 0:1, :], mod_ref[0, 1:2, :]
    modulate = lambda v: (v * (1.0 + sc) + sh).astype(BF16)
    u_ref[0:halo, :] = modulate(prev_ref[0])
    u_ref[halo:halo + t, :] = modulate(h_ref[0])
    u_ref[halo + t:, :] = modulate(next_ref[0])
    keep_prev = jnp.where(j == 0, 0.0, 1.0)
    keep_next = jnp.where(j == pl.num_programs(1) - 1, 0.0, 1.0)

    def xm_matmul(p):
        return _dot(u_ref[...], wup_ref[:, p * slab:(p + 1) * slab])

    def xm_store(p, xm_all):
        cols = slice(p * slab, (p + 1) * slab)
        xm_ref[0, :, cols] = xm_all[halo:halo + t].astype(BF16)
        for q in range(per_slab):
            cb = per_slab * p + q
            lanes = slice(q * V7X_LANES, (q + 1) * V7X_LANES)
            xm_all_ref[cb, 0:halo, :] = xm_all[0:halo, lanes] * keep_prev
            xm_all_ref[cb, halo:halo + t, :] = xm_all[halo:halo + t, lanes]
            xm_all_ref[cb, halo + t:, :] = xm_all[halo + t:, lanes] * keep_next

    def conv_block(cb):
        lanes = slice(cb * V7X_LANES, (cb + 1) * V7X_LANES)
        for r0 in range(0, t, CONV_ROW_BLOCK):
            acc = jnp.zeros((CONV_ROW_BLOCK, V7X_LANES), F32)
            for k in range(QKV_CONV_WIDTH):
                start = halo + r0 - left + k
                acc = acc + xm_all_ref[cb, start:start + CONV_ROW_BLOCK, :] * wconv_ref[k:k + 1, lanes]
            xc_ref[0, r0:r0 + CONV_ROW_BLOCK, lanes] = _silu(acc + bconv_ref[:, lanes]).astype(BF16)

    nslab = e // slab
    for p in range(nslab):
        xm_store(p, xm_matmul(p))
    for p in range(nslab):
        cols = slice(e + p * slab, e + (p + 1) * slab)
        z_ref[0, :, p * slab:(p + 1) * slab] = _dot(u_ref[halo:halo + t, :], wup_ref[:, cols]).astype(BF16)
        for q in range(per_slab):
            conv_block(per_slab * p + q)


def _mlstm_up(h, mod, w_up, w_conv, b_conv):
    bsz, s, d = h.shape
    e = w_up.shape[1] // 2
    t = _token_tile(s, WIDE_TOKEN_TILE)
    halo = V7X_BF16_ROWS
    assert t % halo == 0 and t % CONV_ROW_BLOCK == 0
    nhb = s // halo
    taps = w_conv.shape[0]
    w_conv = jnp.pad(w_conv, ((0, V7X_SUBLANES - taps), (0, 0)))
    tile = pl.BlockSpec((1, t, e), lambda b, j: (b, j, 0))
    out = jax.ShapeDtypeStruct((bsz, s, e), BF16)
    return pl.pallas_call(
        _mlstm_up_kernel,
        grid=(bsz, s // t),
        in_specs=[pl.BlockSpec((1, t, d), lambda b, j: (b, j, 0)),
                  pl.BlockSpec((1, halo, d), lambda b, j: (b, jnp.maximum(j * (t // halo) - 1, 0), 0)),
                  pl.BlockSpec((1, halo, d), lambda b, j: (b, jnp.minimum((j + 1) * (t // halo), nhb - 1), 0)),
                  pl.BlockSpec((1, MOD_ROWS, d), lambda b, j: (b, 0, 0)),
                  _resident((d, 2 * e)), _resident((V7X_SUBLANES, e)), _resident((1, e))],
        out_specs=[tile, tile, tile],
        out_shape=[out, out, out],
        scratch_shapes=[pltpu.VMEM((t + 2 * halo, d), BF16),
                        pltpu.VMEM((e // V7X_LANES, t + 2 * halo, V7X_LANES), F32)],
        compiler_params=_params("parallel", "parallel"),
        name="mlstm_up",
    )(h, h, h, mod, w_up.astype(BF16), w_conv, _row(b_conv))


def _log_sigmoid(x):
    return jnp.minimum(x, 0.0) - jnp.log1p(jnp.exp(-jnp.abs(x)))


def _split_bf16x3(x):
    hi = x.astype(BF16)
    r1 = x - hi.astype(F32)
    mid = r1.astype(BF16)
    lo = (r1 - mid.astype(F32)).astype(BF16)
    return jnp.concatenate([hi, mid, lo], axis=1)


def _mlstm_qkv_kernel(*refs, k_scale, emit_intra):
    refs = list(refs)
    xc_ref, xm_ref, wq_ref, wk_ref, wv_ref, wg_ref, bg_ref, q_ref, kt_ref, v_ref, g_ref = refs[:11]
    h0_ref = refs[11] if emit_intra else None
    e = wq_ref.shape[0]
    t = xc_ref.shape[1]
    nh = MLSTM_HEADS
    dh = e // nh
    half = V7X_LANES // 2
    xc = xc_ref[0]
    q = _dot(xc, wq_ref[...]).astype(BF16)
    q_ref[0] = q
    gates = _dot(q, wg_ref[0:e, :])
    k = _dot(xc, wk_ref[...])
    kt_ref[0, 0] = (k * k_scale).T.astype(BF16)
    gates = gates + _dot(k.astype(BF16), wg_ref[e:2 * e, :])
    v = _dot(xm_ref[0], wv_ref[...]).astype(BF16)
    v_ref[0] = v
    gates = gates + _dot(v, wg_ref[2 * e:, :]) + bg_ref[...]

    lane = lax.broadcasted_iota(jnp.int32, (1, V7X_LANES), 1)
    group = (lane % GATE_PACK_LANES) // nh
    rows = lax.broadcasted_iota(jnp.int32, (t, t), 0)
    cols = lax.broadcasted_iota(jnp.int32, (t, t), 1)
    visible = (rows >= cols, rows <= cols)
    lf = _log_sigmoid(gates)
    parts = _split_bf16x3(lf)
    fold = lambda c3: c3[:, :V7X_LANES] + c3[:, V7X_LANES:2 * V7X_LANES] + c3[:, 2 * V7X_LANES:]
    prefix = fold(_dot(jnp.where(visible[0], 1.0, 0.0).astype(BF16), parts))
    suffix = fold(_dot(jnp.where(visible[1], 1.0, 0.0).astype(BF16), parts))
    b = jnp.where(lane < GATE_PACK_LANES, prefix, suffix)
    a = pltpu.roll(gates, half, 1) - b
    total = jnp.sum(lf, axis=0, keepdims=True)
    amax = jnp.max(a, axis=0, keepdims=True)
    pack = jnp.where(group == 0, b, jnp.where(group == 2, a, jnp.where(group == 4, total,
                                                                         jnp.where(group == 5, amax, 0.0))))
    if emit_intra:
        a_rows = a.T
        qk_of = lambda hd: _dot(q[:, hd * dh:(hd + 1) * dh], kt_ref[0, 0, hd * dh:(hd + 1) * dh, :])
        qk_next = qk_of(0)
        for hd in range(nh):
            lanes = slice(hd * dh, (hd + 1) * dh)
            qk = qk_next
            if hd + 1 < nh:
                qk_next = qk_of(hd + 1)
            for d in range(2):
                base = d * GATE_PACK_LANES + hd
                a_row = a_rows[base + 2 * nh:base + 2 * nh + 1, :]
                masked = jnp.where(visible[d], a_row, NEG_BIG)
                cmax = jnp.max(masked, axis=1, keepdims=True)
                scores = qk * jnp.exp(masked - cmax)
                rowsum = jnp.sum(scores, axis=1, keepdims=True)
                h0_ref[d, 0, :, lanes] = _dot(scores.astype(BF16), v[:, lanes]).astype(BF16)
                pack = jnp.where(lane == base + nh, cmax, jnp.where(lane == base + 3 * nh, rowsum, pack))
    g_ref[0] = pack[:, :g_ref.shape[2]]


def _gate_lane_layout():
    nh = MLSTM_HEADS
    src = np.full((V7X_LANES,), -1, np.int32)
    for d in range(2):
        for g in range(GATE_PACK_LANES // nh):
            for h in range(nh):
                lane = d * GATE_PACK_LANES + g * nh + h
                if g in (0, 2, 4, 5):
                    src[lane] = (2 * d + 1) * nh + h
                if g in (2, 5):
                    src[V7X_LANES // 2 + lane] = 2 * d * nh + h
    return src


def _mlstm_qkv(xc, xm, w_q, w_k, w_v, w_gate, b_gate, emit_intra):
    bsz, s, e = xc.shape
    assert 2 * GATE_PACK_LANES == V7X_LANES // 2
    t = _token_tile(s, SCAN_CHUNK)
    k_scale = float(e // MLSTM_HEADS) ** -0.5
    src = _gate_lane_layout()
    lay = lambda w: jnp.where(src >= 0, w[:, np.maximum(src, 0)], 0.0)
    tile = pl.BlockSpec((1, t, e), lambda b, j: (b, j, 0))
    out = jax.ShapeDtypeStruct((bsz, s, e), BF16)
    out_specs = [tile, pl.BlockSpec((1, 1, e, t), lambda b, j: (b, j, 0, 0)), tile,
                 pl.BlockSpec((1, t, 2 * GATE_PACK_LANES), lambda b, j: (b, j, 0))]
    out_shape = [out, jax.ShapeDtypeStruct((bsz, s // t, e, t), BF16), out,
                 jax.ShapeDtypeStruct((bsz, s, 2 * GATE_PACK_LANES), F32)]
    if emit_intra:
        out_specs.append(pl.BlockSpec((2, 1, t, e), lambda b, j: (0, b, j, 0)))
        out_shape.append(jax.ShapeDtypeStruct((2, bsz, s, e), BF16))
    outs = pl.pallas_call(
        functools.partial(_mlstm_qkv_kernel, k_scale=k_scale, emit_intra=emit_intra),
        grid=(bsz, s // t),
        in_specs=[tile, tile, _resident((e, e)), _resident((e, e)), _resident((e, e)),
                  _resident((3 * e, V7X_LANES)), _resident((1, V7X_LANES))],
        out_specs=out_specs,
        out_shape=out_shape,
        compiler_params=_params("parallel", "parallel"),
        name="mlstm_qkv",
    )(xc, xm, w_q.astype(BF16), w_k.astype(BF16), w_v.astype(BF16), lay(w_gate).astype(BF16),
      lay(_row(b_gate)))
    return tuple(outs) if emit_intra else tuple(outs) + (None,)


def _mlstm_scan_kernel(*refs, has_init, emit_h, emit_state):
    refs = list(refs)
    q_ref, kt_ref, v_ref, g_ref, gt_ref = refs[:5]
    del refs[:5]
    if has_init:
        c0_ref, n0_ref, m0_ref = refs[:3]
        del refs[:3]
    if emit_h:
        o_ref, rr_ref = refs[:2]
        del refs[:2]
    if emit_state:
        cf_ref, nf_ref, mf_ref = refs[:3]
        del refs[:3]
    c_ref, n_ref, m_ref = refs

    length = kt_ref.shape[3]
    chunks = kt_ref.shape[1]
    nh, dh = c_ref.shape[0], c_ref.shape[1]
    backward = pl.program_id(1) == 1
    step = pl.program_id(2)

    @pl.when(step == 0)
    def _():
        if has_init:
            c_ref[...] = c0_ref[0, 0]
            n_ref[...] = n0_ref[0, 0]
            m_ref[...] = m0_ref[0, 0]
        else:
            c_ref[...] = jnp.zeros(c_ref.shape, F32)
            n_ref[...] = jnp.zeros(n_ref.shape, F32)
            m_ref[...] = jnp.zeros(m_ref.shape, F32)

    ones_rows = jnp.ones((V7X_SUBLANES, length), BF16)
    contract_lanes = (((1,), (1,)), ((), ()))

    def one_chunk(i, carry):
        sub = jnp.where(backward, chunks - 1 - i, i)
        rows = pl.ds(pl.multiple_of(sub * length, length), length)
        g = g_ref[0, 0, rows, :]
        gt = gt_ref[0, 0, sub]
        grp = lambda i: g[:, i * nh:(i + 1) * nh]
        grp_t = lambda i: gt[i * nh:(i + 1) * nh, :]
        m_old = m_ref[nh:nh + 1, 0:nh]
        m_old_t = m_ref[0:nh, 0:1]
        total, amax = grp(4)[0:1], grp(5)[0:1]
        m_new = total + jnp.maximum(m_old, amax)
        decay = jnp.exp(total + m_old - m_new)
        m_new_t = grp_t(4) + jnp.maximum(m_old_t, grp_t(5))
        w_k = jnp.exp(grp_t(4) + grp_t(2) - m_new_t)
        if emit_h:
            b_t, cmax, rowsum = grp(0), grp(1), grp(3)
            m_row = jnp.maximum(m_old, cmax)
            r_intra = jnp.exp(cmax - m_row)
            w_inter = jnp.exp(m_old - m_row)
            den_floor = jnp.exp(-(b_t + m_row))

            inverses = []
            for h in range(nh):
                lanes = slice(h * dh, (h + 1) * dh)
                col = slice(h, h + 1)
                q = q_ref[0, rows, lanes]
                q_n = lax.dot_general(q, n_ref[h].astype(BF16), contract_lanes,
                                      preferred_element_type=F32)[:, 0:1]
                o_ref[0, 0, rows, lanes] = (w_inter[:, col] * _dot(q, c_ref[h].astype(BF16))).astype(o_ref.dtype)
                den = r_intra[:, col] * rowsum[:, col] + w_inter[:, col] * q_n
                inverses.append(1.0 / jnp.maximum(jnp.abs(den), den_floor[:, col]))
            rr_ref[0, 0, rows, :] = jnp.concatenate([r_intra] + inverses, axis=1)

        for h in range(nh):
            lanes = slice(h * dh, (h + 1) * dh)
            col = slice(h, h + 1)
            wkt = (kt_ref[0, sub, lanes, :].astype(F32) * w_k[h:h + 1, :]).astype(BF16)
            c_ref[h] = decay[:, col] * c_ref[h] + _dot(wkt, v_ref[0, rows, lanes])
            n_ref[h] = decay[:, col] * n_ref[h] + lax.dot_general(ones_rows, wkt, contract_lanes,
                                                                  preferred_element_type=F32)
        m_ref[0:nh, :] = m_new_t[:, 0:V7X_LANES]
        m_ref[nh:nh + 1, 0:nh] = m_new
        return carry

    lax.fori_loop(0, chunks, one_chunk, 0)

    if emit_state:
        @pl.when(step == pl.num_programs(2) - 1)
        def _():
            cf_ref[0, 0] = c_ref[...]
            nf_ref[0, 0] = n_ref[...]
            mf_ref[0, 0] = m_ref[...]


def _mlstm_scan(q, kt, v, gates, init_state, emit_h, emit_state):
    bsz, s, e = q.shape
    nh = MLSTM_HEADS
    dh = e // nh
    length = _token_tile(s, SCAN_CHUNK)
    nch = s // length
    per_step = SCAN_CHUNKS_PER_STEP if nch % SCAN_CHUNKS_PER_STEP == 0 else 1
    nst = nch // per_step
    span = per_step * length
    g = gates.reshape(bsz, s, 2, GATE_PACK_LANES).transpose(0, 2, 1, 3)
    g_t = g.reshape(bsz, 2, nch, length, GATE_PACK_LANES).transpose(0, 1, 2, 4, 3)
    chunk = lambda d, st: jnp.where(d == 0, st, nst - 1 - st)
    tile = pl.BlockSpec((1, span, e), lambda b, d, st: (b, chunk(d, st), 0))
    dir_tile = pl.BlockSpec((1, 1, span, e), lambda b, d, st: (d, b, chunk(d, st), 0))
    state_shapes = [(nh, dh, dh), (nh, V7X_SUBLANES, dh), (V7X_SUBLANES, V7X_LANES)]
    state_specs = [pl.BlockSpec((1, 1) + shp, lambda b, d, st, n=len(shp): (b, d) + (0,) * n)
                   for shp in state_shapes]
    scratch = [pltpu.VMEM(shp, F32) for shp in state_shapes]

    in_specs = [tile, pl.BlockSpec((1, per_step, e, length), lambda b, d, st: (b, chunk(d, st), 0, 0)), tile,
                pl.BlockSpec((1, 1, span, GATE_PACK_LANES), lambda b, d, st: (b, d, chunk(d, st), 0)),
                pl.BlockSpec((1, 1, per_step, GATE_PACK_LANES, length),
                             lambda b, d, st: (b, d, chunk(d, st), 0, 0))]
    args = [q, kt, v, g, g_t]
    if init_state is not None:
        in_specs += state_specs
        args += list(init_state)
    out_specs, out_shape = [], []
    if emit_h:
        out_specs += [dir_tile, pl.BlockSpec((1, 1, span, 2 * nh), lambda b, d, st: (d, b, chunk(d, st), 0))]
        out_shape += [jax.ShapeDtypeStruct((2, bsz, s, e), BF16), jax.ShapeDtypeStruct((2, bsz, s, 2 * nh), F32)]
    if emit_state:
        out_specs += state_specs
        out_shape += [jax.ShapeDtypeStruct((bsz, 2) + shp, F32) for shp in state_shapes]
    outs = pl.pallas_call(
        functools.partial(_mlstm_scan_kernel, has_init=init_state is not None, emit_h=emit_h,
                          emit_state=emit_state),
        grid=(bsz, 2, nst),
        in_specs=in_specs,
        out_specs=out_specs,
        out_shape=out_shape,
        scratch_shapes=scratch,
        compiler_params=_params("parallel", "parallel", "arbitrary"),
        name="mlstm_scan",
    )(*args)
    outs = list(outs)
    h = (outs.pop(0), outs.pop(0)) if emit_h else None
    return h, (tuple(outs) if emit_state else None)


def _mlstm_out_kernel(h_ref, xm_ref, xc_ref, z_ref, h0_ref, hc_ref, rr_ref, mod_ref, wo_ref, bo_ref, gn_ref,
                      skip_ref, wdown_ref, lng_ref, lnb_ref, o_ref, *, alpha):
    e = xm_ref.shape[2]
    nh = MLSTM_HEADS
    dh = e // MLSTM_HEADS
    sub_rows = NARROW_TOKEN_TILE
    g1 = mod_ref[0, 2:3, :]

    def sub_tile(i, carry):
        rows = pl.ds(pl.multiple_of(i * sub_rows, sub_rows), sub_rows)
        h = h_ref[0, rows, :]
        xm = xm_ref[0, rows, :]
        acc = jnp.zeros(h.shape, F32)

        def gate_logits(hd):
            lanes = slice(hd * dh, (hd + 1) * dh)
            lanes_b = slice(e + hd * dh, e + (hd + 1) * dh)
            return (_dot(xm, wo_ref[:, lanes]) + bo_ref[:, lanes],
                    _dot(xm, wo_ref[:, lanes_b]) + bo_ref[:, lanes_b])

        logits = gate_logits(0)
        for hd in range(MLSTM_HEADS):
            lanes = slice(hd * dh, (hd + 1) * dh)
            o_f, o_b = jax.nn.sigmoid(logits[0]), jax.nn.sigmoid(logits[1])
            if hd + 1 < MLSTM_HEADS:
                logits = gate_logits(hd + 1)
            scan_h = lambda d: ((rr_ref[d, 0, rows, hd:hd + 1] * h0_ref[d, 0, rows, lanes].astype(F32)
                                 + hc_ref[d, 0, rows, lanes].astype(F32)) * rr_ref[d, 0, rows, nh + hd:nh + hd + 1])
            hsum = o_f * scan_h(0) + o_b * scan_h(1)
            mu = jnp.mean(hsum, axis=-1, keepdims=True)
            dlt = hsum - mu
            var = jnp.mean(dlt * dlt, axis=-1, keepdims=True)
            hn = dlt * lax.rsqrt(var + LN_EPS) * gn_ref[:, lanes]
            y = ((hn + skip_ref[:, lanes] * xc_ref[0, rows, lanes].astype(F32))
                 * _silu(z_ref[0, rows, lanes].astype(F32)))
            acc = acc + _dot(y.astype(BF16), wdown_ref[lanes, :])
        o_ref[0, rows, :] = _layer_norm(alpha * h + g1 * acc, lng_ref[...], lnb_ref[...])
        return carry

    lax.fori_loop(0, h_ref.shape[1] // sub_rows, sub_tile, 0)


def _mlstm_out(h, xm, xc, z, h0, hc, rr, mod, w_o, b_o, gn_g, skip, w_down, ln_g, ln_b, alpha):
    bsz, s, d = h.shape
    e = xm.shape[2]
    t = _token_tile(s, NARROW_TOKEN_TILE)
    tile = lambda width: pl.BlockSpec((1, t, width), lambda b, j: (b, j, 0))
    both = lambda width: pl.BlockSpec((2, 1, t, width), lambda b, j: (0, b, j, 0))
    return pl.pallas_call(
        functools.partial(_mlstm_out_kernel, alpha=alpha),
        grid=(bsz, s // t),
        in_specs=[tile(d), tile(e), tile(e), tile(e), both(e), both(e), both(rr.shape[3]),
                  pl.BlockSpec((1, MOD_ROWS, d), lambda b, j: (b, 0, 0)),
                  _resident((e, 2 * e)), _resident((1, 2 * e)), _resident((1, e)), _resident((1, e)),
                  _resident((e, d)), _resident((1, d)), _resident((1, d))],
        out_specs=pl.BlockSpec((1, t, d), lambda b, j: (b, j, 0)),
        out_shape=jax.ShapeDtypeStruct((bsz, s, d), F32),
        compiler_params=_params("parallel", "parallel"),
        name="mlstm_out",
    )(h, xm, xc, z, h0, hc, rr, mod, w_o.astype(BF16), _row(b_o), _row(gn_g), _row(skip),
      w_down.astype(BF16), _row(ln_g), _row(ln_b))


def _mod_tables(ada_out_i, bsz, d):
    m = ada_out_i.reshape(MOD_ROWS, 6, d)
    pad = ((0, 0), (0, MOD_ROWS - 6), (0, 0))
    lat = jnp.pad(m[:bsz], pad)
    ctx = jnp.pad(jnp.broadcast_to(m[bsz][None], (bsz, 6, d)), pad)
    return lat, ctx


def kernel(x, c, ctx, c_ctx, ada_w, ada_b, ln1_g, ln1_b, ln2_g, ln2_b, mlp_w1, mlp_w2, conv_w_pw1, conv_b_pw1, conv_w_dw, conv_b_dw, conv_ln_g, conv_ln_b, conv_w_pw2, conv_b_pw2, ml_w_up, ml_w_conv, ml_b_conv, ml_w_q, ml_w_k, ml_w_v, ml_w_gate, ml_b_gate, ml_w_o, ml_b_o, ml_gn_g, ml_skip, ml_w_down):
    bsz, s, d = x.shape
    depth = ada_w.shape[0]
    assert depth == 2 and bsz + 1 <= MOD_ROWS, "built for the two-layer block: conv mixer, then mLSTM"
    alpha = (2 * depth) ** 0.25

    cvec = jnp.concatenate([c, c_ctx[None], jnp.zeros((MOD_ROWS - bsz - 1, d), F32)], axis=0)
    ada = _adaln(cvec, ada_w, ada_b)
    mod0, mod0_ctx = _mod_tables(ada[0], bsz, d)
    mod1, mod1_ctx = _mod_tables(ada[1], bsz, d)

    conv_p = dict(w_pw1=conv_w_pw1[0], b_pw1=conv_b_pw1[0], w_dw=conv_w_dw[0], b_dw=conv_b_dw[0],
                  ln_g=conv_ln_g[0], ln_b=conv_ln_b[0], w_pw2=conv_w_pw2[0], b_pw2=conv_b_pw2[0],
                  ln1_g=ln1_g[0], ln1_b=ln1_b[0])
    h = _conv_mixer_layer(x, GRID_W, mod0, conv_p, alpha)
    h = _mlp_layer(h, mod0, mlp_w1[0], mlp_w2[0], ln2_g[0], ln2_b[0], alpha)
    hc = _conv_mixer_layer(ctx, ctx.shape[1], mod0_ctx, conv_p, alpha)
    hc = _mlp_layer(hc, mod0_ctx, mlp_w1[0], mlp_w2[0], ln2_g[0], ln2_b[0], alpha)

    feats = lambda hh, mm: _mlstm_up(hh, mm, ml_w_up[0], ml_w_conv[0], ml_b_conv[0])
    qkv = lambda xc_, xm_, intra: _mlstm_qkv(xc_, xm_, ml_w_q[0], ml_w_k[0], ml_w_v[0], ml_w_gate[0],
                                             ml_b_gate[0], emit_intra=intra)
    xm_c, xc_c, _ = feats(hc, mod1_ctx)
    _, ctx_state = _mlstm_scan(*qkv(xc_c, xm_c, False)[:4], init_state=None, emit_h=False, emit_state=True)
    xm, xc, z = feats(h, mod1)
    q, kt, v, gates, h0 = qkv(xc, xm, True)
    (hc, rr), _ = _mlstm_scan(q, kt, v, gates, init_state=ctx_state, emit_h=True, emit_state=False)
    h = _mlstm_out(h, xm, xc, z, h0, hc, rr, mod1, ml_w_o[0], ml_b_o[0], ml_gn_g[0], ml_skip[0],
                   ml_w_down[0], ln1_g[1], ln1_b[1], alpha)
    return _mlp_layer(h, mod1, mlp_w1[1], mlp_w2[1], ln2_g[1], ln2_b[1], alpha)
```

```python
import functools

import jax
import jax.numpy as jnp
import numpy as np
from jax import lax
from jax.experimental import pallas as pl
from jax.experimental.pallas import tpu as pltpu

F32 = jnp.float32
BF16 = jnp.bfloat16

GRID_W = 64
QKV_CONV_WIDTH = 5
MLSTM_HEADS = 4
GATE_PACK_LANES = 8 * MLSTM_HEADS
LN_EPS = 1e-5

V7X_LANES = 128
V7X_SUBLANES = 8
V7X_BF16_ROWS = 16
V7X_MXU_DIM = 256
V7X_VMEM_LIMIT_BYTES = 56 * 1024 * 1024

WIDE_TOKEN_TILE = 512
NARROW_TOKEN_TILE = 256
SCAN_CHUNK = 256
SCAN_CHUNKS_PER_STEP = 4
MLP_FF_CHUNK = 1024
ADALN_COLUMN_BLOCKS = 4
CONV_ROW_BLOCK = 64
MOD_ROWS = 8
NEG_BIG = -1e30


def _resident(shape):
    zeros = (0,) * len(shape)
    return pl.BlockSpec(shape, lambda *_: zeros, pipeline_mode=pl.Buffered(1))


def _params(*semantics):
    return pltpu.CompilerParams(dimension_semantics=semantics,
                                vmem_limit_bytes=V7X_VMEM_LIMIT_BYTES)


def _token_tile(s, preferred):
    t = min(s, preferred)
    assert s % t == 0
    return t


def _layer_norm(r, g, b):
    mu = jnp.mean(r, axis=-1, keepdims=True)
    d = r - mu
    var = jnp.mean(d * d, axis=-1, keepdims=True)
    return d * lax.rsqrt(var + LN_EPS) * g + b


def _silu(x):
    return x * jax.nn.sigmoid(x)


def _dot(a, b):
    return jnp.dot(a, b, preferred_element_type=F32)


def _row(v):
    return v.reshape(1, -1)


def _adaln_kernel(c_ref, w_ref, b_ref, o_ref):
    o_ref[0] = _dot(_silu(c_ref[...]), w_ref[0]) + b_ref[0]


def _adaln(cvec, ada_w, ada_b):
    depth, d, n = ada_w.shape
    assert n % (ADALN_COLUMN_BLOCKS * V7X_LANES) == 0
    tn = n // ADALN_COLUMN_BLOCKS
    return pl.pallas_call(
        _adaln_kernel,
        grid=(depth, n // tn),
        in_specs=[pl.BlockSpec((MOD_ROWS, d), lambda i, j: (0, 0)),
                  pl.BlockSpec((1, d, tn), lambda i, j: (i, 0, j)),
                  pl.BlockSpec((1, 1, tn), lambda i, j: (i, 0, j))],
        out_specs=pl.BlockSpec((1, MOD_ROWS, tn), lambda i, j: (i, 0, j)),
        out_shape=jax.ShapeDtypeStruct((depth, MOD_ROWS, n), F32),
        compiler_params=_params("parallel", "parallel"),
        name="adaln",
    )(cvec, ada_w, ada_b.reshape(depth, 1, n))


def _dft_tables(seg, taps):
    left = (taps - 1) // 2
    nfft = 1 << (seg + taps - 2).bit_length()
    nf = nfft // 2 + 1
    nfp = -(-nf // V7X_SUBLANES) * V7X_SUBLANES
    taps_padded = -(-taps // V7X_SUBLANES) * V7X_SUBLANES
    f = np.arange(nf, dtype=np.float64)[:, None]
    ang = 2.0 * np.pi * f * np.arange(seg)[None, :] / nfft
    fwd = np.zeros((2 * nfp, seg))
    fwd[:nf], fwd[nfp:nfp + nf] = np.cos(ang), -np.sin(ang)
    weight = np.where((f == 0) | (f == nfft // 2), 1.0, 2.0) / nfft
    inv = np.zeros((seg, 2 * nfp))
    inv[:, :nf], inv[:, nfp:nfp + nf] = (weight * np.cos(ang)).T, (-weight * np.sin(ang)).T
    ang_w = 2.0 * np.pi * f * (left - np.arange(taps))[None, :] / nfft
    spec = np.zeros((2 * nfp, taps_padded))
    spec[:nf, :taps], spec[nfp:nfp + nf, :taps] = np.cos(ang_w), -np.sin(ang_w)
    group = max(1, V7X_MXU_DIM // seg)
    eye = np.eye(group)
    fwd = np.concatenate([np.kron(eye, fwd[:nfp]), np.kron(eye, fwd[nfp:])], axis=0)
    inv = np.concatenate([np.kron(eye, inv[:, :nfp]), np.kron(eye, inv[:, nfp:])], axis=1)
    return fwd.astype(np.float32), inv.astype(np.float32), spec.astype(np.float32)


def _filter_spectrum_kernel(tab_ref, w_ref, o_ref):
    o_ref[...] = jnp.dot(tab_ref[...], w_ref[...], preferred_element_type=F32,
                         precision=lax.Precision.HIGHEST)


def _filter_spectrum(spec_tab, w_dw):
    rows, taps_padded = spec_tab.shape
    c = w_dw.shape[1]
    w = jnp.pad(w_dw, ((0, taps_padded - w_dw.shape[0]), (0, 0)))
    return pl.pallas_call(
        _filter_spectrum_kernel,
        out_shape=jax.ShapeDtypeStruct((rows, c), F32),
        name="conv_filter_spectrum",
    )(jnp.asarray(spec_tab), w)


def _conv_mixer_kernel(h_ref, mod_ref, wpw1_ref, bpw1_ref, fwd_ref, inv_ref, gspec_ref, bdw_ref, cg_ref, cb_ref,
                       wpw2_ref, bpw2_ref, lng_ref, lnb_ref, o_ref, *, alpha, seg):
    c = wpw2_ref.shape[0]
    rows = fwd_ref.shape[1]
    nfp = gspec_ref.shape[0] // 2
    t = h_ref.shape[1]
    h = h_ref[0]
    sh, sc, g1 = mod_ref[0, 0:1, :], mod_ref[0, 1:2, :], mod_ref[0, 2:3, :]
    u = (h * (1.0 + sc) + sh).astype(BF16)

    def glu_slab(p):
        lo = slice(p * V7X_MXU_DIM, (p + 1) * V7X_MXU_DIM)
        hi = slice(c + p * V7X_MXU_DIM, c + (p + 1) * V7X_MXU_DIM)
        a = _dot(u, wpw1_ref[:, lo]) + bpw1_ref[:, lo]
        gate = _dot(u, wpw1_ref[:, hi]) + bpw1_ref[:, hi]
        return (a * jax.nn.sigmoid(gate)).astype(BF16)

    def conv_slab(p, glu):
        lo = slice(p * V7X_MXU_DIM, (p + 1) * V7X_MXU_DIM)
        group = rows // seg
        g_re = jnp.concatenate([gspec_ref[0:nfp, lo]] * group, axis=0)
        g_im = jnp.concatenate([gspec_ref[nfp:, lo]] * group, axis=0)
        out = []
        for r0 in range(0, t, rows):
            spec = _dot(fwd_ref[...], glu[r0:r0 + rows])
            a_re, a_im = spec[0:group * nfp], spec[group * nfp:]
            prod = jnp.concatenate([a_re * g_re - a_im * g_im, a_re * g_im + a_im * g_re], axis=0)
            out.append(_dot(inv_ref[...], prod.astype(BF16)))
        return jnp.concatenate(out, axis=0) if len(out) > 1 else out[0]

    nslab = c // V7X_MXU_DIM
    slabs = []
    glu = glu_slab(0)
    for p in range(nslab):
        nxt = glu_slab(p + 1) if p + 1 < nslab else None
        slabs.append(conv_slab(p, glu))
        glu = nxt
    conv = jnp.concatenate(slabs, axis=1) + bdw_ref[...]
    act = _silu(_layer_norm(conv, cg_ref[...], cb_ref[...])).astype(BF16)
    y = _dot(act, wpw2_ref[...]) + bpw2_ref[...]
    o_ref[0] = _layer_norm(alpha * h + g1 * y, lng_ref[...], lnb_ref[...])


def _conv_mixer_layer(h, seg, mod, p, alpha):
    bsz, s, d = h.shape
    t = _token_tile(s, 2 * WIDE_TOKEN_TILE)
    assert t % seg == 0 and seg % V7X_BF16_ROWS == 0
    c = p["w_pw2"].shape[0]
    fwd, inv, spec_tab = _dft_tables(seg, p["w_dw"].shape[0])
    assert t % fwd.shape[1] == 0 and c % V7X_MXU_DIM == 0
    gspec = _filter_spectrum(spec_tab, p["w_dw"])
    return pl.pallas_call(
        functools.partial(_conv_mixer_kernel, alpha=alpha, seg=seg),
        grid=(bsz, s // t),
        in_specs=[pl.BlockSpec((1, t, d), lambda b, j: (b, j, 0)),
                  pl.BlockSpec((1, MOD_ROWS, d), lambda b, j: (b, 0, 0)),
                  _resident((d, 2 * c)), _resident((1, 2 * c)),
                  _resident(fwd.shape), _resident(inv.shape), _resident(gspec.shape), _resident((1, c)),
                  _resident((1, c)), _resident((1, c)),
                  _resident((c, d)), _resident((1, d)), _resident((1, d)), _resident((1, d))],
        out_specs=pl.BlockSpec((1, t, d), lambda b, j: (b, j, 0)),
        out_shape=jax.ShapeDtypeStruct((bsz, s, d), F32),
        compiler_params=_params("parallel", "parallel"),
        name="conv_mixer_layer",
    )(h, mod, p["w_pw1"].astype(BF16), _row(p["b_pw1"]), jnp.asarray(fwd, BF16), jnp.asarray(inv, BF16),
      gspec, _row(p["b_dw"]), _row(p["ln_g"]), _row(p["ln_b"]), p["w_pw2"].astype(BF16),
      _row(p["b_pw2"]), _row(p["ln1_g"]), _row(p["ln1_b"]))


def _mlp_kernel(h_ref, mod_ref, w1_ref, w2_ref, g_ref, b_ref, o_ref, *, alpha):
    h = h_ref[0]
    sh, sc, g2 = mod_ref[0, 3:4, :], mod_ref[0, 4:5, :], mod_ref[0, 5:6, :]
    u = (h * (1.0 + sc) + sh).astype(BF16)
    ff = w1_ref.shape[1]
    acc = jnp.zeros(h.shape, F32)
    for f0 in range(0, ff, MLP_FF_CHUNK):
        hid = jnp.maximum(_dot(u, w1_ref[:, f0:f0 + MLP_FF_CHUNK]), 0.0)
        acc = acc + _dot((hid * hid).astype(BF16), w2_ref[f0:f0 + MLP_FF_CHUNK, :])
    o_ref[0] = _layer_norm(alpha * h + g2 * acc, g_ref[...], b_ref[...])


def _mlp_layer(h, mod, w1, w2, ln_g, ln_b, alpha):
    bsz, s, d = h.shape
    t = _token_tile(s, WIDE_TOKEN_TILE)
    ff = w1.shape[1]
    assert ff % MLP_FF_CHUNK == 0
    return pl.pallas_call(
        functools.partial(_mlp_kernel, alpha=alpha),
        grid=(bsz, s // t),
        in_specs=[pl.BlockSpec((1, t, d), lambda b, j: (b, j, 0)),
                  pl.BlockSpec((1, MOD_ROWS, d), lambda b, j: (b, 0, 0)),
                  _resident((d, ff)), _resident((ff, d)), _resident((1, d)), _resident((1, d))],
        out_specs=pl.BlockSpec((1, t, d), lambda b, j: (b, j, 0)),
        out_shape=jax.ShapeDtypeStruct((bsz, s, d), F32),
        compiler_params=_params("parallel", "parallel"),
        name="mlp_layer",
    )(h, mod, w1.astype(BF16), w2.astype(BF16), _row(ln_g), _row(ln_b))


def _mlstm_up_kernel(h_ref, prev_ref, next_ref, mod_ref, wup_ref, wconv_ref, bconv_ref,
                     xm_ref, xc_ref, z_ref, u_ref, xm_all_ref):
    t = h_ref.shape[1]
    e = xm_ref.shape[2]
    halo = prev_ref.shape[1]
    slab = 2 * V7X_MXU_DIM
    per_slab = slab // V7X_LANES
    left = (QKV_CONV_WIDTH - 1) // 2
    j = pl.program_id(1)
    sh, sc = mod_ref[0, 0:1, :], mod_ref[0, 1:2, :]
    modulate = lambda v: (v * (1.0 + sc) + sh).astype(BF16)
    u_ref[0:halo, :] = modulate(prev_ref[0])
    u_ref[halo:halo + t, :] = modulate(h_ref[0])
    u_ref[halo + t:, :] = modulate(next_ref[0])
    keep_prev = jnp.where(j == 0, 0.0, 1.0)
    keep_next = jnp.where(j == pl.num_programs(1) - 1, 0.0, 1.0)

    def xm_matmul(p):
        return _dot(u_ref[...], wup_ref[:, p * slab:(p + 1) * slab])

    def xm_store(p, xm_all):
        cols = slice(p * slab, (p + 1) * slab)
        xm_ref[0, :, cols] = xm_all[halo:halo + t].astype(BF16)
        for q in range(per_slab):
            cb = per_slab * p + q
            lanes = slice(q * V7X_LANES, (q + 1) * V7X_LANES)
            xm_all_ref[cb, 0:halo, :] = xm_all[0:halo, lanes] * keep_prev
            xm_all_ref[cb, halo:halo + t, :] = xm_all[halo:halo + t, lanes]
            xm_all_ref[cb, halo + t:, :] = xm_all[halo + t:, lanes] * keep_next

    def conv_block(cb):
        lanes = slice(cb * V7X_LANES, (cb + 1) * V7X_LANES)
        for r0 in range(0, t, CONV_ROW_BLOCK):
            acc = jnp.zeros((CONV_ROW_BLOCK, V7X_LANES), F32)
            for k in range(QKV_CONV_WIDTH):
                start = halo + r0 - left + k
                acc = acc + xm_all_ref[cb, start:start + CONV_ROW_BLOCK, :] * wconv_ref[k:k + 1, lanes]
            xc_ref[0, r0:r0 + CONV_ROW_BLOCK, lanes] = _silu(acc + bconv_ref[:, lanes]).astype(BF16)

    nslab = e // slab
    for p in range(nslab):
        xm_store(p, xm_matmul(p))
    for p in range(nslab):
        cols = slice(e + p * slab, e + (p + 1) * slab)
        z_ref[0, :, p * slab:(p + 1) * slab] = _dot(u_ref[halo:halo + t, :], wup_ref[:, cols]).astype(BF16)
        for q in range(per_slab):
            conv_block(per_slab * p + q)


def _mlstm_up(h, mod, w_up, w_conv, b_conv):
    bsz, s, d = h.shape
    e = w_up.shape[1] // 2
    t = _token_tile(s, WIDE_TOKEN_TILE)
    halo = V7X_BF16_ROWS
    assert t % halo == 0 and t % CONV_ROW_BLOCK == 0
    nhb = s // halo
    taps = w_conv.shape[0]
    w_conv = jnp.pad(w_conv, ((0, V7X_SUBLANES - taps), (0, 0)))
    tile = pl.BlockSpec((1, t, e), lambda b, j: (b, j, 0))
    out = jax.ShapeDtypeStruct((bsz, s, e), BF16)
    return pl.pallas_call(
        _mlstm_up_kernel,
        grid=(bsz, s // t),
        in_specs=[pl.BlockSpec((1, t, d), lambda b, j: (b, j, 0)),
                  pl.BlockSpec((1, halo, d), lambda b, j: (b, jnp.maximum(j * (t // halo) - 1, 0), 0)),
                  pl.BlockSpec((1, halo, d), lambda b, j: (b, jnp.minimum((j + 1) * (t // halo), nhb - 1), 0)),
                  pl.BlockSpec((1, MOD_ROWS, d), lambda b, j: (b, 0, 0)),
                  _resident((d, 2 * e)), _resident((V7X_SUBLANES, e)), _resident((1, e))],
        out_specs=[tile, tile, tile],
        out_shape=[out, out, out],
        scratch_shapes=[pltpu.VMEM((t + 2 * halo, d), BF16),
                        pltpu.VMEM((e // V7X_LANES, t + 2 * halo, V7X_LANES), F32)],
        compiler_params=_params("parallel", "parallel"),
        name="mlstm_up",
    )(h, h, h, mod, w_up.astype(BF16), w_conv, _row(b_conv))


def _log_sigmoid(x):
    return jnp.minimum(x, 0.0) - jnp.log1p(jnp.exp(-jnp.abs(x)))


def _split_bf16x3(x):
    hi = x.astype(BF16)
    r1 = x - hi.astype(F32)
    mid = r1.astype(BF16)
    lo = (r1 - mid.astype(F32)).astype(BF16)
    return jnp.concatenate([hi, mid, lo], axis=1)


def _mlstm_qkv_kernel(*refs, k_scale, emit_intra):
    refs = list(refs)
    xc_ref, xm_ref, wq_ref, wk_ref, wv_ref, wg_ref, bg_ref, q_ref, kt_ref, v_ref, g_ref = refs[:11]
    h0_ref = refs[11] if emit_intra else None
    e = wq_ref.shape[0]
    t = xc_ref.shape[1]
    nh = MLSTM_HEADS
    dh = e // nh
    half = V7X_LANES // 2
    xc = xc_ref[0]
    q = _dot(xc, wq_ref[...]).astype(BF16)
    q_ref[0] = q
    gates = _dot(q, wg_ref[0:e, :])
    k = _dot(xc, wk_ref[...])
    kt_ref[0, 0] = (k * k_scale).T.astype(BF16)
    gates = gates + _dot(k.astype(BF16), wg_ref[e:2 * e, :])
    v = _dot(xm_ref[0], wv_ref[...]).astype(BF16)
    v_ref[0] = v
    gates = gates + _dot(v, wg_ref[2 * e:, :]) + bg_ref[...]

    lane = lax.broadcasted_iota(jnp.int32, (1, V7X_LANES), 1)
    group = (lane % GATE_PACK_LANES) // nh
    rows = lax.broadcasted_iota(jnp.int32, (t, t), 0)
    cols = lax.broadcasted_iota(jnp.int32, (t, t), 1)
    visible = (rows >= cols, rows <= cols)
    lf = _log_sigmoid(gates)
    parts = _split_bf16x3(lf)
    fold = lambda c3: c3[:, :V7X_LANES] + c3[:, V7X_LANES:2 * V7X_LANES] + c3[:, 2 * V7X_LANES:]
    prefix = fold(_dot(jnp.where(visible[0], 1.0, 0.0).astype(BF16), parts))
    suffix = fold(_dot(jnp.where(visible[1], 1.0, 0.0).astype(BF16), parts))
    b = jnp.where(lane < GATE_PACK_LANES, prefix, suffix)
    a = pltpu.roll(gates, half, 1) - b
    total = jnp.sum(lf, axis=0, keepdims=True)
    amax = jnp.max(a, axis=0, keepdims=True)
    pack = jnp.where(group == 0, b, jnp.where(group == 2, a, jnp.where(group == 4, total,
                                                                         jnp.where(group == 5, amax, 0.0))))
    if emit_intra:
        a_rows = a.T
        qk_of = lambda hd: _dot(q[:, hd * dh:(hd + 1) * dh], kt_ref[0, 0, hd * dh:(hd + 1) * dh, :])
        qk_next = qk_of(0)
        for hd in range(nh):
            lanes = slice(hd * dh, (hd + 1) * dh)
            qk = qk_next
            if hd + 1 < nh:
                qk_next = qk_of(hd + 1)
            for d in range(2):
                base = d * GATE_PACK_LANES + hd
                a_row = a_rows[base + 2 * nh:base + 2 * nh + 1, :]
                masked = jnp.where(visible[d], a_row, NEG_BIG)
                cmax = jnp.max(masked, axis=1, keepdims=True)
                scores = qk * jnp.exp(masked - cmax)
                rowsum = jnp.sum(scores, axis=1, keepdims=True)
                h0_ref[d, 0, :, lanes] = _dot(scores.astype(BF16), v[:, lanes]).astype(BF16)
                pack = jnp.where(lane == base + nh, cmax, jnp.where(lane == base + 3 * nh, rowsum, pack))
    g_ref[0] = pack[:, :g_ref.shape[2]]


def _gate_lane_layout():
    nh = MLSTM_HEADS
    src = np.full((V7X_LANES,), -1, np.int32)
    for d in range(2):
        for g in range(GATE_PACK_LANES // nh):
            for h in range(nh):
                lane = d * GATE_PACK_LANES + g * nh + h
                if g in (0, 2, 4, 5):
                    src[lane] = (2 * d + 1) * nh + h
                if g in (2, 5):
                    src[V7X_LANES // 2 + lane] = 2 * d * nh + h
    return src


def _mlstm_qkv(xc, xm, w_q, w_k, w_v, w_gate, b_gate, emit_intra):
    bsz, s, e = xc.shape
    assert 2 * GATE_PACK_LANES == V7X_LANES // 2
    t = _token_tile(s, SCAN_CHUNK)
    k_scale = float(e // MLSTM_HEADS) ** -0.5
    src = _gate_lane_layout()
    lay = lambda w: jnp.where(src >= 0, w[:, np.maximum(src, 0)], 0.0)
    tile = pl.BlockSpec((1, t, e), lambda b, j: (b, j, 0))
    out = jax.ShapeDtypeStruct((bsz, s, e), BF16)
    out_specs = [tile, pl.BlockSpec((1, 1, e, t), lambda b, j: (b, j, 0, 0)), tile,
                 pl.BlockSpec((1, t, 2 * GATE_PACK_LANES), lambda b, j: (b, j, 0))]
    out_shape = [out, jax.ShapeDtypeStruct((bsz, s // t, e, t), BF16), out,
                 jax.ShapeDtypeStruct((bsz, s, 2 * GATE_PACK_LANES), F32)]
    if emit_intra:
        out_specs.append(pl.BlockSpec((2, 1, t, e), lambda b, j: (0, b, j, 0)))
        out_shape.append(jax.ShapeDtypeStruct((2, bsz, s, e), BF16))
    outs = pl.pallas_call(
        functools.partial(_mlstm_qkv_kernel, k_scale=k_scale, emit_intra=emit_intra),
        grid=(bsz, s // t),
        in_specs=[tile, tile, _resident((e, e)), _resident((e, e)), _resident((e, e)),
                  _resident((3 * e, V7X_LANES)), _resident((1, V7X_LANES))],
        out_specs=out_specs,
        out_shape=out_shape,
        compiler_params=_params("parallel", "parallel"),
        name="mlstm_qkv",
    )(xc, xm, w_q.astype(BF16), w_k.astype(BF16), w_v.astype(BF16), lay(w_gate).astype(BF16),
      lay(_row(b_gate)))
    return tuple(outs) if emit_intra else tuple(outs) + (None,)


def _mlstm_scan_kernel(*refs, has_init, emit_h, emit_state):
    refs = list(refs)
    q_ref, kt_ref, v_ref, g_ref, gt_ref = refs[:5]
    del refs[:5]
    if emit_h:
        h0_ref = refs.pop(0)
    if has_init:
        c0_ref, n0_ref, m0_ref = refs[:3]
        del refs[:3]
    if emit_h:
        o_ref = refs.pop(0)
    if emit_state:
        cf_ref, nf_ref, mf_ref = refs[:3]
        del refs[:3]
    c_ref, n_ref, m_ref = refs

    length = kt_ref.shape[3]
    chunks = kt_ref.shape[1]
    nh, dh = c_ref.shape[0], c_ref.shape[1]
    backward = pl.program_id(1) == 1
    step = pl.program_id(2)

    @pl.when(step == 0)
    def _():
        if has_init:
            c_ref[...] = c0_ref[0, 0]
            n_ref[...] = n0_ref[0, 0]
            m_ref[...] = m0_ref[0, 0]
        else:
            c_ref[...] = jnp.zeros(c_ref.shape, F32)
            n_ref[...] = jnp.zeros(n_ref.shape, F32)
            m_ref[...] = jnp.zeros(m_ref.shape, F32)

    ones_rows = jnp.ones((V7X_SUBLANES, length), BF16)
    contract_lanes = (((1,), (1,)), ((), ()))

    def one_chunk(i, carry):
        sub = jnp.where(backward, chunks - 1 - i, i)
        rows = pl.ds(pl.multiple_of(sub * length, length), length)
        g = g_ref[0, 0, rows, :]
        gt = gt_ref[0, 0, sub]
        grp = lambda i: g[:, i * nh:(i + 1) * nh]
        grp_t = lambda i: gt[i * nh:(i + 1) * nh, :]
        m_old = m_ref[nh:nh + 1, 0:nh]
        m_old_t = m_ref[0:nh, 0:1]
        total, amax = grp(4)[0:1], grp(5)[0:1]
        m_new = total + jnp.maximum(m_old, amax)
        decay = jnp.exp(total + m_old - m_new)
        m_new_t = grp_t(4) + jnp.maximum(m_old_t, grp_t(5))
        w_k = jnp.exp(grp_t(4) + grp_t(2) - m_new_t)
        if emit_h:
            b_t, cmax, rowsum = grp(0), grp(1), grp(3)
            m_row = jnp.maximum(m_old, cmax)
            r_intra = jnp.exp(cmax - m_row)
            w_inter = jnp.exp(m_old - m_row)
            den_floor = jnp.exp(-(b_t + m_row))

            for h in range(nh):
                lanes = slice(h * dh, (h + 1) * dh)
                col = slice(h, h + 1)
                q = q_ref[0, rows, lanes]
                q_n = lax.dot_general(q, n_ref[h].astype(BF16), contract_lanes,
                                      preferred_element_type=F32)[:, 0:1]
                num = (r_intra[:, col] * h0_ref[0, 0, rows, lanes].astype(F32)
                       + w_inter[:, col] * _dot(q, c_ref[h].astype(BF16)))
                den = r_intra[:, col] * rowsum[:, col] + w_inter[:, col] * q_n
                o_ref[0, 0, rows, lanes] = (num / jnp.maximum(jnp.abs(den), den_floor[:, col])).astype(o_ref.dtype)

        for h in range(nh):
            lanes = slice(h * dh, (h + 1) * dh)
            col = slice(h, h + 1)
            wkt = (kt_ref[0, sub, lanes, :].astype(F32) * w_k[h:h + 1, :]).astype(BF16)
            c_ref[h] = decay[:, col] * c_ref[h] + _dot(wkt, v_ref[0, rows, lanes])
            n_ref[h] = decay[:, col] * n_ref[h] + lax.dot_general(ones_rows, wkt, contract_lanes,
                                                                  preferred_element_type=F32)
        m_ref[0:nh, :] = m_new_t[:, 0:V7X_LANES]
        m_ref[nh:nh + 1, 0:nh] = m_new
        return carry

    lax.fori_loop(0, chunks, one_chunk, 0)

    if emit_state:
        @pl.when(step == pl.num_programs(2) - 1)
        def _():
            cf_ref[0, 0] = c_ref[...]
            nf_ref[0, 0] = n_ref[...]
            mf_ref[0, 0] = m_ref[...]


def _mlstm_scan(q, kt, v, gates, h0, init_state, emit_state):
    bsz, s, e = q.shape
    nh = MLSTM_HEADS
    dh = e // nh
    emit_h = h0 is not None
    length = _token_tile(s, SCAN_CHUNK)
    nch = s // length
    per_step = SCAN_CHUNKS_PER_STEP if nch % SCAN_CHUNKS_PER_STEP == 0 else 1
    nst = nch // per_step
    span = per_step * length
    g = gates.reshape(bsz, s, 2, GATE_PACK_LANES).transpose(0, 2, 1, 3)
    g_t = g.reshape(bsz, 2, nch, length, GATE_PACK_LANES).transpose(0, 1, 2, 4, 3)
    chunk = lambda d, st: jnp.where(d == 0, st, nst - 1 - st)
    tile = pl.BlockSpec((1, span, e), lambda b, d, st: (b, chunk(d, st), 0))
    dir_tile = pl.BlockSpec((1, 1, span, e), lambda b, d, st: (d, b, chunk(d, st), 0))
    state_shapes = [(nh, dh, dh), (nh, V7X_SUBLANES, dh), (V7X_SUBLANES, V7X_LANES)]
    state_specs = [pl.BlockSpec((1, 1) + shp, lambda b, d, st, n=len(shp): (b, d) + (0,) * n)
                   for shp in state_shapes]
    scratch = [pltpu.VMEM(shp, F32) for shp in state_shapes]

    in_specs = [tile, pl.BlockSpec((1, per_step, e, length), lambda b, d, st: (b, chunk(d, st), 0, 0)), tile,
                pl.BlockSpec((1, 1, span, GATE_PACK_LANES), lambda b, d, st: (b, d, chunk(d, st), 0)),
                pl.BlockSpec((1, 1, per_step, GATE_PACK_LANES, length),
                             lambda b, d, st: (b, d, chunk(d, st), 0, 0))]
    args = [q, kt, v, g, g_t]
    if emit_h:
        in_specs.append(dir_tile)
        args.append(h0)
    if init_state is not None:
        in_specs += state_specs
        args += list(init_state)
    out_specs, out_shape = [], []
    if emit_h:
        out_specs.append(dir_tile)
        out_shape.append(jax.ShapeDtypeStruct((2, bsz, s, e), BF16))
    if emit_state:
        out_specs += state_specs
        out_shape += [jax.ShapeDtypeStruct((bsz, 2) + shp, F32) for shp in state_shapes]
    outs = pl.pallas_call(
        functools.partial(_mlstm_scan_kernel, has_init=init_state is not None, emit_h=emit_h,
                          emit_state=emit_state),
        grid=(bsz, 2, nst),
        in_specs=in_specs,
        out_specs=out_specs,
        out_shape=out_shape,
        scratch_shapes=scratch,
        compiler_params=_params("parallel", "parallel", "arbitrary"),
        name="mlstm_scan",
    )(*args)
    outs = list(outs)
    h = outs.pop(0) if emit_h else None
    return h, (tuple(outs) if emit_state else None)


def _mlstm_out_kernel(h_ref, xm_ref, xc_ref, z_ref, hs_ref, mod_ref, wo_ref, bo_ref, gn_ref, skip_ref,
                      wdown_ref, lng_ref, lnb_ref, o_ref, *, alpha):
    e = xm_ref.shape[2]
    dh = e // MLSTM_HEADS
    sub_rows = NARROW_TOKEN_TILE
    g1 = mod_ref[0, 2:3, :]

    def sub_tile(i, carry):
        rows = pl.ds(pl.multiple_of(i * sub_rows, sub_rows), sub_rows)
        h = h_ref[0, rows, :]
        xm = xm_ref[0, rows, :]
        acc = jnp.zeros(h.shape, F32)

        def gate_logits(hd):
            lanes = slice(hd * dh, (hd + 1) * dh)
            lanes_b = slice(e + hd * dh, e + (hd + 1) * dh)
            return (_dot(xm, wo_ref[:, lanes]) + bo_ref[:, lanes],
                    _dot(xm, wo_ref[:, lanes_b]) + bo_ref[:, lanes_b])

        logits = gate_logits(0)
        for hd in range(MLSTM_HEADS):
            lanes = slice(hd * dh, (hd + 1) * dh)
            o_f, o_b = jax.nn.sigmoid(logits[0]), jax.nn.sigmoid(logits[1])
            if hd + 1 < MLSTM_HEADS:
                logits = gate_logits(hd + 1)
            hsum = (o_f * hs_ref[0, 0, rows, lanes].astype(F32)
                    + o_b * hs_ref[1, 0, rows, lanes].astype(F32))
            mu = jnp.mean(hsum, axis=-1, keepdims=True)
            dlt = hsum - mu
            var = jnp.mean(dlt * dlt, axis=-1, keepdims=True)
            hn = dlt * lax.rsqrt(var + LN_EPS) * gn_ref[:, lanes]
            y = ((hn + skip_ref[:, lanes] * xc_ref[0, rows, lanes].astype(F32))
                 * _silu(z_ref[0, rows, lanes].astype(F32)))
            acc = acc + _dot(y.astype(BF16), wdown_ref[lanes, :])
        o_ref[0, rows, :] = _layer_norm(alpha * h + g1 * acc, lng_ref[...], lnb_ref[...])
        return carry

    lax.fori_loop(0, h_ref.shape[1] // sub_rows, sub_tile, 0)


def _mlstm_out(h, xm, xc, z, hs, mod, w_o, b_o, gn_g, skip, w_down, ln_g, ln_b, alpha):
    bsz, s, d = h.shape
    e = xm.shape[2]
    t = _token_tile(s, WIDE_TOKEN_TILE)
    assert t % NARROW_TOKEN_TILE == 0
    tile = lambda width: pl.BlockSpec((1, t, width), lambda b, j: (b, j, 0))
    return pl.pallas_call(
        functools.partial(_mlstm_out_kernel, alpha=alpha),
        grid=(bsz, s // t),
        in_specs=[tile(d), tile(e), tile(e), tile(e),
                  pl.BlockSpec((2, 1, t, e), lambda b, j: (0, b, j, 0)),
                  pl.BlockSpec((1, MOD_ROWS, d), lambda b, j: (b, 0, 0)),
                  _resident((e, 2 * e)), _resident((1, 2 * e)), _resident((1, e)), _resident((1, e)),
                  _resident((e, d)), _resident((1, d)), _resident((1, d))],
        out_specs=pl.BlockSpec((1, t, d), lambda b, j: (b, j, 0)),
        out_shape=jax.ShapeDtypeStruct((bsz, s, d), F32),
        compiler_params=_params("parallel", "parallel"),
        name="mlstm_out",
    )(h, xm, xc, z, hs, mod, w_o.astype(BF16), _row(b_o), _row(gn_g), _row(skip),
      w_down.astype(BF16), _row(ln_g), _row(ln_b))


def _mod_tables(ada_out_i, bsz, d):
    m = ada_out_i.reshape(MOD_ROWS, 6, d)
    pad = ((0, 0), (0, MOD_ROWS - 6), (0, 0))
    lat = jnp.pad(m[:bsz], pad)
    ctx = jnp.pad(jnp.broadcast_to(m[bsz][None], (bsz, 6, d)), pad)
    return lat, ctx


def kernel(x, c, ctx, c_ctx, ada_w, ada_b, ln1_g, ln1_b, ln2_g, ln2_b, mlp_w1, mlp_w2, conv_w_pw1, conv_b_pw1, conv_w_dw, conv_b_dw, conv_ln_g, conv_ln_b, conv_w_pw2, conv_b_pw2, ml_w_up, ml_w_conv, ml_b_conv, ml_w_q, ml_w_k, ml_w_v, ml_w_gate, ml_b_gate, ml_w_o, ml_b_o, ml_gn_g, ml_skip, ml_w_down):
    bsz, s, d = x.shape
    depth = ada_w.shape[0]
    assert depth == 2 and bsz + 1 <= MOD_ROWS, "built for the two-layer block: conv mixer, then mLSTM"
    alpha = (2 * depth) ** 0.25

    cvec = jnp.concatenate([c, c_ctx[None], jnp.zeros((MOD_ROWS - bsz - 1, d), F32)], axis=0)
    ada = _adaln(cvec, ada_w, ada_b)
    mod0, mod0_ctx = _mod_tables(ada[0], bsz, d)
    mod1, mod1_ctx = _mod_tables(ada[1], bsz, d)

    conv_p = dict(w_pw1=conv_w_pw1[0], b_pw1=conv_b_pw1[0], w_dw=conv_w_dw[0], b_dw=conv_b_dw[0],
                  ln_g=conv_ln_g[0], ln_b=conv_ln_b[0], w_pw2=conv_w_pw2[0], b_pw2=conv_b_pw2[0],
                  ln1_g=ln1_g[0], ln1_b=ln1_b[0])
    h = _conv_mixer_layer(x, GRID_W, mod0, conv_p, alpha)
    h = _mlp_layer(h, mod0, mlp_w1[0], mlp_w2[0], ln2_g[0], ln2_b[0], alpha)
    hc = _conv_mixer_layer(ctx, ctx.shape[1], mod0_ctx, conv_p, alpha)
    hc = _mlp_layer(hc, mod0_ctx, mlp_w1[0], mlp_w2[0], ln2_g[0], ln2_b[0], alpha)

    feats = lambda hh, mm: _mlstm_up(hh, mm, ml_w_up[0], ml_w_conv[0], ml_b_conv[0])
    qkv = lambda xc_, xm_, intra: _mlstm_qkv(xc_, xm_, ml_w_q[0], ml_w_k[0], ml_w_v[0], ml_w_gate[0],
                                             ml_b_gate[0], emit_intra=intra)
    xm_c, xc_c, _ = feats(hc, mod1_ctx)
    _, ctx_state = _mlstm_scan(*qkv(xc_c, xm_c, False), init_state=None, emit_state=True)
    xm, xc, z = feats(h, mod1)
    hs, _ = _mlstm_scan(*qkv(xc, xm, True), init_state=ctx_state, emit_state=False)
    h = _mlstm_out(h, xm, xc, z, hs, mod1, ml_w_o[0], ml_b_o[0], ml_gn_g[0], ml_skip[0],
                   ml_w_down[0], ln1_g[1], ln1_b[1], alpha)
    return _mlp_layer(h, mod1, mlp_w1[1], mlp_w2[1], ln2_g[1], ln2_b[1], alpha)
```

```python
import functools

import jax
import jax.numpy as jnp
import numpy as np
from jax import lax
from jax.experimental import pallas as pl
from jax.experimental.pallas import tpu as pltpu

F32 = jnp.float32
BF16 = jnp.bfloat16

GRID_W = 64
QKV_CONV_WIDTH = 5
MLSTM_HEADS = 4
GATE_PACK_LANES = 8 * MLSTM_HEADS
LN_EPS = 1e-5

V7X_LANES = 128
V7X_SUBLANES = 8
V7X_BF16_ROWS = 16
V7X_MXU_DIM = 256
V7X_VMEM_LIMIT_BYTES = 56 * 1024 * 1024

WIDE_TOKEN_TILE = 512
NARROW_TOKEN_TILE = 256
SCAN_CHUNK = 256
SCAN_CHUNKS_PER_STEP = 4
MLP_FF_CHUNK = 1024
ADALN_COLUMN_BLOCKS = 4
CONV_ROW_BLOCK = 64
MOD_ROWS = 8
NEG_BIG = -1e30


def _resident(shape):
    zeros = (0,) * len(shape)
    return pl.BlockSpec(shape, lambda *_: zeros, pipeline_mode=pl.Buffered(1))


def _params(*semantics):
    return pltpu.CompilerParams(dimension_semantics=semantics,
                                vmem_limit_bytes=V7X_VMEM_LIMIT_BYTES)


def _token_tile(s, preferred):
    t = min(s, preferred)
    assert s % t == 0
    return t


def _layer_norm(r, g, b):
    mu = jnp.mean(r, axis=-1, keepdims=True)
    d = r - mu
    var = jnp.mean(d * d, axis=-1, keepdims=True)
    return d * lax.rsqrt(var + LN_EPS) * g + b


def _silu(x):
    return x * jax.nn.sigmoid(x)


def _dot(a, b):
    return jnp.dot(a, b, preferred_element_type=F32)


def _row(v):
    return v.reshape(1, -1)


def _adaln_kernel(c_ref, w_ref, b_ref, o_ref):
    o_ref[0] = _dot(_silu(c_ref[...]), w_ref[0]) + b_ref[0]


def _adaln(cvec, ada_w, ada_b):
    depth, d, n = ada_w.shape
    assert n % (ADALN_COLUMN_BLOCKS * V7X_LANES) == 0
    tn = n // ADALN_COLUMN_BLOCKS
    return pl.pallas_call(
        _adaln_kernel,
        grid=(depth, n // tn),
        in_specs=[pl.BlockSpec((MOD_ROWS, d), lambda i, j: (0, 0)),
                  pl.BlockSpec((1, d, tn), lambda i, j: (i, 0, j)),
                  pl.BlockSpec((1, 1, tn), lambda i, j: (i, 0, j))],
        out_specs=pl.BlockSpec((1, MOD_ROWS, tn), lambda i, j: (i, 0, j)),
        out_shape=jax.ShapeDtypeStruct((depth, MOD_ROWS, n), F32),
        compiler_params=_params("parallel", "parallel"),
        name="adaln",
    )(cvec, ada_w, ada_b.reshape(depth, 1, n))


def _dft_tables(seg, taps):
    left = (taps - 1) // 2
    nfft = 1 << (seg + taps - 2).bit_length()
    nf = nfft // 2 + 1
    nfp = -(-nf // V7X_SUBLANES) * V7X_SUBLANES
    taps_padded = -(-taps // V7X_SUBLANES) * V7X_SUBLANES
    f = np.arange(nf, dtype=np.float64)[:, None]
    ang = 2.0 * np.pi * f * np.arange(seg)[None, :] / nfft
    fwd = np.zeros((2 * nfp, seg))
    fwd[:nf], fwd[nfp:nfp + nf] = np.cos(ang), -np.sin(ang)
    weight = np.where((f == 0) | (f == nfft // 2), 1.0, 2.0) / nfft
    inv = np.zeros((seg, 2 * nfp))
    inv[:, :nf], inv[:, nfp:nfp + nf] = (weight * np.cos(ang)).T, (-weight * np.sin(ang)).T
    ang_w = 2.0 * np.pi * f * (left - np.arange(taps))[None, :] / nfft
    spec = np.zeros((2 * nfp, taps_padded))
    spec[:nf, :taps], spec[nfp:nfp + nf, :taps] = np.cos(ang_w), -np.sin(ang_w)
    group = max(1, V7X_MXU_DIM // seg)
    eye = np.eye(group)
    fwd = np.concatenate([np.kron(eye, fwd[:nfp]), np.kron(eye, fwd[nfp:])], axis=0)
    inv = np.concatenate([np.kron(eye, inv[:, :nfp]), np.kron(eye, inv[:, nfp:])], axis=1)
    return fwd.astype(np.float32), inv.astype(np.float32), spec.astype(np.float32)


def _filter_spectrum_kernel(tab_ref, w_ref, o_ref):
    o_ref[...] = jnp.dot(tab_ref[...], w_ref[...], preferred_element_type=F32,
                         precision=lax.Precision.HIGHEST)


def _filter_spectrum(spec_tab, w_dw):
    rows, taps_padded = spec_tab.shape
    c = w_dw.shape[1]
    w = jnp.pad(w_dw, ((0, taps_padded - w_dw.shape[0]), (0, 0)))
    return pl.pallas_call(
        _filter_spectrum_kernel,
        out_shape=jax.ShapeDtypeStruct((rows, c), F32),
        name="conv_filter_spectrum",
    )(jnp.asarray(spec_tab), w)


def _conv_mixer_kernel(h_ref, mod_ref, wpw1_ref, bpw1_ref, fwd_ref, inv_ref, gspec_ref, bdw_ref, cg_ref, cb_ref,
                       wpw2_ref, bpw2_ref, lng_ref, lnb_ref, o_ref, *, alpha, seg):
    c = wpw2_ref.shape[0]
    rows = fwd_ref.shape[1]
    nfp = gspec_ref.shape[0] // 2
    t = h_ref.shape[1]
    h = h_ref[0]
    sh, sc, g1 = mod_ref[0, 0:1, :], mod_ref[0, 1:2, :], mod_ref[0, 2:3, :]
    u = (h * (1.0 + sc) + sh).astype(BF16)

    def glu_slab(p):
        lo = slice(p * V7X_MXU_DIM, (p + 1) * V7X_MXU_DIM)
        hi = slice(c + p * V7X_MXU_DIM, c + (p + 1) * V7X_MXU_DIM)
        a = _dot(u, wpw1_ref[:, lo]) + bpw1_ref[:, lo]
        gate = _dot(u, wpw1_ref[:, hi]) + bpw1_ref[:, hi]
        return (a * jax.nn.sigmoid(gate)).astype(BF16)

    def conv_slab(p, glu):
        lo = slice(p * V7X_MXU_DIM, (p + 1) * V7X_MXU_DIM)
        group = rows // seg
        g_re = jnp.concatenate([gspec_ref[0:nfp, lo]] * group, axis=0)
        g_im = jnp.concatenate([gspec_ref[nfp:, lo]] * group, axis=0)
        out = []
        for r0 in range(0, t, rows):
            spec = _dot(fwd_ref[...], glu[r0:r0 + rows])
            a_re, a_im = spec[0:group * nfp], spec[group * nfp:]
            prod = jnp.concatenate([a_re * g_re - a_im * g_im, a_re * g_im + a_im * g_re], axis=0)
            out.append(_dot(inv_ref[...], prod.astype(BF16)))
        return jnp.concatenate(out, axis=0) if len(out) > 1 else out[0]

    nslab = c // V7X_MXU_DIM
    slabs = []
    glu = glu_slab(0)
    for p in range(nslab):
        nxt = glu_slab(p + 1) if p + 1 < nslab else None
        slabs.append(conv_slab(p, glu))
        glu = nxt
    conv = jnp.concatenate(slabs, axis=1) + bdw_ref[...]
    act = _silu(_layer_norm(conv, cg_ref[...], cb_ref[...])).astype(BF16)
    y = _dot(act, wpw2_ref[...]) + bpw2_ref[...]
    o_ref[0] = _layer_norm(alpha * h + g1 * y, lng_ref[...], lnb_ref[...])


def _conv_mixer_layer(h, seg, mod, p, alpha):
    bsz, s, d = h.shape
    t = _token_tile(s, 2 * WIDE_TOKEN_TILE)
    assert t % seg == 0 and seg % V7X_BF16_ROWS == 0
    c = p["w_pw2"].shape[0]
    fwd, inv, spec_tab = _dft_tables(seg, p["w_dw"].shape[0])
    assert t % fwd.shape[1] == 0 and c % V7X_MXU_DIM == 0
    gspec = _filter_spectrum(spec_tab, p["w_dw"])
    return pl.pallas_call(
        functools.partial(_conv_mixer_kernel, alpha=alpha, seg=seg),
        grid=(bsz, s // t),
        in_specs=[pl.BlockSpec((1, t, d), lambda b, j: (b, j, 0)),
                  pl.BlockSpec((1, MOD_ROWS, d), lambda b, j: (b, 0, 0)),
                  _resident((d, 2 * c)), _resident((1, 2 * c)),
                  _resident(fwd.shape), _resident(inv.shape), _resident(gspec.shape), _resident((1, c)),
                  _resident((1, c)), _resident((1, c)),
                  _resident((c, d)), _resident((1, d)), _resident((1, d)), _resident((1, d))],
        out_specs=pl.BlockSpec((1, t, d), lambda b, j: (b, j, 0)),
        out_shape=jax.ShapeDtypeStruct((bsz, s, d), F32),
        compiler_params=_params("parallel", "parallel"),
        name="conv_mixer_layer",
    )(h, mod, p["w_pw1"].astype(BF16), _row(p["b_pw1"]), jnp.asarray(fwd, BF16), jnp.asarray(inv, BF16),
      gspec, _row(p["b_dw"]), _row(p["ln_g"]), _row(p["ln_b"]), p["w_pw2"].astype(BF16),
      _row(p["b_pw2"]), _row(p["ln1_g"]), _row(p["ln1_b"]))


def _mlp_kernel(h_ref, mod_ref, w1_ref, w2_ref, g_ref, b_ref, o_ref, *, alpha):
    h = h_ref[0]
    sh, sc, g2 = mod_ref[0, 3:4, :], mod_ref[0, 4:5, :], mod_ref[0, 5:6, :]
    u = (h * (1.0 + sc) + sh).astype(BF16)
    ff = w1_ref.shape[1]
    acc = jnp.zeros(h.shape, F32)
    for f0 in range(0, ff, MLP_FF_CHUNK):
        hid = jnp.maximum(_dot(u, w1_ref[:, f0:f0 + MLP_FF_CHUNK]), 0.0)
        acc = acc + _dot((hid * hid).astype(BF16), w2_ref[f0:f0 + MLP_FF_CHUNK, :])
    o_ref[0] = _layer_norm(alpha * h + g2 * acc, g_ref[...], b_ref[...])


def _mlp_layer(h, mod, w1, w2, ln_g, ln_b, alpha):
    bsz, s, d = h.shape
    t = _token_tile(s, 2 * WIDE_TOKEN_TILE)
    ff = w1.shape[1]
    assert ff % MLP_FF_CHUNK == 0
    return pl.pallas_call(
        functools.partial(_mlp_kernel, alpha=alpha),
        grid=(bsz, s // t),
        in_specs=[pl.BlockSpec((1, t, d), lambda b, j: (b, j, 0)),
                  pl.BlockSpec((1, MOD_ROWS, d), lambda b, j: (b, 0, 0)),
                  _resident((d, ff)), _resident((ff, d)), _resident((1, d)), _resident((1, d))],
        out_specs=pl.BlockSpec((1, t, d), lambda b, j: (b, j, 0)),
        out_shape=jax.ShapeDtypeStruct((bsz, s, d), F32),
        compiler_params=_params("parallel", "parallel"),
        name="mlp_layer",
    )(h, mod, w1.astype(BF16), w2.astype(BF16), _row(ln_g), _row(ln_b))


def _mlstm_up_kernel(h_ref, prev_ref, next_ref, mod_ref, wup_ref, wconv_ref, bconv_ref,
                     xm_ref, xc_ref, z_ref, u_ref, xm_all_ref):
    t = h_ref.shape[1]
    e = xm_ref.shape[2]
    halo = prev_ref.shape[1]
    slab = 2 * V7X_MXU_DIM
    per_slab = slab // V7X_LANES
    left = (QKV_CONV_WIDTH - 1) // 2
    j = pl.program_id(1)
    sh, sc = mod_ref[0, 0:1, :], mod_ref[0, 1:2, :]
    modulate = lambda v: (v * (1.0 + sc) + sh).astype(BF16)
    u_ref[0:halo, :] = modulate(prev_ref[0])
    u_ref[halo:halo + t, :] = modulate(h_ref[0])
    u_ref[halo + t:, :] = modulate(next_ref[0])
    keep_prev = jnp.where(j == 0, 0.0, 1.0)
    keep_next = jnp.where(j == pl.num_programs(1) - 1, 0.0, 1.0)

    def xm_matmul(p):
        return _dot(u_ref[...], wup_ref[:, p * slab:(p + 1) * slab])

    def xm_store(p, xm_all):
        cols = slice(p * slab, (p + 1) * slab)
        xm_ref[0, :, cols] = xm_all[halo:halo + t].astype(BF16)
        for q in range(per_slab):
            cb = per_slab * p + q
            lanes = slice(q * V7X_LANES, (q + 1) * V7X_LANES)
            xm_all_ref[cb, 0:halo, :] = xm_all[0:halo, lanes] * keep_prev
            xm_all_ref[cb, halo:halo + t, :] = xm_all[halo:halo + t, lanes]
            xm_all_ref[cb, halo + t:, :] = xm_all[halo + t:, lanes] * keep_next

    def conv_block(cb):
        lanes = slice(cb * V7X_LANES, (cb + 1) * V7X_LANES)
        for r0 in range(0, t, CONV_ROW_BLOCK):
            acc = jnp.zeros((CONV_ROW_BLOCK, V7X_LANES), F32)
            for k in range(QKV_CONV_WIDTH):
                start = halo + r0 - left + k
                acc = acc + xm_all_ref[cb, start:start + CONV_ROW_BLOCK, :] * wconv_ref[k:k + 1, lanes]
            xc_ref[0, r0:r0 + CONV_ROW_BLOCK, lanes] = _silu(acc + bconv_ref[:, lanes]).astype(BF16)

    nslab = e // slab
    for p in range(nslab):
        xm_store(p, xm_matmul(p))
    for p in range(nslab):
        cols = slice(e + p * slab, e + (p + 1) * slab)
        z_ref[0, :, p * slab:(p + 1) * slab] = _dot(u_ref[halo:halo + t, :], wup_ref[:, cols]).astype(BF16)
        for q in range(per_slab):
            conv_block(per_slab * p + q)


def _mlstm_up(h, mod, w_up, w_conv, b_conv):
    bsz, s, d = h.shape
    e = w_up.shape[1] // 2
    t = _token_tile(s, WIDE_TOKEN_TILE)
    halo = V7X_BF16_ROWS
    assert t % halo == 0 and t % CONV_ROW_BLOCK == 0
    nhb = s // halo
    taps = w_conv.shape[0]
    w_conv = jnp.pad(w_conv, ((0, V7X_SUBLANES - taps), (0, 0)))
    tile = pl.BlockSpec((1, t, e), lambda b, j: (b, j, 0))
    out = jax.ShapeDtypeStruct((bsz, s, e), BF16)
    return pl.pallas_call(
        _mlstm_up_kernel,
        grid=(bsz, s // t),
        in_specs=[pl.BlockSpec((1, t, d), lambda b, j: (b, j, 0)),
                  pl.BlockSpec((1, halo, d), lambda b, j: (b, jnp.maximum(j * (t // halo) - 1, 0), 0)),
                  pl.BlockSpec((1, halo, d), lambda b, j: (b, jnp.minimum((j + 1) * (t // halo), nhb - 1), 0)),
                  pl.BlockSpec((1, MOD_ROWS, d), lambda b, j: (b, 0, 0)),
                  _resident((d, 2 * e)), _resident((V7X_SUBLANES, e)), _resident((1, e))],
        out_specs=[tile, tile, tile],
        out_shape=[out, out, out],
        scratch_shapes=[pltpu.VMEM((t + 2 * halo, d), BF16),
                        pltpu.VMEM((e // V7X_LANES, t + 2 * halo, V7X_LANES), F32)],
        compiler_params=_params("parallel", "parallel"),
        name="mlstm_up",
    )(h, h, h, mod, w_up.astype(BF16), w_conv, _row(b_conv))


def _log_sigmoid(x):
    return jnp.minimum(x, 0.0) - jnp.log1p(jnp.exp(-jnp.abs(x)))


def _split_bf16x3(x):
    hi = x.astype(BF16)
    r1 = x - hi.astype(F32)
    mid = r1.astype(BF16)
    lo = (r1 - mid.astype(F32)).astype(BF16)
    return jnp.concatenate([hi, mid, lo], axis=1)


def _mlstm_qkv_kernel(*refs, k_scale, emit_intra):
    refs = list(refs)
    xc_ref, xm_ref, wq_ref, wk_ref, wv_ref, wg_ref, bg_ref, q_ref, kt_ref, v_ref, g_ref = refs[:11]
    h0_ref = refs[11] if emit_intra else None
    e = wq_ref.shape[0]
    t = xc_ref.shape[1]
    nh = MLSTM_HEADS
    dh = e // nh
    half = V7X_LANES // 2
    xc = xc_ref[0]
    q = _dot(xc, wq_ref[...]).astype(BF16)
    q_ref[0] = q
    gates = _dot(q, wg_ref[0:e, :])
    k = _dot(xc, wk_ref[...])
    kt_ref[0, 0] = (k * k_scale).T.astype(BF16)
    gates = gates + _dot(k.astype(BF16), wg_ref[e:2 * e, :])
    v = _dot(xm_ref[0], wv_ref[...]).astype(BF16)
    v_ref[0] = v
    gates = gates + _dot(v, wg_ref[2 * e:, :]) + bg_ref[...]

    lane = lax.broadcasted_iota(jnp.int32, (1, V7X_LANES), 1)
    group = (lane % GATE_PACK_LANES) // nh
    rows = lax.broadcasted_iota(jnp.int32, (t, t), 0)
    cols = lax.broadcasted_iota(jnp.int32, (t, t), 1)
    visible = (rows >= cols, rows <= cols)
    lf = _log_sigmoid(gates)
    parts = _split_bf16x3(lf)
    fold = lambda c3: c3[:, :V7X_LANES] + c3[:, V7X_LANES:2 * V7X_LANES] + c3[:, 2 * V7X_LANES:]
    prefix = fold(_dot(jnp.where(visible[0], 1.0, 0.0).astype(BF16), parts))
    suffix = fold(_dot(jnp.where(visible[1], 1.0, 0.0).astype(BF16), parts))
    b = jnp.where(lane < GATE_PACK_LANES, prefix, suffix)
    a = pltpu.roll(gates, half, 1) - b
    total = jnp.sum(lf, axis=0, keepdims=True)
    amax = jnp.max(a, axis=0, keepdims=True)
    pack = jnp.where(group == 0, b, jnp.where(group == 2, a, jnp.where(group == 4, total,
                                                                         jnp.where(group == 5, amax, 0.0))))
    if emit_intra:
        a_rows = a.T
        qk_of = lambda hd: _dot(q[:, hd * dh:(hd + 1) * dh], kt_ref[0, 0, hd * dh:(hd + 1) * dh, :])
        qk_next = qk_of(0)
        for hd in range(nh):
            lanes = slice(hd * dh, (hd + 1) * dh)
            qk = qk_next
            if hd + 1 < nh:
                qk_next = qk_of(hd + 1)
            for d in range(2):
                base = d * GATE_PACK_LANES + hd
                a_row = a_rows[base + 2 * nh:base + 2 * nh + 1, :]
                masked = jnp.where(visible[d], a_row, NEG_BIG)
                cmax = jnp.max(masked, axis=1, keepdims=True)
                scores = qk * jnp.exp(masked - cmax)
                rowsum = jnp.sum(scores, axis=1, keepdims=True)
                h0_ref[d, 0, :, lanes] = _dot(scores.astype(BF16), v[:, lanes]).astype(BF16)
                pack = jnp.where(lane == base + nh, cmax, jnp.where(lane == base + 3 * nh, rowsum, pack))
    g_ref[0] = pack[:, :g_ref.shape[2]]


def _gate_lane_layout():
    nh = MLSTM_HEADS
    src = np.full((V7X_LANES,), -1, np.int32)
    for d in range(2):
        for g in range(GATE_PACK_LANES // nh):
            for h in range(nh):
                lane = d * GATE_PACK_LANES + g * nh + h
                if g in (0, 2, 4, 5):
                    src[lane] = (2 * d + 1) * nh + h
                if g in (2, 5):
                    src[V7X_LANES // 2 + lane] = 2 * d * nh + h
    return src


def _mlstm_qkv(xc, xm, w_q, w_k, w_v, w_gate, b_gate, emit_intra):
    bsz, s, e = xc.shape
    assert 2 * GATE_PACK_LANES == V7X_LANES // 2
    t = _token_tile(s, SCAN_CHUNK)
    k_scale = float(e // MLSTM_HEADS) ** -0.5
    src = _gate_lane_layout()
    lay = lambda w: jnp.where(src >= 0, w[:, np.maximum(src, 0)], 0.0)
    tile = pl.BlockSpec((1, t, e), lambda b, j: (b, j, 0))
    out = jax.ShapeDtypeStruct((bsz, s, e), BF16)
    out_specs = [tile, pl.BlockSpec((1, 1, e, t), lambda b, j: (b, j, 0, 0)), tile,
                 pl.BlockSpec((1, t, 2 * GATE_PACK_LANES), lambda b, j: (b, j, 0))]
    out_shape = [out, jax.ShapeDtypeStruct((bsz, s // t, e, t), BF16), out,
                 jax.ShapeDtypeStruct((bsz, s, 2 * GATE_PACK_LANES), F32)]
    if emit_intra:
        out_specs.append(pl.BlockSpec((2, 1, t, e), lambda b, j: (0, b, j, 0)))
        out_shape.append(jax.ShapeDtypeStruct((2, bsz, s, e), BF16))
    outs = pl.pallas_call(
        functools.partial(_mlstm_qkv_kernel, k_scale=k_scale, emit_intra=emit_intra),
        grid=(bsz, s // t),
        in_specs=[tile, tile, _resident((e, e)), _resident((e, e)), _resident((e, e)),
                  _resident((3 * e, V7X_LANES)), _resident((1, V7X_LANES))],
        out_specs=out_specs,
        out_shape=out_shape,
        compiler_params=_params("parallel", "parallel"),
        name="mlstm_qkv",
    )(xc, xm, w_q.astype(BF16), w_k.astype(BF16), w_v.astype(BF16), lay(w_gate).astype(BF16),
      lay(_row(b_gate)))
    return tuple(outs) if emit_intra else tuple(outs) + (None,)


def _mlstm_scan_kernel(*refs, has_init, emit_h, emit_state):
    refs = list(refs)
    q_ref, kt_ref, v_ref, g_ref, gt_ref = refs[:5]
    del refs[:5]
    if emit_h:
        h0_ref = refs.pop(0)
    if has_init:
        c0_ref, n0_ref, m0_ref = refs[:3]
        del refs[:3]
    if emit_h:
        o_ref = refs.pop(0)
    if emit_state:
        cf_ref, nf_ref, mf_ref = refs[:3]
        del refs[:3]
    c_ref, n_ref, m_ref = refs

    length = kt_ref.shape[3]
    chunks = kt_ref.shape[1]
    nh, dh = c_ref.shape[0], c_ref.shape[1]
    backward = pl.program_id(1) == 1
    step = pl.program_id(2)

    @pl.when(step == 0)
    def _():
        if has_init:
            c_ref[...] = c0_ref[0, 0]
            n_ref[...] = n0_ref[0, 0]
            m_ref[...] = m0_ref[0, 0]
        else:
            c_ref[...] = jnp.zeros(c_ref.shape, F32)
            n_ref[...] = jnp.zeros(n_ref.shape, F32)
            m_ref[...] = jnp.zeros(m_ref.shape, F32)

    ones_rows = jnp.ones((V7X_SUBLANES, length), BF16)
    contract_lanes = (((1,), (1,)), ((), ()))

    def one_chunk(i, carry):
        sub = jnp.where(backward, chunks - 1 - i, i)
        rows = pl.ds(pl.multiple_of(sub * length, length), length)
        g = g_ref[0, 0, rows, :]
        gt = gt_ref[0, 0, sub]
        grp = lambda i: g[:, i * nh:(i + 1) * nh]
        grp_t = lambda i: gt[i * nh:(i + 1) * nh, :]
        m_old = m_ref[nh:nh + 1, 0:nh]
        m_old_t = m_ref[0:nh, 0:1]
        total, amax = grp(4)[0:1], grp(5)[0:1]
        m_new = total + jnp.maximum(m_old, amax)
        decay = jnp.exp(total + m_old - m_new)
        m_new_t = grp_t(4) + jnp.maximum(m_old_t, grp_t(5))
        w_k = jnp.exp(grp_t(4) + grp_t(2) - m_new_t)
        if emit_h:
            b_t, cmax, rowsum = grp(0), grp(1), grp(3)
            m_row = jnp.maximum(m_old, cmax)
            r_intra = jnp.exp(cmax - m_row)
            w_inter = jnp.exp(m_old - m_row)
            den_floor = jnp.exp(-(b_t + m_row))

            for h in range(nh):
                lanes = slice(h * dh, (h + 1) * dh)
                col = slice(h, h + 1)
                q = q_ref[0, rows, lanes]
                q_n = lax.dot_general(q, n_ref[h].astype(BF16), contract_lanes,
                                      preferred_element_type=F32)[:, 0:1]
                num = (r_intra[:, col] * h0_ref[0, 0, rows, lanes].astype(F32)
                       + w_inter[:, col] * _dot(q, c_ref[h].astype(BF16)))
                den = r_intra[:, col] * rowsum[:, col] + w_inter[:, col] * q_n
                o_ref[0, 0, rows, lanes] = (num / jnp.maximum(jnp.abs(den), den_floor[:, col])).astype(o_ref.dtype)

        for h in range(nh):
            lanes = slice(h * dh, (h + 1) * dh)
            col = slice(h, h + 1)
            wkt = (kt_ref[0, sub, lanes, :].astype(F32) * w_k[h:h + 1, :]).astype(BF16)
            c_ref[h] = decay[:, col] * c_ref[h] + _dot(wkt, v_ref[0, rows, lanes])
            n_ref[h] = decay[:, col] * n_ref[h] + lax.dot_general(ones_rows, wkt, contract_lanes,
                                                                  preferred_element_type=F32)
        m_ref[0:nh, :] = m_new_t[:, 0:V7X_LANES]
        m_ref[nh:nh + 1, 0:nh] = m_new
        return carry

    lax.fori_loop(0, chunks, one_chunk, 0)

    if emit_state:
        @pl.when(step == pl.num_programs(2) - 1)
        def _():
            cf_ref[0, 0] = c_ref[...]
            nf_ref[0, 0] = n_ref[...]
            mf_ref[0, 0] = m_ref[...]


def _mlstm_scan(q, kt, v, gates, h0, init_state, emit_state):
    bsz, s, e = q.shape
    nh = MLSTM_HEADS
    dh = e // nh
    emit_h = h0 is not None
    length = _token_tile(s, SCAN_CHUNK)
    nch = s // length
    per_step = SCAN_CHUNKS_PER_STEP if nch % SCAN_CHUNKS_PER_STEP == 0 else 1
    nst = nch // per_step
    span = per_step * length
    g = gates.reshape(bsz, s, 2, GATE_PACK_LANES).transpose(0, 2, 1, 3)
    g_t = g.reshape(bsz, 2, nch, length, GATE_PACK_LANES).transpose(0, 1, 2, 4, 3)
    chunk = lambda d, st: jnp.where(d == 0, st, nst - 1 - st)
    tile = pl.BlockSpec((1, span, e), lambda b, d, st: (b, chunk(d, st), 0))
    dir_tile = pl.BlockSpec((1, 1, span, e), lambda b, d, st: (d, b, chunk(d, st), 0))
    state_shapes = [(nh, dh, dh), (nh, V7X_SUBLANES, dh), (V7X_SUBLANES, V7X_LANES)]
    state_specs = [pl.BlockSpec((1, 1) + shp, lambda b, d, st, n=len(shp): (b, d) + (0,) * n)
                   for shp in state_shapes]
    scratch = [pltpu.VMEM(shp, F32) for shp in state_shapes]

    in_specs = [tile, pl.BlockSpec((1, per_step, e, length), lambda b, d, st: (b, chunk(d, st), 0, 0)), tile,
                pl.BlockSpec((1, 1, span, GATE_PACK_LANES), lambda b, d, st: (b, d, chunk(d, st), 0)),
                pl.BlockSpec((1, 1, per_step, GATE_PACK_LANES, length),
                             lambda b, d, st: (b, d, chunk(d, st), 0, 0))]
    args = [q, kt, v, g, g_t]
    if emit_h:
        in_specs.append(dir_tile)
        args.append(h0)
    if init_state is not None:
        in_specs += state_specs
        args += list(init_state)
    out_specs, out_shape = [], []
    if emit_h:
        out_specs.append(dir_tile)
        out_shape.append(jax.ShapeDtypeStruct((2, bsz, s, e), BF16))
    if emit_state:
        out_specs += state_specs
        out_shape += [jax.ShapeDtypeStruct((bsz, 2) + shp, F32) for shp in state_shapes]
    outs = pl.pallas_call(
        functools.partial(_mlstm_scan_kernel, has_init=init_state is not None, emit_h=emit_h,
                          emit_state=emit_state),
        grid=(bsz, 2, nst),
        in_specs=in_specs,
        out_specs=out_specs,
        out_shape=out_shape,
        scratch_shapes=scratch,
        compiler_params=_params("parallel", "parallel", "arbitrary"),
        name="mlstm_scan",
    )(*args)
    outs = list(outs)
    h = outs.pop(0) if emit_h else None
    return h, (tuple(outs) if emit_state else None)


def _mlstm_out_kernel(h_ref, xm_ref, xc_ref, z_ref, hs_ref, mod_ref, wo_ref, bo_ref, gn_ref, skip_ref,
                      wdown_ref, lng_ref, lnb_ref, o_ref, *, alpha):
    e = xm_ref.shape[2]
    dh = e // MLSTM_HEADS
    sub_rows = NARROW_TOKEN_TILE
    g1 = mod_ref[0, 2:3, :]

    def sub_tile(i, carry):
        rows = pl.ds(pl.multiple_of(i * sub_rows, sub_rows), sub_rows)
        h = h_ref[0, rows, :]
        xm = xm_ref[0, rows, :]
        acc = jnp.zeros(h.shape, F32)

        def gate_logits(hd):
            lanes = slice(hd * dh, (hd + 1) * dh)
            lanes_b = slice(e + hd * dh, e + (hd + 1) * dh)
            return (_dot(xm, wo_ref[:, lanes]) + bo_ref[:, lanes],
                    _dot(xm, wo_ref[:, lanes_b]) + bo_ref[:, lanes_b])

        logits = gate_logits(0)
        for hd in range(MLSTM_HEADS):
            lanes = slice(hd * dh, (hd + 1) * dh)
            o_f, o_b = jax.nn.sigmoid(logits[0]), jax.nn.sigmoid(logits[1])
            if hd + 1 < MLSTM_HEADS:
                logits = gate_logits(hd + 1)
            hsum = (o_f * hs_ref[0, 0, rows, lanes].astype(F32)
                    + o_b * hs_ref[1, 0, rows, lanes].astype(F32))
            mu = jnp.mean(hsum, axis=-1, keepdims=True)
            dlt = hsum - mu
            var = jnp.mean(dlt * dlt, axis=-1, keepdims=True)
            hn = dlt * lax.rsqrt(var + LN_EPS) * gn_ref[:, lanes]
            y = ((hn + skip_ref[:, lanes] * xc_ref[0, rows, lanes].astype(F32))
                 * _silu(z_ref[0, rows, lanes].astype(F32)))
            acc = acc + _dot(y.astype(BF16), wdown_ref[lanes, :])
        o_ref[0, rows, :] = _layer_norm(alpha * h + g1 * acc, lng_ref[...], lnb_ref[...])
        return carry

    lax.fori_loop(0, h_ref.shape[1] // sub_rows, sub_tile, 0)


def _mlstm_out(h, xm, xc, z, hs, mod, w_o, b_o, gn_g, skip, w_down, ln_g, ln_b, alpha):
    bsz, s, d = h.shape
    e = xm.shape[2]
    t = _token_tile(s, WIDE_TOKEN_TILE)
    assert t % NARROW_TOKEN_TILE == 0
    tile = lambda width: pl.BlockSpec((1, t, width), lambda b, j: (b, j, 0))
    return pl.pallas_call(
        functools.partial(_mlstm_out_kernel, alpha=alpha),
        grid=(bsz, s // t),
        in_specs=[tile(d), tile(e), tile(e), tile(e),
                  pl.BlockSpec((2, 1, t, e), lambda b, j: (0, b, j, 0)),
                  pl.BlockSpec((1, MOD_ROWS, d), lambda b, j: (b, 0, 0)),
                  _resident((e, 2 * e)), _resident((1, 2 * e)), _resident((1, e)), _resident((1, e)),
                  _resident((e, d)), _resident((1, d)), _resident((1, d))],
        out_specs=pl.BlockSpec((1, t, d), lambda b, j: (b, j, 0)),
        out_shape=jax.ShapeDtypeStruct((bsz, s, d), F32),
        compiler_params=_params("parallel", "parallel"),
        name="mlstm_out",
    )(h, xm, xc, z, hs, mod, w_o.astype(BF16), _row(b_o), _row(gn_g), _row(skip),
      w_down.astype(BF16), _row(ln_g), _row(ln_b))


def _mod_tables(ada_out_i, bsz, d):
    m = ada_out_i.reshape(MOD_ROWS, 6, d)
    pad = ((0, 0), (0, MOD_ROWS - 6), (0, 0))
    lat = jnp.pad(m[:bsz], pad)
    ctx = jnp.pad(jnp.broadcast_to(m[bsz][None], (bsz, 6, d)), pad)
    return lat, ctx


def kernel(x, c, ctx, c_ctx, ada_w, ada_b, ln1_g, ln1_b, ln2_g, ln2_b, mlp_w1, mlp_w2, conv_w_pw1, conv_b_pw1, conv_w_dw, conv_b_dw, conv_ln_g, conv_ln_b, conv_w_pw2, conv_b_pw2, ml_w_up, ml_w_conv, ml_b_conv, ml_w_q, ml_w_k, ml_w_v, ml_w_gate, ml_b_gate, ml_w_o, ml_b_o, ml_gn_g, ml_skip, ml_w_down):
    bsz, s, d = x.shape
    depth = ada_w.shape[0]
    assert depth == 2 and bsz + 1 <= MOD_ROWS, "built for the two-layer block: conv mixer, then mLSTM"
    alpha = (2 * depth) ** 0.25

    cvec = jnp.concatenate([c, c_ctx[None], jnp.zeros((MOD_ROWS - bsz - 1, d), F32)], axis=0)
    ada = _adaln(cvec, ada_w, ada_b)
    mod0, mod0_ctx = _mod_tables(ada[0], bsz, d)
    mod1, mod1_ctx = _mod_tables(ada[1], bsz, d)

    conv_p = dict(w_pw1=conv_w_pw1[0], b_pw1=conv_b_pw1[0], w_dw=conv_w_dw[0], b_dw=conv_b_dw[0],
                  ln_g=conv_ln_g[0], ln_b=conv_ln_b[0], w_pw2=conv_w_pw2[0], b_pw2=conv_b_pw2[0],
                  ln1_g=ln1_g[0], ln1_b=ln1_b[0])
    h = _conv_mixer_layer(x, GRID_W, mod0, conv_p, alpha)
    h = _mlp_layer(h, mod0, mlp_w1[0], mlp_w2[0], ln2_g[0], ln2_b[0], alpha)
    hc = _conv_mixer_layer(ctx, ctx.shape[1], mod0_ctx, conv_p, alpha)
    hc = _mlp_layer(hc, mod0_ctx, mlp_w1[0], mlp_w2[0], ln2_g[0], ln2_b[0], alpha)

    feats = lambda hh, mm: _mlstm_up(hh, mm, ml_w_up[0], ml_w_conv[0], ml_b_conv[0])
    qkv = lambda xc_, xm_, intra: _mlstm_qkv(xc_, xm_, ml_w_q[0], ml_w_k[0], ml_w_v[0], ml_w_gate[0],
                                             ml_b_gate[0], emit_intra=intra)
    xm_c, xc_c, _ = feats(hc, mod1_ctx)
    _, ctx_state = _mlstm_scan(*qkv(xc_c, xm_c, False), init_state=None, emit_state=True)
    xm, xc, z = feats(h, mod1)
    hs, _ = _mlstm_scan(*qkv(xc, xm, True), init_state=ctx_state, emit_state=False)
    h = _mlstm_out(h, xm, xc, z, hs, mod1, ml_w_o[0], ml_b_o[0], ml_gn_g[0], ml_skip[0],
                   ml_w_down[0], ln1_g[1], ln1_b[1], alpha)
    return _mlp_layer(h, mod1, mlp_w1[1], mlp_w2[1], ln2_g[1], ln2_b[1], alpha)
```
